```python
import math
import jax, jax.numpy as jnp
from jax import lax
import numpy as np

D_MODEL = 1024
BATCH = 8
SEQ = 2048
DEPTH = 2
DEC_BATCH = 128
DEC_SEQ = 1
PAST_LEN = 16384
PAGE_SIZE = 128

CONV_WIDTH = 3
CONV_A_WIDTH = D_MODEL // 2
SSM_WIDTH = D_MODEL // 2
SSM_GROUP = 16
SSM_GROUPS = SSM_WIDTH // SSM_GROUP
SSM_STATE = 64
D_FF = 2816
PLE_DIM = 256
NORM_EPS = 1e-6
DT_MIN = 1e-3
DT_MAX = 1e-1
IN_PROJ_WIDTH = 3 * CONV_A_WIDTH + SSM_WIDTH + 2 * D_MODEL
SPLITS = (CONV_A_WIDTH, 2 * CONV_A_WIDTH, 3 * CONV_A_WIDTH,
          3 * CONV_A_WIDTH + SSM_WIDTH, 3 * CONV_A_WIDTH + SSM_WIDTH + D_MODEL)

kernel_name = "hybrid_shortconv_s5_convffn_step"


def rms_norm(x, g):
    xf = x.astype(jnp.float32)
    y = xf * lax.rsqrt(jnp.mean(xf * xf, axis=-1, keepdims=True) + NORM_EPS)
    return (y * g.astype(jnp.float32)).astype(x.dtype)


def causal_dwconv(u, buf, w, b):
    t = u.shape[1]
    ext = jnp.concatenate([buf.astype(u.dtype), u], axis=1)
    y = b + sum(ext[:, k:k + t] * w[k] for k in range(CONV_WIDTH))
    return y, ext[:, -(CONV_WIDTH - 1):]


def _complex_linear_combine(left, right):
    a1r, a1i, b1r, b1i = left
    a2r, a2i, b2r, b2i = right
    return (a2r * a1r - a2i * a1i,
            a2r * a1i + a2i * a1r,
            a2r * b1r - a2i * b1i + b2r,
            a2r * b1i + a2i * b1r + b2i)


def s5_mimo(u, s0_re, s0_im, log_dt, lam_re, lam_im, b_re, b_im, c_re, c_im, d_skip):
    bt, t, _ = u.shape
    ug = u.reshape(bt, t, SSM_GROUPS, SSM_GROUP)
    dt = jnp.exp(log_dt)[:, None]
    mag = jnp.exp(lam_re * dt)
    ang = lam_im * dt
    abar_re = mag * jnp.cos(ang)
    abar_im = mag * jnp.sin(ang)
    den = lam_re * lam_re + lam_im * lam_im
    nr = abar_re - 1
    coef_re = (nr * lam_re + abar_im * lam_im) / den
    coef_im = (abar_im * lam_re - nr * lam_im) / den
    bbar_re = coef_re[..., None] * b_re - coef_im[..., None] * b_im
    bbar_im = coef_re[..., None] * b_im + coef_im[..., None] * b_re
    bu_re = jnp.einsum('btgh,gph->btgp', ug, bbar_re)
    bu_im = jnp.einsum('btgh,gph->btgp', ug, bbar_im)
    a_re = jnp.broadcast_to(abar_re, (1, t) + abar_re.shape)
    a_im = jnp.broadcast_to(abar_im, (1, t) + abar_im.shape)
    cum_re, cum_im, h_re, h_im = lax.associative_scan(
        _complex_linear_combine, (a_re, a_im, bu_re, bu_im), axis=1)
    s0r = s0_re[:, None].astype(u.dtype)
    s0i = s0_im[:, None].astype(u.dtype)
    s_re = h_re + cum_re * s0r - cum_im * s0i
    s_im = h_im + cum_re * s0i + cum_im * s0r
    y = (jnp.einsum('btgp,ghp->btgh', s_re, c_re)
         - jnp.einsum('btgp,ghp->btgh', s_im, c_im)
         + d_skip.reshape(SSM_GROUPS, SSM_GROUP) * ug)
    return y.reshape(bt, t, SSM_WIDTH), s_re[:, -1], s_im[:, -1]


def hybrid_layer(x, p, conv_buf, s_re, s_im, ffn_buf,
                 norm_mix, w_in, conv_a_w, conv_a_b, w_out_a,
                 log_dt, lam_re, lam_im, b_re, b_im, c_re, c_im, d_skip,
                 w_glu_a, w_glu_b, w_o,
                 norm_ffn, w_up, ffn_conv_w, ffn_conv_b, w_down,
                 norm_ple, w_ple_gate, w_ple):
    xn = rms_norm(x, norm_mix)
    h, gate_b_in, gate_c_in, u, gate_a, gate_s = jnp.split(xn @ w_in, SPLITS, axis=-1)
    conv_y, new_conv_buf = causal_dwconv(gate_c_in * h, conv_buf, conv_a_w, conv_a_b)
    z_a = (gate_b_in * conv_y) @ w_out_a
    ssm_y, new_re, new_im = s5_mimo(u, s_re, s_im, log_dt, lam_re, lam_im,
                                    b_re, b_im, c_re, c_im, d_skip)
    ssm_y = jax.nn.gelu(ssm_y)
    z_b = (ssm_y @ w_glu_a) * jax.nn.sigmoid(ssm_y @ w_glu_b)
    x = x + (jax.nn.sigmoid(gate_a) * z_a + jax.nn.sigmoid(gate_s) * z_b) @ w_o
    up_a, up_b = jnp.split(rms_norm(x, norm_ffn) @ w_up, 2, axis=-1)
    up_a, new_ffn_buf = causal_dwconv(up_a, ffn_buf, ffn_conv_w, ffn_conv_b)
    x = x + (jax.nn.gelu(up_a) * up_b) @ w_down
    x = x + jax.nn.sigmoid(rms_norm(x, norm_ple) @ w_ple_gate) * (p @ w_ple)
    return x, new_conv_buf, new_re, new_im, new_ffn_buf


def run_trunk(x, p, conv_bufs, ssm_re, ssm_im, ffn_bufs,
              norm_mix, w_in, conv_a_w, conv_a_b, w_out_a,
              log_dt, lam_re, lam_im, b_re, b_im, c_re, c_im, d_skip,
              w_glu_a, w_glu_b, w_o,
              norm_ffn, w_up, ffn_conv_w, ffn_conv_b, w_down,
              norm_ple, w_ple_gate, w_ple, norm_final):
    new_conv, new_re, new_im, new_ffn = [], [], [], []
    for i in range(DEPTH):
        x, cb, sr, si, fb = hybrid_layer(
            x, p[i], conv_bufs[i], ssm_re[i], ssm_im[i], ffn_bufs[i],
            norm_mix[i], w_in[i], conv_a_w[i], conv_a_b[i], w_out_a[i],
            log_dt[i], lam_re[i], lam_im[i], b_re[i], b_im[i], c_re[i], c_im[i], d_skip[i],
            w_glu_a[i], w_glu_b[i], w_o[i],
            norm_ffn[i], w_up[i], ffn_conv_w[i], ffn_conv_b[i], w_down[i],
            norm_ple[i], w_ple_gate[i], w_ple[i])
        new_conv.append(cb)
        new_re.append(sr)
        new_im.append(si)
        new_ffn.append(fb)
    y = rms_norm(x, norm_final)
    return y, jnp.stack(new_conv), jnp.stack(new_re), jnp.stack(new_im), jnp.stack(new_ffn)


def setup_inputs(seed: int = 0) -> dict:
    key = jax.random.key(seed)
    ks = jax.random.split(key, 40)
    f32 = jnp.float32
    nrm = lambda k, shape, scale: scale * jax.random.normal(k, shape, f32)
    gain = lambda k, shape: 1.0 + 0.05 * jax.random.normal(k, shape, f32)
    n_idx = jnp.arange(SSM_STATE, dtype=f32)
    lam_re = -0.5 + 0.01 * jax.random.normal(ks[12], (DEPTH, SSM_GROUPS, SSM_STATE), f32)
    lam_im = math.pi * n_idx + 0.01 * jax.random.normal(ks[13], (DEPTH, SSM_GROUPS, SSM_STATE), f32)
    log_dt = jax.random.uniform(ks[11], (DEPTH, SSM_GROUPS), f32,
                                math.log(DT_MIN), math.log(DT_MAX))
    return {
        "x_prompt": nrm(ks[0], (BATCH, SEQ, D_MODEL), 1.0),
        "x_sample": nrm(ks[1], (DEC_BATCH, DEC_SEQ, D_MODEL), 1.0),
        "p_prompt": nrm(ks[2], (DEPTH, BATCH, SEQ, PLE_DIM), 1.0),
        "p_sample": nrm(ks[3], (DEPTH, DEC_BATCH, DEC_SEQ, PLE_DIM), 1.0),
        "state_conv_a": nrm(ks[4], (DEPTH, DEC_BATCH, CONV_WIDTH - 1, CONV_A_WIDTH), 1.0),
        "state_ssm_re": nrm(ks[5], (DEPTH, DEC_BATCH, SSM_GROUPS, SSM_STATE), 0.5),
        "state_ssm_im": nrm(ks[6], (DEPTH, DEC_BATCH, SSM_GROUPS, SSM_STATE), 0.5),
        "state_ffn_conv": nrm(ks[7], (DEPTH, DEC_BATCH, CONV_WIDTH - 1, D_FF), 1.0),
        "norm_mix": gain(ks[8], (DEPTH, D_MODEL)),
        "w_in": nrm(ks[9], (DEPTH, D_MODEL, IN_PROJ_WIDTH), D_MODEL ** -0.5),
        "conv_a_w": nrm(ks[10], (DEPTH, CONV_WIDTH, CONV_A_WIDTH), CONV_WIDTH ** -0.5),
        "conv_a_b": nrm(ks[14], (DEPTH, CONV_A_WIDTH), 0.01),
        "w_out_a": nrm(ks[15], (DEPTH, CONV_A_WIDTH, D_MODEL), CONV_A_WIDTH ** -0.5),
        "log_dt": log_dt,
        "lam_re": lam_re,
        "lam_im": lam_im,
        "b_re": nrm(ks[16], (DEPTH, SSM_GROUPS, SSM_STATE, SSM_GROUP), (2 * SSM_GROUP) ** -0.5),
        "b_im": nrm(ks[17], (DEPTH, SSM_GROUPS, SSM_STATE, SSM_GROUP), (2 * SSM_GROUP) ** -0.5),
        "c_re": nrm(ks[18], (DEPTH, SSM_GROUPS, SSM_GROUP, SSM_STATE), (2 * SSM_STATE) ** -0.5),
        "c_im": nrm(ks[19], (DEPTH, SSM_GROUPS, SSM_GROUP, SSM_STATE), (2 * SSM_STATE) ** -0.5),
        "d_skip": nrm(ks[20], (DEPTH, SSM_WIDTH), 0.5),
        "w_glu_a": nrm(ks[21], (DEPTH, SSM_WIDTH, D_MODEL), SSM_WIDTH ** -0.5),
        "w_glu_b": nrm(ks[22], (DEPTH, SSM_WIDTH, D_MODEL), SSM_WIDTH ** -0.5),
        "w_o": nrm(ks[23], (DEPTH, D_MODEL, D_MODEL), D_MODEL ** -0.5),
        "norm_ffn": gain(ks[24], (DEPTH, D_MODEL)),
        "w_up": nrm(ks[25], (DEPTH, D_MODEL, 2 * D_FF), D_MODEL ** -0.5),
        "ffn_conv_w": nrm(ks[26], (DEPTH, CONV_WIDTH, D_FF), CONV_WIDTH ** -0.5),
        "ffn_conv_b": nrm(ks[27], (DEPTH, D_FF), 0.01),
        "w_down": nrm(ks[28], (DEPTH, D_FF, D_MODEL), D_FF ** -0.5),
        "norm_ple": gain(ks[29], (DEPTH, D_MODEL)),
        "w_ple_gate": nrm(ks[30], (DEPTH, D_MODEL, D_MODEL), D_MODEL ** -0.5),
        "w_ple": nrm(ks[31], (DEPTH, PLE_DIM, D_MODEL), PLE_DIM ** -0.5),
        "norm_final": gain(ks[32], (D_MODEL,)),
    }


def reference(x_prompt, x_sample, p_prompt, p_sample,
              state_conv_a, state_ssm_re, state_ssm_im, state_ffn_conv,
              norm_mix, w_in, conv_a_w, conv_a_b, w_out_a,
              log_dt, lam_re, lam_im, b_re, b_im, c_re, c_im, d_skip,
              w_glu_a, w_glu_b, w_o,
              norm_ffn, w_up, ffn_conv_w, ffn_conv_b, w_down,
              norm_ple, w_ple_gate, w_ple, norm_final):
    weights = (norm_mix, w_in, conv_a_w, conv_a_b, w_out_a,
               log_dt, lam_re, lam_im, b_re, b_im, c_re, c_im, d_skip,
               w_glu_a, w_glu_b, w_o,
               norm_ffn, w_up, ffn_conv_w, ffn_conv_b, w_down,
               norm_ple, w_ple_gate, w_ple, norm_final)
    bp = x_prompt.shape[0]
    dt_ = x_prompt.dtype
    zero_conv = jnp.zeros((DEPTH, bp, CONV_WIDTH - 1, CONV_A_WIDTH), dt_)
    zero_re = jnp.zeros((DEPTH, bp, SSM_GROUPS, SSM_STATE), dt_)
    zero_im = jnp.zeros((DEPTH, bp, SSM_GROUPS, SSM_STATE), dt_)
    zero_ffn = jnp.zeros((DEPTH, bp, CONV_WIDTH - 1, D_FF), dt_)
    y_prompt, conv_a_p, ssm_re_p, ssm_im_p, ffn_p = run_trunk(
        x_prompt, p_prompt, zero_conv, zero_re, zero_im, zero_ffn, *weights)
    y_sample, conv_a_s, ssm_re_s, ssm_im_s, ffn_s = run_trunk(
        x_sample, p_sample, state_conv_a, state_ssm_re, state_ssm_im, state_ffn_conv, *weights)
    return (y_prompt, y_sample, conv_a_p, ssm_re_p, ssm_im_p, ffn_p,
            conv_a_s, ssm_re_s, ssm_im_s, ffn_s)
```

```python
import functools
import math

import jax
import jax.numpy as jnp
from jax import lax
from jax.experimental import pallas as pl
from jax.experimental.pallas import tpu as pltpu

D_MODEL = 1024
DEPTH = 2
CONV_WIDTH = 3
CONV_A_WIDTH = 512
SSM_WIDTH = 512
SSM_GROUP = 16
SSM_GROUPS = 32
SSM_STATE = 64
SSM_FLAT = SSM_GROUPS * SSM_STATE
D_FF = 2816
PLE_DIM = 256
NORM_EPS = 1e-6
IN_PROJ_WIDTH = 4096

SSM_CHUNKS = 4
CHUNK_CH = SSM_WIDTH // SSM_CHUNKS
CHUNK_ST = SSM_FLAT // SSM_CHUNKS

FF_CHUNKS = ((0, 1024), (1024, 2048), (2048, 2816))
FF_CHUNK_MAX = 1024

PROMPT_TIME_BLOCK = 64
VMEM_LIMIT_BYTES = 56 * 1024 * 1024

_BF16 = jnp.bfloat16
_F32 = jnp.float32


def _dot(a, b):
    return jnp.dot(a, b, preferred_element_type=_F32)


def _dot_nt(a, b):
    return lax.dot_general(a, b, (((1,), (1,)), ((), ())), preferred_element_type=_F32)


def _rms(x, g_ref):
    ms = jnp.mean(x * x, axis=-1, keepdims=True)
    return x * lax.rsqrt(ms + NORM_EPS) * g_ref[...]


def _gelu(x):
    c = math.sqrt(2.0 / math.pi)
    return x * (0.5 * (1.0 + jnp.tanh(c * (x + 0.044715 * (x * x * x)))))


def _sigmoid(x):
    return 0.5 * jnp.tanh(0.5 * x) + 0.5


def _ssm_prep_kernel(logdt_ref, lre_ref, lim_ref, bre_ref, bim_ref, cre_ref, cim_ref,
                     abar_re_ref, abar_im_ref, bblk_ref, cblk_re_ref, cblk_im_ref):
    dt = jnp.exp(logdt_ref[...])
    lre = lre_ref[...]
    lim = lim_ref[...]
    mag = jnp.exp(lre * dt)
    ang = lim * dt
    ar = mag * jnp.cos(ang)
    ai = mag * jnp.sin(ang)
    abar_re_ref[...] = ar
    abar_im_ref[...] = ai
    den = lre * lre + lim * lim
    nr = ar - 1.0
    coef_re = (nr * lre + ai * lim) / den
    coef_im = (ai * lre - nr * lim) / den
    bre = bre_ref[...]
    bim = bim_ref[...]
    bbar_re = coef_re * bre - coef_im * bim
    bbar_im = coef_re * bim + coef_im * bre
    row = lax.broadcasted_iota(jnp.int32, (CHUNK_CH, CHUNK_ST), 0)
    lane = lax.broadcasted_iota(jnp.int32, (CHUNK_CH, CHUNK_ST), 1)
    same_group = (row // SSM_GROUP) == (lane // SSM_STATE)
    reps = CHUNK_CH // SSM_GROUP
    for c in range(SSM_CHUNKS):
        cols = slice(c * CHUNK_ST, (c + 1) * CHUNK_ST)
        tre = jnp.concatenate([bbar_re[:, cols]] * reps, axis=0)
        tim = jnp.concatenate([bbar_im[:, cols]] * reps, axis=0)
        bblk_ref[c, :, 0:CHUNK_ST] = jnp.where(same_group, tre, 0.0).astype(_BF16)
        bblk_ref[c, :, CHUNK_ST:2 * CHUNK_ST] = jnp.where(same_group, tim, 0.0).astype(_BF16)
        rows = slice(c * CHUNK_CH, (c + 1) * CHUNK_CH)
        cblk_re_ref[c] = jnp.where(same_group, cre_ref[rows, :], 0.0).astype(_BF16)
        cblk_im_ref[c] = jnp.where(same_group, -cim_ref[rows, :], 0.0).astype(_BF16)


def _ssm_prep(log_dt, lam_re, lam_im, b_re, b_im, c_re, c_im):
    logdt = jnp.repeat(log_dt, SSM_STATE, axis=1).reshape(DEPTH, 1, SSM_FLAT)
    lre = lam_re.reshape(DEPTH, 1, SSM_FLAT)
    lim = lam_im.reshape(DEPTH, 1, SSM_FLAT)
    bre = b_re.transpose(0, 3, 1, 2).reshape(DEPTH, SSM_GROUP, SSM_FLAT)
    bim = b_im.transpose(0, 3, 1, 2).reshape(DEPTH, SSM_GROUP, SSM_FLAT)
    reps = CHUNK_ST // SSM_STATE
    cre = jnp.tile(c_re.reshape(DEPTH, SSM_WIDTH, SSM_STATE), (1, 1, reps))
    cim = jnp.tile(c_im.reshape(DEPTH, SSM_WIDTH, SSM_STATE), (1, 1, reps))

    def spec(*shape):
        return pl.BlockSpec((None,) + shape, lambda i: (i,) + (0,) * len(shape))

    return pl.pallas_call(
        _ssm_prep_kernel,
        grid=(DEPTH,),
        in_specs=[spec(1, SSM_FLAT), spec(1, SSM_FLAT), spec(1, SSM_FLAT),
                  spec(SSM_GROUP, SSM_FLAT), spec(SSM_GROUP, SSM_FLAT),
                  spec(SSM_WIDTH, CHUNK_ST), spec(SSM_WIDTH, CHUNK_ST)],
        out_specs=[spec(1, SSM_FLAT), spec(1, SSM_FLAT),
                   spec(SSM_CHUNKS, CHUNK_CH, 2 * CHUNK_ST),
                   spec(SSM_CHUNKS, CHUNK_CH, CHUNK_ST),
                   spec(SSM_CHUNKS, CHUNK_CH, CHUNK_ST)],
        out_shape=[jax.ShapeDtypeStruct((DEPTH, 1, SSM_FLAT), _F32),
                   jax.ShapeDtypeStruct((DEPTH, 1, SSM_FLAT), _F32),
                   jax.ShapeDtypeStruct((DEPTH, SSM_CHUNKS, CHUNK_CH, 2 * CHUNK_ST), _BF16),
                   jax.ShapeDtypeStruct((DEPTH, SSM_CHUNKS, CHUNK_CH, CHUNK_ST), _BF16),
                   jax.ShapeDtypeStruct((DEPTH, SSM_CHUNKS, CHUNK_CH, CHUNK_ST), _BF16)],
        name="ssm_prep",
    )(logdt, lre, lim, bre, bim, cre, cim)


def _mixer_kernel(x_ref, conv0_ref, sre0_ref, sim0_ref,
                  g_ref, w_in_ref, cw_ref, cb_ref, w_out_a_ref,
                  abar_re_ref, abar_im_ref, bblk_ref, cblk_re_ref, cblk_im_ref, dskip_ref,
                  glu_a_ref, glu_b_ref, w_o_ref,
                  out_ref, conv_out_ref, sre_out_ref, sim_out_ref,
                  vbuf, s_re, s_im, *, rb, tb):
    rows = rb * tb
    carry_rows = (CONV_WIDTH - 1) * rb

    @pl.when(pl.program_id(0) == 0)
    def _():
        conv_out_ref[...] = conv0_ref[...]
        sre_out_ref[...] = sre0_ref[...]
        sim_out_ref[...] = sim0_ref[...]

    x = x_ref[...]
    xn = _rms(x, g_ref).astype(_BF16)

    def proj(lo, hi):
        return _dot(xn, w_in_ref[:, lo:hi])

    h = proj(0, 512)
    gate_c = proj(1024, 1536)
    vbuf[0:carry_rows, :] = conv_out_ref[...]
    vbuf[carry_rows:carry_rows + rows, :] = gate_c * h
    conv_y = (cb_ref[...]
              + vbuf[0:rows, :] * cw_ref[0:1, :]
              + vbuf[rb:rb + rows, :] * cw_ref[1:2, :]
              + vbuf[2 * rb:2 * rb + rows, :] * cw_ref[2:3, :])
    conv_out_ref[...] = vbuf[rows:rows + carry_rows, :]
    gate_b = proj(512, 1024)
    z_a = _dot((gate_b * conv_y).astype(_BF16), w_out_a_ref[...])
    merged = _sigmoid(proj(2048, 3072)) * z_a

    u = proj(1536, 2048)
    ub = u.astype(_BF16)
    for c in range(SSM_CHUNKS):
        bu = _dot(ub[:, c * CHUNK_CH:(c + 1) * CHUNK_CH], bblk_ref[c])
        s_re[:, c * CHUNK_ST:(c + 1) * CHUNK_ST] = bu[:, 0:CHUNK_ST]
        s_im[:, c * CHUNK_ST:(c + 1) * CHUNK_ST] = bu[:, CHUNK_ST:2 * CHUNK_ST]

    for c in range(SSM_CHUNKS):
        cols = slice(c * CHUNK_ST, (c + 1) * CHUNK_ST)
        ar = jnp.broadcast_to(abar_re_ref[:, cols], (rb, CHUNK_ST))
        ai = jnp.broadcast_to(abar_im_ref[:, cols], (rb, CHUNK_ST))

        def step(t, carry, cols=cols, ar=ar, ai=ai):
            sr, si = carry
            r0 = pl.multiple_of(t * rb, rb)
            nsr = ar * sr - ai * si + s_re[pl.ds(r0, rb), cols]
            nsi = ar * si + ai * sr + s_im[pl.ds(r0, rb), cols]
            s_re[pl.ds(r0, rb), cols] = nsr
            s_im[pl.ds(r0, rb), cols] = nsi
            return nsr, nsi

        sr, si = lax.fori_loop(0, tb, step, (sre_out_ref[:, cols], sim_out_ref[:, cols]),
                               unroll=min(tb, 8))
        sre_out_ref[:, cols] = sr
        sim_out_ref[:, cols] = si

    ys = []
    for c in range(SSM_CHUNKS):
        cols = slice(c * CHUNK_ST, (c + 1) * CHUNK_ST)
        ys.append(_dot_nt(s_re[:, cols].astype(_BF16), cblk_re_ref[c])
                  + _dot_nt(s_im[:, cols].astype(_BF16), cblk_im_ref[c]))
    ssm_y = jnp.concatenate(ys, axis=1) + dskip_ref[...] * u
    sy = _gelu(ssm_y).astype(_BF16)
    z_b = _dot(sy, glu_a_ref[...]) * _sigmoid(_dot(sy, glu_b_ref[...]))
    merged = (merged + _sigmoid(proj(3072, 4096)) * z_b).astype(_BF16)
    out_ref[...] = x + _dot(merged, w_o_ref[...])


def _ffn_kernel(x_ref, p_ref, f0_ref,
                g2_ref, w_up_ref, fcw_ref, fcb_ref, w_down_ref,
                g3_ref, w_pg_ref, w_ple_ref, gf_ref,
                out_ref, fconv_out_ref,
                fbuf, *, rb, tb, final_norm):
    rows = rb * tb
    carry_rows = (CONV_WIDTH - 1) * rb

    @pl.when(pl.program_id(0) == 0)
    def _():
        fconv_out_ref[...] = f0_ref[...]

    x = x_ref[...]
    xn = _rms(x, g2_ref).astype(_BF16)
    acc = x
    for lo, hi in FF_CHUNKS:
        wc = hi - lo
        up_a = _dot(xn, w_up_ref[:, lo:hi])
        up_b = _dot(xn, w_up_ref[:, D_FF + lo:D_FF + hi])
        fbuf[0:carry_rows, 0:wc] = fconv_out_ref[:, lo:hi]
        fbuf[carry_rows:carry_rows + rows, 0:wc] = up_a
        conv_a = (fcb_ref[:, lo:hi]
                  + fbuf[0:rows, 0:wc] * fcw_ref[0:1, lo:hi]
                  + fbuf[rb:rb + rows, 0:wc] * fcw_ref[1:2, lo:hi]
                  + fbuf[2 * rb:2 * rb + rows, 0:wc] * fcw_ref[2:3, lo:hi])
        fconv_out_ref[:, lo:hi] = fbuf[rows:rows + carry_rows, 0:wc]
        hid = (_gelu(conv_a) * up_b).astype(_BF16)
        acc = acc + _dot(hid, w_down_ref[lo:hi, :])
    x2 = acc
    xn3 = _rms(x2, g3_ref).astype(_BF16)
    gate = _sigmoid(_dot(xn3, w_pg_ref[...]))
    pe = _dot(p_ref[...].astype(_BF16), w_ple_ref[...])
    x3 = x2 + gate * pe
    if final_norm:
        x3 = _rms(x3, gf_ref)
    out_ref[...] = x3


def _const_spec(shape, layer=None):
    if layer is None:
        return pl.BlockSpec(shape, lambda j: (0,) * len(shape), pipeline_mode=pl.Buffered(1))
    return pl.BlockSpec((None,) + shape, lambda j: (layer,) + (0,) * len(shape),
                        pipeline_mode=pl.Buffered(1))


def _mixer_call(x, conv0, sre0, sim0, wts, layer, rb, tb):
    n = x.shape[0]
    rows = rb * tb
    carry_rows = (CONV_WIDTH - 1) * rb
    row_spec = pl.BlockSpec((rows, D_MODEL), lambda j: (j, 0))
    in_specs = [
        row_spec,
        _const_spec((carry_rows, CONV_A_WIDTH)),
        _const_spec((rb, SSM_FLAT)),
        _const_spec((rb, SSM_FLAT)),
        _const_spec((1, D_MODEL), layer),
        _const_spec((D_MODEL, IN_PROJ_WIDTH), layer),
        _const_spec((CONV_WIDTH, CONV_A_WIDTH), layer),
        _const_spec((1, CONV_A_WIDTH), layer),
        _const_spec((CONV_A_WIDTH, D_MODEL), layer),
        _const_spec((1, SSM_FLAT), layer),
        _const_spec((1, SSM_FLAT), layer),
        _const_spec((SSM_CHUNKS, CHUNK_CH, 2 * CHUNK_ST), layer),
        _const_spec((SSM_CHUNKS, CHUNK_CH, CHUNK_ST), layer),
        _const_spec((SSM_CHUNKS, CHUNK_CH, CHUNK_ST), layer),
        _const_spec((1, SSM_WIDTH), layer),
        _const_spec((SSM_WIDTH, D_MODEL), layer),
        _const_spec((SSM_WIDTH, D_MODEL), layer),
        _const_spec((D_MODEL, D_MODEL), layer),
    ]
    out_specs = [
        row_spec,
        pl.BlockSpec((carry_rows, CONV_A_WIDTH), lambda j: (0, 0)),
        pl.BlockSpec((rb, SSM_FLAT), lambda j: (0, 0)),
        pl.BlockSpec((rb, SSM_FLAT), lambda j: (0, 0)),
    ]
    out_shape = [
        jax.ShapeDtypeStruct((n, D_MODEL), _F32),
        jax.ShapeDtypeStruct((carry_rows, CONV_A_WIDTH), _F32),
        jax.ShapeDtypeStruct((rb, SSM_FLAT), _F32),
        jax.ShapeDtypeStruct((rb, SSM_FLAT), _F32),
    ]
    return pl.pallas_call(
        functools.partial(_mixer_kernel, rb=rb, tb=tb),
        grid=(n // rows,),
        in_specs=in_specs,
        out_specs=out_specs,
        out_shape=out_shape,
        scratch_shapes=[
            pltpu.VMEM((carry_rows + rows, CONV_A_WIDTH), _F32),
            pltpu.VMEM((rows, SSM_FLAT), _F32),
            pltpu.VMEM((rows, SSM_FLAT), _F32),
        ],
        compiler_params=pltpu.CompilerParams(
            dimension_semantics=("arbitrary",), vmem_limit_bytes=VMEM_LIMIT_BYTES),
        name=f"mixer_l{layer}_rb{rb}",
    )(x, conv0, sre0, sim0,
      wts["norm_mix"], wts["w_in"], wts["conv_a_w"], wts["conv_a_b"], wts["w_out_a"],
      wts["abar_re"], wts["abar_im"], wts["bblk"], wts["cblk_re"], wts["cblk_im"], wts["d_skip"],
      wts["w_glu_a"], wts["w_glu_b"], wts["w_o"])


def _ffn_call(x, p, f0, wts, layer, rb, tb, final_norm):
    n = x.shape[0]
    rows = rb * tb
    carry_rows = (CONV_WIDTH - 1) * rb
    row_spec = pl.BlockSpec((rows, D_MODEL), lambda j: (j, 0))
    in_specs = [
        row_spec,
        pl.BlockSpec((None, rows, PLE_DIM), lambda j: (layer, j, 0)),
        _const_spec((carry_rows, D_FF)),
        _const_spec((1, D_MODEL), layer),
        _const_spec((D_MODEL, 2 * D_FF), layer),
        _const_spec((CONV_WIDTH, D_FF), layer),
        _const_spec((1, D_FF), layer),
        _const_spec((D_FF, D_MODEL), layer),
        _const_spec((1, D_MODEL), layer),
        _const_spec((D_MODEL, D_MODEL), layer),
        _const_spec((PLE_DIM, D_MODEL), layer),
        _const_spec((1, D_MODEL)),
    ]
    out_specs = [row_spec, pl.BlockSpec((carry_rows, D_FF), lambda j: (0, 0))]
    out_shape = [jax.ShapeDtypeStruct((n, D_MODEL), _F32),
                 jax.ShapeDtypeStruct((carry_rows, D_FF), _F32)]
    return pl.pallas_call(
        functools.partial(_ffn_kernel, rb=rb, tb=tb, final_norm=final_norm),
        grid=(n // rows,),
        in_specs=in_specs,
        out_specs=out_specs,
        out_shape=out_shape,
        scratch_shapes=[pltpu.VMEM((carry_rows + rows, FF_CHUNK_MAX), _F32)],
        compiler_params=pltpu.CompilerParams(
            dimension_semantics=("arbitrary",), vmem_limit_bytes=VMEM_LIMIT_BYTES),
        name=f"ffn_l{layer}_rb{rb}",
    )(x, p, f0,
      wts["norm_ffn"], wts["w_up"], wts["ffn_conv_w"], wts["ffn_conv_b"], wts["w_down"],
      wts["norm_ple"], wts["w_ple_gate"], wts["w_ple"], wts["norm_final"])


def _run_trunk(x, p, conv0, sre0, sim0, f0, wts, rb, tb):
    convs, sres, sims, ffns = [], [], [], []
    for layer in range(DEPTH):
        x, conv_n, sre_n, sim_n = _mixer_call(x, conv0[layer], sre0[layer], sim0[layer],
                                              wts, layer, rb, tb)
        x, ffn_n = _ffn_call(x, p, f0[layer], wts, layer, rb, tb, final_norm=(layer == DEPTH - 1))
        convs.append(conv_n)
        sres.append(sre_n)
        sims.append(sim_n)
        ffns.append(ffn_n)
    return x, jnp.stack(convs), jnp.stack(sres), jnp.stack(sims), jnp.stack(ffns)


def _to_time_major_state(s):
    d, b, k, c = s.shape
    return s.transpose(0, 2, 1, 3).reshape(d, k * b, c)


def _from_time_major_state(s, b):
    d, kb, c = s.shape
    return s.reshape(d, kb // b, b, c).transpose(0, 2, 1, 3)


def kernel(x_prompt, x_sample, p_prompt, p_sample, state_conv_a, state_ssm_re, state_ssm_im, state_ffn_conv, norm_mix, w_in, conv_a_w, conv_a_b, w_out_a, log_dt, lam_re, lam_im, b_re, b_im, c_re, c_im, d_skip, w_glu_a, w_glu_b, w_o, norm_ffn, w_up, ffn_conv_w, ffn_conv_b, w_down, norm_ple, w_ple_gate, w_ple, norm_final):
    abar_re, abar_im, bblk, cblk_re, cblk_im = _ssm_prep(log_dt, lam_re, lam_im, b_re, b_im, c_re, c_im)
    wts = {
        "norm_mix": norm_mix.reshape(DEPTH, 1, D_MODEL),
        "w_in": w_in.astype(_BF16),
        "conv_a_w": conv_a_w,
        "conv_a_b": conv_a_b.reshape(DEPTH, 1, CONV_A_WIDTH),
        "w_out_a": w_out_a.astype(_BF16),
        "abar_re": abar_re, "abar_im": abar_im,
        "bblk": bblk, "cblk_re": cblk_re, "cblk_im": cblk_im,
        "d_skip": d_skip.reshape(DEPTH, 1, SSM_WIDTH),
        "w_glu_a": w_glu_a.astype(_BF16),
        "w_glu_b": w_glu_b.astype(_BF16),
        "w_o": w_o.astype(_BF16),
        "norm_ffn": norm_ffn.reshape(DEPTH, 1, D_MODEL),
        "w_up": w_up.astype(_BF16),
        "ffn_conv_w": ffn_conv_w,
        "ffn_conv_b": ffn_conv_b.reshape(DEPTH, 1, D_FF),
        "w_down": w_down.astype(_BF16),
        "norm_ple": norm_ple.reshape(DEPTH, 1, D_MODEL),
        "w_ple_gate": w_ple_gate.astype(_BF16),
        "w_ple": w_ple.astype(_BF16),
        "norm_final": norm_final.reshape(1, D_MODEL),
    }

    bp, seq, _ = x_prompt.shape
    xp = x_prompt.transpose(1, 0, 2).reshape(seq * bp, D_MODEL)
    pp = p_prompt.transpose(0, 2, 1, 3).reshape(DEPTH, seq * bp, PLE_DIM)
    zc = jnp.zeros((DEPTH, (CONV_WIDTH - 1) * bp, CONV_A_WIDTH), _F32)
    zs = jnp.zeros((DEPTH, bp, SSM_FLAT), _F32)
    zf = jnp.zeros((DEPTH, (CONV_WIDTH - 1) * bp, D_FF), _F32)
    yp, conv_p, sre_p, sim_p, ffn_p = _run_trunk(xp, pp, zc, zs, zs, zf, wts, bp, PROMPT_TIME_BLOCK)
    y_prompt = yp.reshape(seq, bp, D_MODEL).transpose(1, 0, 2)

    bs, sseq, _ = x_sample.shape
    xs = x_sample.transpose(1, 0, 2).reshape(sseq * bs, D_MODEL)
    ps = p_sample.transpose(0, 2, 1, 3).reshape(DEPTH, sseq * bs, PLE_DIM)
    ys, conv_s, sre_s, sim_s, ffn_s = _run_trunk(
        xs, ps, _to_time_major_state(state_conv_a),
        state_ssm_re.reshape(DEPTH, bs, SSM_FLAT), state_ssm_im.reshape(DEPTH, bs, SSM_FLAT),
        _to_time_major_state(state_ffn_conv), wts, bs, sseq)
    y_sample = ys.reshape(sseq, bs, D_MODEL).transpose(1, 0, 2)

    grp = (SSM_GROUPS, SSM_STATE)
    return (y_prompt, y_sample,
            _from_time_major_state(conv_p, bp), sre_p.reshape(DEPTH, bp, *grp),
            sim_p.reshape(DEPTH, bp, *grp), _from_time_major_state(ffn_p, bp),
            _from_time_major_state(conv_s, bs), sre_s.reshape(DEPTH, bs, *grp),
            sim_s.reshape(DEPTH, bs, *grp), _from_time_major_state(ffn_s, bs))
```

```python
import functools
import math

import jax
import jax.numpy as jnp
from jax import lax
from jax.experimental import pallas as pl
from jax.experimental.pallas import tpu as pltpu

D_MODEL = 1024
DEPTH = 2
CONV_WIDTH = 3
CONV_A_WIDTH = 512
SSM_WIDTH = 512
SSM_GROUP = 16
SSM_GROUPS = 32
SSM_STATE = 64
SSM_FLAT = SSM_GROUPS * SSM_STATE
D_FF = 2816
PLE_DIM = 256
NORM_EPS = 1e-6
IN_PROJ_WIDTH = 4096

SSM_CHUNKS = 4
CHUNK_CH = SSM_WIDTH // SSM_CHUNKS
CHUNK_ST = SSM_FLAT // SSM_CHUNKS

FF_CHUNKS = ((0, 1024), (1024, 2048), (2048, 2816))
FF_CHUNK_MAX = 1024

PROMPT_TIME_BLOCK = 64
VMEM_LIMIT_BYTES = 56 * 1024 * 1024

_BF16 = jnp.bfloat16
_F32 = jnp.float32


def _dot(a, b):
    return jnp.dot(a, b, preferred_element_type=_F32)


def _dot_nt(a, b):
    return lax.dot_general(a, b, (((1,), (1,)), ((), ())), preferred_element_type=_F32)


def _rms(x, g_ref):
    ms = jnp.mean(x * x, axis=-1, keepdims=True)
    return x * lax.rsqrt(ms + NORM_EPS) * g_ref[...]


def _gelu(x):
    c = math.sqrt(2.0 / math.pi)
    return x * (0.5 * (1.0 + jnp.tanh(c * (x + 0.044715 * (x * x * x)))))


def _sigmoid(x):
    return 0.5 * jnp.tanh(0.5 * x) + 0.5


def _seq_copies(hbm_ref, lead, buf, sem, step, slot, tb, to_hbm):
    copies = []
    for b in range(buf.shape[2]):
        hbm_view = hbm_ref.at[lead + (b, pl.ds(step * tb, tb))]
        vmem_view = buf.at[slot, :, b]
        src, dst = (vmem_view, hbm_view) if to_hbm else (hbm_view, vmem_view)
        copies.append(pltpu.make_async_copy(src, dst, sem.at[slot]))
    return copies


def _load_time_major(hbm_ref, lead, buf, sem, tb):
    j = pl.program_id(0)
    slot = j % 2

    @pl.when(j == 0)
    def _():
        for c in _seq_copies(hbm_ref, lead, buf, sem, 0, 0, tb, False):
            c.start()

    @pl.when(j + 1 < pl.num_programs(0))
    def _():
        for c in _seq_copies(hbm_ref, lead, buf, sem, j + 1, 1 - slot, tb, False):
            c.start()

    for c in _seq_copies(hbm_ref, lead, buf, sem, j, slot, tb, False):
        c.wait()
    return buf[slot].reshape(tb * buf.shape[2], buf.shape[3])


def _store_batch_major(val, hbm_ref, buf, sem, tb):
    j = pl.program_id(0)
    slot = j % 2
    buf[slot] = val.reshape(tb, buf.shape[2], buf.shape[3])
    for c in _seq_copies(hbm_ref, (), buf, sem, j, slot, tb, True):
        c.start()

    @pl.when(j >= 1)
    def _():
        for c in _seq_copies(hbm_ref, (), buf, sem, j - 1, 1 - slot, tb, True):
            c.wait()

    @pl.when(j == pl.num_programs(0) - 1)
    def _():
        for c in _seq_copies(hbm_ref, (), buf, sem, j, slot, tb, True):
            c.wait()


def _ssm_prep_kernel(logdt_ref, lre_ref, lim_ref, bre_ref, bim_ref, cre_ref, cim_ref,
                     abar_re_ref, abar_im_ref, bblk_ref, cblk_re_ref, cblk_im_ref):
    dt = jnp.exp(logdt_ref[...])
    lre = lre_ref[...]
    lim = lim_ref[...]
    mag = jnp.exp(lre * dt)
    ang = lim * dt
    ar = mag * jnp.cos(ang)
    ai = mag * jnp.sin(ang)
    abar_re_ref[...] = ar
    abar_im_ref[...] = ai
    den = lre * lre + lim * lim
    nr = ar - 1.0
    coef_re = (nr * lre + ai * lim) / den
    coef_im = (ai * lre - nr * lim) / den
    bre = bre_ref[...]
    bim = bim_ref[...]
    bbar_re = coef_re * bre - coef_im * bim
    bbar_im = coef_re * bim + coef_im * bre
    row = lax.broadcasted_iota(jnp.int32, (CHUNK_CH, CHUNK_ST), 0)
    lane = lax.broadcasted_iota(jnp.int32, (CHUNK_CH, CHUNK_ST), 1)
    same_group = (row // SSM_GROUP) == (lane // SSM_STATE)
    reps = CHUNK_CH // SSM_GROUP
    for c in range(SSM_CHUNKS):
        cols = slice(c * CHUNK_ST, (c + 1) * CHUNK_ST)
        tre = jnp.concatenate([bbar_re[:, cols]] * reps, axis=0)
        tim = jnp.concatenate([bbar_im[:, cols]] * reps, axis=0)
        bblk_ref[c, :, 0:CHUNK_ST] = jnp.where(same_group, tre, 0.0).astype(_BF16)
        bblk_ref[c, :, CHUNK_ST:2 * CHUNK_ST] = jnp.where(same_group, tim, 0.0).astype(_BF16)
        rows = slice(c * CHUNK_CH, (c + 1) * CHUNK_CH)
        cblk_re_ref[c] = jnp.where(same_group, cre_ref[rows, :], 0.0).astype(_BF16)
        cblk_im_ref[c] = jnp.where(same_group, -cim_ref[rows, :], 0.0).astype(_BF16)


def _ssm_prep(log_dt, lam_re, lam_im, b_re, b_im, c_re, c_im):
    logdt = jnp.repeat(log_dt, SSM_STATE, axis=1).reshape(DEPTH, 1, SSM_FLAT)
    lre = lam_re.reshape(DEPTH, 1, SSM_FLAT)
    lim = lam_im.reshape(DEPTH, 1, SSM_FLAT)
    bre = b_re.transpose(0, 3, 1, 2).reshape(DEPTH, SSM_GROUP, SSM_FLAT)
    bim = b_im.transpose(0, 3, 1, 2).reshape(DEPTH, SSM_GROUP, SSM_FLAT)
    reps = CHUNK_ST // SSM_STATE
    cre = jnp.tile(c_re.reshape(DEPTH, SSM_WIDTH, SSM_STATE), (1, 1, reps))
    cim = jnp.tile(c_im.reshape(DEPTH, SSM_WIDTH, SSM_STATE), (1, 1, reps))

    def spec(*shape):
        return pl.BlockSpec((None,) + shape, lambda i: (i,) + (0,) * len(shape))

    return pl.pallas_call(
        _ssm_prep_kernel,
        grid=(DEPTH,),
        in_specs=[spec(1, SSM_FLAT), spec(1, SSM_FLAT), spec(1, SSM_FLAT),
                  spec(SSM_GROUP, SSM_FLAT), spec(SSM_GROUP, SSM_FLAT),
                  spec(SSM_WIDTH, CHUNK_ST), spec(SSM_WIDTH, CHUNK_ST)],
        out_specs=[spec(1, SSM_FLAT), spec(1, SSM_FLAT),
                   spec(SSM_CHUNKS, CHUNK_CH, 2 * CHUNK_ST),
                   spec(SSM_CHUNKS, CHUNK_CH, CHUNK_ST),
                   spec(SSM_CHUNKS, CHUNK_CH, CHUNK_ST)],
        out_shape=[jax.ShapeDtypeStruct((DEPTH, 1, SSM_FLAT), _F32),
                   jax.ShapeDtypeStruct((DEPTH, 1, SSM_FLAT), _F32),
                   jax.ShapeDtypeStruct((DEPTH, SSM_CHUNKS, CHUNK_CH, 2 * CHUNK_ST), _BF16),
                   jax.ShapeDtypeStruct((DEPTH, SSM_CHUNKS, CHUNK_CH, CHUNK_ST), _BF16),
                   jax.ShapeDtypeStruct((DEPTH, SSM_CHUNKS, CHUNK_CH, CHUNK_ST), _BF16)],
        name="ssm_prep",
    )(logdt, lre, lim, bre, bim, cre, cim)


def _mixer_kernel(x_ref, conv0_ref, sre0_ref, sim0_ref,
                  g_ref, w_in_ref, cw_ref, cb_ref, w_out_a_ref,
                  abar_re_ref, abar_im_ref, bblk_ref, cblk_re_ref, cblk_im_ref, dskip_ref,
                  glu_a_ref, glu_b_ref, w_o_ref,
                  out_ref, conv_out_ref, sre_out_ref, sim_out_ref,
                  vbuf, s_re, s_im, *x_dma, rb, tb):
    rows = rb * tb
    carry_rows = (CONV_WIDTH - 1) * rb

    @pl.when(pl.program_id(0) == 0)
    def _():
        conv_out_ref[...] = conv0_ref[...]
        sre_out_ref[...] = sre0_ref[...]
        sim_out_ref[...] = sim0_ref[...]

    x = _load_time_major(x_ref, (), *x_dma, tb) if x_dma else x_ref[...]
    xn = _rms(x, g_ref).astype(_BF16)

    def proj(lo, hi):
        return _dot(xn, w_in_ref[:, lo:hi])

    h = proj(0, 512)
    gate_c = proj(1024, 1536)
    vbuf[0:carry_rows, :] = conv_out_ref[...]
    vbuf[carry_rows:carry_rows + rows, :] = gate_c * h
    conv_y = (cb_ref[...]
              + vbuf[0:rows, :] * cw_ref[0:1, :]
              + vbuf[rb:rb + rows, :] * cw_ref[1:2, :]
              + vbuf[2 * rb:2 * rb + rows, :] * cw_ref[2:3, :])
    conv_out_ref[...] = vbuf[rows:rows + carry_rows, :]
    gate_b = proj(512, 1024)
    z_a = _dot((gate_b * conv_y).astype(_BF16), w_out_a_ref[...])
    merged = _sigmoid(proj(2048, 3072)) * z_a

    u = proj(1536, 2048)
    ub = u.astype(_BF16)
    for c in range(SSM_CHUNKS):
        bu = _dot(ub[:, c * CHUNK_CH:(c + 1) * CHUNK_CH], bblk_ref[c])
        s_re[:, c * CHUNK_ST:(c + 1) * CHUNK_ST] = bu[:, 0:CHUNK_ST]
        s_im[:, c * CHUNK_ST:(c + 1) * CHUNK_ST] = bu[:, CHUNK_ST:2 * CHUNK_ST]

    for c in range(SSM_CHUNKS):
        cols = slice(c * CHUNK_ST, (c + 1) * CHUNK_ST)
        ar = jnp.broadcast_to(abar_re_ref[:, cols], (rb, CHUNK_ST))
        ai = jnp.broadcast_to(abar_im_ref[:, cols], (rb, CHUNK_ST))

        def step(t, carry, cols=cols, ar=ar, ai=ai):
            sr, si = carry
            r0 = pl.multiple_of(t * rb, rb)
            nsr = ar * sr - ai * si + s_re[pl.ds(r0, rb), cols]
            nsi = ar * si + ai * sr + s_im[pl.ds(r0, rb), cols]
            s_re[pl.ds(r0, rb), cols] = nsr
            s_im[pl.ds(r0, rb), cols] = nsi
            return nsr, nsi

        sr, si = lax.fori_loop(0, tb, step, (sre_out_ref[:, cols], sim_out_ref[:, cols]),
                               unroll=min(tb, 8))
        sre_out_ref[:, cols] = sr
        sim_out_ref[:, cols] = si

    ys = []
    for c in range(SSM_CHUNKS):
        cols = slice(c * CHUNK_ST, (c + 1) * CHUNK_ST)
        ys.append(_dot_nt(s_re[:, cols].astype(_BF16), cblk_re_ref[c])
                  + _dot_nt(s_im[:, cols].astype(_BF16), cblk_im_ref[c]))
    ssm_y = jnp.concatenate(ys, axis=1) + dskip_ref[...] * u
    sy = _gelu(ssm_y).astype(_BF16)
    z_b = _dot(sy, glu_a_ref[...]) * _sigmoid(_dot(sy, glu_b_ref[...]))
    merged = (merged + _sigmoid(proj(3072, 4096)) * z_b).astype(_BF16)
    out_ref[...] = x + _dot(merged, w_o_ref[...])


def _ffn_kernel(x_ref, p_ref, f0_ref,
                g2_ref, w_up_ref, fcw_ref, fcb_ref, w_down_ref,
                g3_ref, w_pg_ref, w_ple_ref, gf_ref,
                out_ref, fconv_out_ref,
                fbuf, *dma, rb, tb, layer, final_norm, p_native, out_native):
    rows = rb * tb
    carry_rows = (CONV_WIDTH - 1) * rb
    dma = list(dma)
    p_dma = [dma.pop(0), dma.pop(0)] if p_native else None
    out_dma = [dma.pop(0), dma.pop(0)] if out_native else None

    @pl.when(pl.program_id(0) == 0)
    def _():
        fconv_out_ref[...] = f0_ref[...]

    x = x_ref[...]
    xn = _rms(x, g2_ref).astype(_BF16)
    acc = x
    for lo, hi in FF_CHUNKS:
        wc = hi - lo
        up_a = _dot(xn, w_up_ref[:, lo:hi])
        up_b = _dot(xn, w_up_ref[:, D_FF + lo:D_FF + hi])
        fbuf[0:carry_rows, 0:wc] = fconv_out_ref[:, lo:hi]
        fbuf[carry_rows:carry_rows + rows, 0:wc] = up_a
        conv_a = (fcb_ref[:, lo:hi]
                  + fbuf[0:rows, 0:wc] * fcw_ref[0:1, lo:hi]
                  + fbuf[rb:rb + rows, 0:wc] * fcw_ref[1:2, lo:hi]
                  + fbuf[2 * rb:2 * rb + rows, 0:wc] * fcw_ref[2:3, lo:hi])
        fconv_out_ref[:, lo:hi] = fbuf[rows:rows + carry_rows, 0:wc]
        hid = (_gelu(conv_a) * up_b).astype(_BF16)
        acc = acc + _dot(hid, w_down_ref[lo:hi, :])
    x2 = acc
    xn3 = _rms(x2, g3_ref).astype(_BF16)
    gate = _sigmoid(_dot(xn3, w_pg_ref[...]))
    p = _load_time_major(p_ref, (layer,), *p_dma, tb) if p_native else p_ref[...]
    pe = _dot(p.astype(_BF16), w_ple_ref[...])
    x3 = x2 + gate * pe
    if final_norm:
        x3 = _rms(x3, gf_ref)
    if out_native:
        _store_batch_major(x3, out_ref, *out_dma, tb)
    else:
        out_ref[...] = x3


def _const_spec(shape, layer=None):
    if layer is None:
        return pl.BlockSpec(shape, lambda j: (0,) * len(shape), pipeline_mode=pl.Buffered(1))
    return pl.BlockSpec((None,) + shape, lambda j: (layer,) + (0,) * len(shape),
                        pipeline_mode=pl.Buffered(1))


def _native_scratch(tb, rb, width):
    return [pltpu.VMEM((2, tb, rb, width), _F32), pltpu.SemaphoreType.DMA((2,))]


def _mixer_call(x, conv0, sre0, sim0, wts, layer, rb, tb, x_native):
    n = x.shape[0] * x.shape[1] if x_native else x.shape[0]
    rows = rb * tb
    carry_rows = (CONV_WIDTH - 1) * rb
    row_spec = pl.BlockSpec((rows, D_MODEL), lambda j: (j, 0))
    in_specs = [
        pl.BlockSpec(memory_space=pl.ANY) if x_native else row_spec,
        _const_spec((carry_rows, CONV_A_WIDTH), layer),
        _const_spec((rb, SSM_FLAT), layer),
        _const_spec((rb, SSM_FLAT), layer),
        _const_spec((1, D_MODEL), layer),
        _const_spec((D_MODEL, IN_PROJ_WIDTH), layer),
        _const_spec((CONV_WIDTH, CONV_A_WIDTH), layer),
        _const_spec((1, CONV_A_WIDTH), layer),
        _const_spec((CONV_A_WIDTH, D_MODEL), layer),
        _const_spec((1, SSM_FLAT), layer),
        _const_spec((1, SSM_FLAT), layer),
        _const_spec((SSM_CHUNKS, CHUNK_CH, 2 * CHUNK_ST), layer),
        _const_spec((SSM_CHUNKS, CHUNK_CH, CHUNK_ST), layer),
        _const_spec((SSM_CHUNKS, CHUNK_CH, CHUNK_ST), layer),
        _const_spec((1, SSM_WIDTH), layer),
        _const_spec((SSM_WIDTH, D_MODEL), layer),
        _const_spec((SSM_WIDTH, D_MODEL), layer),
        _const_spec((D_MODEL, D_MODEL), layer),
    ]
    out_specs = [
        row_spec,
        pl.BlockSpec((carry_rows, CONV_A_WIDTH), lambda j: (0, 0)),
        pl.BlockSpec((rb, SSM_FLAT), lambda j: (0, 0)),
        pl.BlockSpec((rb, SSM_FLAT), lambda j: (0, 0)),
    ]
    out_shape = [
        jax.ShapeDtypeStruct((n, D_MODEL), _F32),
        jax.ShapeDtypeStruct((carry_rows, CONV_A_WIDTH), _F32),
        jax.ShapeDtypeStruct((rb, SSM_FLAT), _F32),
        jax.ShapeDtypeStruct((rb, SSM_FLAT), _F32),
    ]
    scratch = [
        pltpu.VMEM((carry_rows + rows, CONV_A_WIDTH), _F32),
        pltpu.VMEM((rows, SSM_FLAT), _F32),
        pltpu.VMEM((rows, SSM_FLAT), _F32),
    ]
    if x_native:
        scratch += _native_scratch(tb, rb, D_MODEL)
    return pl.pallas_call(
        functools.partial(_mixer_kernel, rb=rb, tb=tb),
        grid=(n // rows,),
        in_specs=in_specs,
        out_specs=out_specs,
        out_shape=out_shape,
        scratch_shapes=scratch,
        compiler_params=pltpu.CompilerParams(
            dimension_semantics=("arbitrary",), vmem_limit_bytes=VMEM_LIMIT_BYTES),
        name=f"mixer_l{layer}_rb{rb}",
    )(x, conv0, sre0, sim0,
      wts["norm_mix"], wts["w_in"], wts["conv_a_w"], wts["conv_a_b"], wts["w_out_a"],
      wts["abar_re"], wts["abar_im"], wts["bblk"], wts["cblk_re"], wts["cblk_im"], wts["d_skip"],
      wts["w_glu_a"], wts["w_glu_b"], wts["w_o"])


def _ffn_call(x, p, f0, wts, layer, rb, tb, final_norm, p_native, out_native):
    n = x.shape[0]
    rows = rb * tb
    carry_rows = (CONV_WIDTH - 1) * rb
    row_spec = pl.BlockSpec((rows, D_MODEL), lambda j: (j, 0))
    in_specs = [
        row_spec,
        (pl.BlockSpec(memory_space=pl.ANY) if p_native
         else pl.BlockSpec((None, rows, PLE_DIM), lambda j: (layer, j, 0))),
        _const_spec((carry_rows, D_FF), layer),
        _const_spec((1, D_MODEL), layer),
        _const_spec((D_MODEL, 2 * D_FF), layer),
        _const_spec((CONV_WIDTH, D_FF), layer),
        _const_spec((1, D_FF), layer),
        _const_spec((D_FF, D_MODEL), layer),
        _const_spec((1, D_MODEL), layer),
        _const_spec((D_MODEL, D_MODEL), layer),
        _const_spec((PLE_DIM, D_MODEL), layer),
        _const_spec((1, D_MODEL)),
    ]
    out_specs = [pl.BlockSpec(memory_space=pl.ANY) if out_native else row_spec,
                 pl.BlockSpec((carry_rows, D_FF), lambda j: (0, 0))]
    out_shape = [jax.ShapeDtypeStruct((rb, n // rb, D_MODEL) if out_native else (n, D_MODEL), _F32),
                 jax.ShapeDtypeStruct((carry_rows, D_FF), _F32)]
    scratch = [pltpu.VMEM((carry_rows + rows, FF_CHUNK_MAX), _F32)]
    if p_native:
        scratch += _native_scratch(tb, rb, PLE_DIM)
    if out_native:
        scratch += _native_scratch(tb, rb, D_MODEL)
    return pl.pallas_call(
        functools.partial(_ffn_kernel, rb=rb, tb=tb, layer=layer, final_norm=final_norm,
                          p_native=p_native, out_native=out_native),
        grid=(n // rows,),
        in_specs=in_specs,
        out_specs=out_specs,
        out_shape=out_shape,
        scratch_shapes=scratch,
        compiler_params=pltpu.CompilerParams(
            dimension_semantics=("arbitrary",), vmem_limit_bytes=VMEM_LIMIT_BYTES),
        name=f"ffn_l{layer}_rb{rb}",
    )(x, p, f0,
      wts["norm_ffn"], wts["w_up"], wts["ffn_conv_w"], wts["ffn_conv_b"], wts["w_down"],
      wts["norm_ple"], wts["w_ple_gate"], wts["w_ple"], wts["norm_final"])


def _run_trunk(x, p, conv0, sre0, sim0, f0, wts, rb, tb, native):
    convs, sres, sims, ffns = [], [], [], []
    for layer in range(DEPTH):
        last = layer == DEPTH - 1
        x, conv_n, sre_n, sim_n = _mixer_call(x, conv0, sre0, sim0, wts, layer, rb, tb,
                                              x_native=native and layer == 0)
        x, ffn_n = _ffn_call(x, p, f0, wts, layer, rb, tb, final_norm=last,
                             p_native=native, out_native=native and last)
        convs.append(conv_n)
        sres.append(sre_n)
        sims.append(sim_n)
        ffns.append(ffn_n)
    return x, jnp.stack(convs), jnp.stack(sres), jnp.stack(sims), jnp.stack(ffns)


def _to_time_major_state(s):
    d, b, k, c = s.shape
    return s.transpose(0, 2, 1, 3).reshape(d, k * b, c)


def _from_time_major_state(s, b):
    d, kb, c = s.shape
    return s.reshape(d, kb // b, b, c).transpose(0, 2, 1, 3)


def kernel(x_prompt, x_sample, p_prompt, p_sample, state_conv_a, state_ssm_re, state_ssm_im, state_ffn_conv, norm_mix, w_in, conv_a_w, conv_a_b, w_out_a, log_dt, lam_re, lam_im, b_re, b_im, c_re, c_im, d_skip, w_glu_a, w_glu_b, w_o, norm_ffn, w_up, ffn_conv_w, ffn_conv_b, w_down, norm_ple, w_ple_gate, w_ple, norm_final):
    abar_re, abar_im, bblk, cblk_re, cblk_im = _ssm_prep(log_dt, lam_re, lam_im, b_re, b_im, c_re, c_im)
    wts = {
        "norm_mix": norm_mix.reshape(DEPTH, 1, D_MODEL),
        "w_in": w_in.astype(_BF16),
        "conv_a_w": conv_a_w,
        "conv_a_b": conv_a_b.reshape(DEPTH, 1, CONV_A_WIDTH),
        "w_out_a": w_out_a.astype(_BF16),
        "abar_re": abar_re, "abar_im": abar_im,
        "bblk": bblk, "cblk_re": cblk_re, "cblk_im": cblk_im,
        "d_skip": d_skip.reshape(DEPTH, 1, SSM_WIDTH),
        "w_glu_a": w_glu_a.astype(_BF16),
        "w_glu_b": w_glu_b.astype(_BF16),
        "w_o": w_o.astype(_BF16),
        "norm_ffn": norm_ffn.reshape(DEPTH, 1, D_MODEL),
        "w_up": w_up.astype(_BF16),
        "ffn_conv_w": ffn_conv_w,
        "ffn_conv_b": ffn_conv_b.reshape(DEPTH, 1, D_FF),
        "w_down": w_down.astype(_BF16),
        "norm_ple": norm_ple.reshape(DEPTH, 1, D_MODEL),
        "w_ple_gate": w_ple_gate.astype(_BF16),
        "w_ple": w_ple.astype(_BF16),
        "norm_final": norm_final.reshape(1, D_MODEL),
    }

    bp = x_prompt.shape[0]
    zc = jnp.zeros((DEPTH, (CONV_WIDTH - 1) * bp, CONV_A_WIDTH), _F32)
    zs = jnp.zeros((DEPTH, bp, SSM_FLAT), _F32)
    zf = jnp.zeros((DEPTH, (CONV_WIDTH - 1) * bp, D_FF), _F32)
    y_prompt, conv_p, sre_p, sim_p, ffn_p = _run_trunk(
        x_prompt, p_prompt, zc, zs, zs, zf, wts, bp, PROMPT_TIME_BLOCK, native=True)

    bs, sseq, _ = x_sample.shape
    xs = x_sample.transpose(1, 0, 2).reshape(sseq * bs, D_MODEL)
    ps = p_sample.transpose(0, 2, 1, 3).reshape(DEPTH, sseq * bs, PLE_DIM)
    ys, conv_s, sre_s, sim_s, ffn_s = _run_trunk(
        xs, ps, _to_time_major_state(state_conv_a),
        state_ssm_re.reshape(DEPTH, bs, SSM_FLAT), state_ssm_im.reshape(DEPTH, bs, SSM_FLAT),
        _to_time_major_state(state_ffn_conv), wts, bs, sseq, native=False)
    y_sample = ys.reshape(sseq, bs, D_MODEL).transpose(1, 0, 2)

    grp = (SSM_GROUPS, SSM_STATE)
    return (y_prompt, y_sample,
            _from_time_major_state(conv_p, bp), sre_p.reshape(DEPTH, bp, *grp),
            sim_p.reshape(DEPTH, bp, *grp), _from_time_major_state(ffn_p, bp),
            _from_time_major_state(conv_s, bs), sre_s.reshape(DEPTH, bs, *grp),
            sim_s.reshape(DEPTH, bs, *grp), _from_time_major_state(ffn_s, bs))
```

```python
import functools
import math

import jax
import jax.numpy as jnp
from jax import lax
from jax.experimental import pallas as pl
from jax.experimental.pallas import tpu as pltpu

D_MODEL = 1024
DEPTH = 2
CONV_WIDTH = 3
CONV_A_WIDTH = 512
SSM_WIDTH = 512
SSM_GROUP = 16
SSM_GROUPS = 32
SSM_STATE = 64
SSM_FLAT = SSM_GROUPS * SSM_STATE
D_FF = 2816
PLE_DIM = 256
NORM_EPS = 1e-6
IN_PROJ_WIDTH = 4096

SSM_CHUNKS = 4
CHUNK_CH = SSM_WIDTH // SSM_CHUNKS
CHUNK_ST = SSM_FLAT // SSM_CHUNKS

FF_CHUNKS = ((0, 1024), (1024, 2048), (2048, 2816))
FF_CHUNK_MAX = 1024

PROMPT_TIME_BLOCK = 64
VMEM_LIMIT_BYTES = 56 * 1024 * 1024

_BF16 = jnp.bfloat16
_F32 = jnp.float32


def _dot(a, b):
    return jnp.dot(a, b, preferred_element_type=_F32)


def _dot_nt(a, b):
    return lax.dot_general(a, b, (((1,), (1,)), ((), ())), preferred_element_type=_F32)


def _rms(x, g_ref):
    ms = jnp.mean(x * x, axis=-1, keepdims=True)
    return x * lax.rsqrt(ms + NORM_EPS) * g_ref[...]


def _gelu(x):
    c = math.sqrt(2.0 / math.pi)
    return x * (0.5 * (1.0 + jnp.tanh(c * (x + 0.044715 * (x * x * x)))))


def _sigmoid(x):
    return 0.5 * jnp.tanh(0.5 * x) + 0.5


def _seq_copies(hbm_ref, lead, buf, sem, step, slot, tb, to_hbm):
    copies = []
    for b in range(buf.shape[2]):
        hbm_view = hbm_ref.at[lead + (b, pl.ds(step * tb, tb))]
        vmem_view = buf.at[slot, :, b]
        src, dst = (vmem_view, hbm_view) if to_hbm else (hbm_view, vmem_view)
        copies.append(pltpu.make_async_copy(src, dst, sem.at[slot]))
    return copies


def _load_time_major(hbm_ref, lead, buf, sem, tb):
    j = pl.program_id(0)
    slot = j % 2

    @pl.when(j == 0)
    def _():
        for c in _seq_copies(hbm_ref, lead, buf, sem, 0, 0, tb, False):
            c.start()

    @pl.when(j + 1 < pl.num_programs(0))
    def _():
        for c in _seq_copies(hbm_ref, lead, buf, sem, j + 1, 1 - slot, tb, False):
            c.start()

    for c in _seq_copies(hbm_ref, lead, buf, sem, j, slot, tb, False):
        c.wait()
    return buf[slot].reshape(tb * buf.shape[2], buf.shape[3])


def _store_batch_major(val, hbm_ref, buf, sem, tb):
    j = pl.program_id(0)
    slot = j % 2
    buf[slot] = val.reshape(tb, buf.shape[2], buf.shape[3])
    for c in _seq_copies(hbm_ref, (), buf, sem, j, slot, tb, True):
        c.start()

    @pl.when(j >= 1)
    def _():
        for c in _seq_copies(hbm_ref, (), buf, sem, j - 1, 1 - slot, tb, True):
            c.wait()

    @pl.when(j == pl.num_programs(0) - 1)
    def _():
        for c in _seq_copies(hbm_ref, (), buf, sem, j, slot, tb, True):
            c.wait()


def _ssm_prep_kernel(logdt_ref, lre_ref, lim_ref, bre_ref, bim_ref, cre_ref, cim_ref,
                     abar_re_ref, abar_im_ref, bblk_ref, cblk_re_ref, cblk_im_ref):
    dt = jnp.exp(logdt_ref[...])
    lre = lre_ref[...]
    lim = lim_ref[...]
    mag = jnp.exp(lre * dt)
    ang = lim * dt
    ar = mag * jnp.cos(ang)
    ai = mag * jnp.sin(ang)
    abar_re_ref[...] = ar
    abar_im_ref[...] = ai
    den = lre * lre + lim * lim
    nr = ar - 1.0
    coef_re = (nr * lre + ai * lim) / den
    coef_im = (ai * lre - nr * lim) / den
    bre = bre_ref[...]
    bim = bim_ref[...]
    bbar_re = coef_re * bre - coef_im * bim
    bbar_im = coef_re * bim + coef_im * bre
    row = lax.broadcasted_iota(jnp.int32, (CHUNK_CH, CHUNK_ST), 0)
    lane = lax.broadcasted_iota(jnp.int32, (CHUNK_CH, CHUNK_ST), 1)
    same_group = (row // SSM_GROUP) == (lane // SSM_STATE)
    reps = CHUNK_CH // SSM_GROUP
    for c in range(SSM_CHUNKS):
        cols = slice(c * CHUNK_ST, (c + 1) * CHUNK_ST)
        tre = jnp.concatenate([bbar_re[:, cols]] * reps, axis=0)
        tim = jnp.concatenate([bbar_im[:, cols]] * reps, axis=0)
        bblk_ref[c, :, 0:CHUNK_ST] = jnp.where(same_group, tre, 0.0).astype(_BF16)
        bblk_ref[c, :, CHUNK_ST:2 * CHUNK_ST] = jnp.where(same_group, tim, 0.0).astype(_BF16)
        rows = slice(c * CHUNK_CH, (c + 1) * CHUNK_CH)
        cblk_re_ref[c] = jnp.where(same_group, cre_ref[rows, :], 0.0).astype(_BF16)
        cblk_im_ref[c] = jnp.where(same_group, -cim_ref[rows, :], 0.0).astype(_BF16)


def _ssm_prep(log_dt, lam_re, lam_im, b_re, b_im, c_re, c_im):
    logdt = jnp.repeat(log_dt, SSM_STATE, axis=1).reshape(DEPTH, 1, SSM_FLAT)
    lre = lam_re.reshape(DEPTH, 1, SSM_FLAT)
    lim = lam_im.reshape(DEPTH, 1, SSM_FLAT)
    bre = b_re.transpose(0, 3, 1, 2).reshape(DEPTH, SSM_GROUP, SSM_FLAT)
    bim = b_im.transpose(0, 3, 1, 2).reshape(DEPTH, SSM_GROUP, SSM_FLAT)
    reps = CHUNK_ST // SSM_STATE
    cre = jnp.tile(c_re.reshape(DEPTH, SSM_WIDTH, SSM_STATE), (1, 1, reps))
    cim = jnp.tile(c_im.reshape(DEPTH, SSM_WIDTH, SSM_STATE), (1, 1, reps))

    def spec(*shape):
        return pl.BlockSpec((None,) + shape, lambda i: (i,) + (0,) * len(shape))

    return pl.pallas_call(
        _ssm_prep_kernel,
        grid=(DEPTH,),
        in_specs=[spec(1, SSM_FLAT), spec(1, SSM_FLAT), spec(1, SSM_FLAT),
                  spec(SSM_GROUP, SSM_FLAT), spec(SSM_GROUP, SSM_FLAT),
                  spec(SSM_WIDTH, CHUNK_ST), spec(SSM_WIDTH, CHUNK_ST)],
        out_specs=[spec(1, SSM_FLAT), spec(1, SSM_FLAT),
                   spec(SSM_CHUNKS, CHUNK_CH, 2 * CHUNK_ST),
                   spec(SSM_CHUNKS, CHUNK_CH, CHUNK_ST),
                   spec(SSM_CHUNKS, CHUNK_CH, CHUNK_ST)],
        out_shape=[jax.ShapeDtypeStruct((DEPTH, 1, SSM_FLAT), _F32),
                   jax.ShapeDtypeStruct((DEPTH, 1, SSM_FLAT), _F32),
                   jax.ShapeDtypeStruct((DEPTH, SSM_CHUNKS, CHUNK_CH, 2 * CHUNK_ST), _BF16),
                   jax.ShapeDtypeStruct((DEPTH, SSM_CHUNKS, CHUNK_CH, CHUNK_ST), _BF16),
                   jax.ShapeDtypeStruct((DEPTH, SSM_CHUNKS, CHUNK_CH, CHUNK_ST), _BF16)],
        name="ssm_prep",
    )(logdt, lre, lim, bre, bim, cre, cim)


def _mixer_kernel(x_ref, conv0_ref, sre0_ref, sim0_ref,
                  g_ref, w_in_ref, cw_ref, cb_ref, w_out_a_ref,
                  abar_re_ref, abar_im_ref, bblk_ref, cblk_re_ref, cblk_im_ref, dskip_ref,
                  glu_a_ref, glu_b_ref, w_o_ref,
                  out_ref, conv_out_ref, sre_out_ref, sim_out_ref,
                  vbuf, s_re, s_im, *x_dma, rb, tb):
    rows = rb * tb
    carry_rows = (CONV_WIDTH - 1) * rb

    @pl.when(pl.program_id(0) == 0)
    def _():
        conv_out_ref[...] = conv0_ref[...]
        sre_out_ref[...] = sre0_ref[...]
        sim_out_ref[...] = sim0_ref[...]

    x = _load_time_major(x_ref, (), *x_dma, tb) if x_dma else x_ref[...]
    xn = _rms(x, g_ref).astype(_BF16)

    def proj(lo, hi):
        return _dot(xn, w_in_ref[:, lo:hi])

    h = proj(0, 512)
    gate_c = proj(1024, 1536)
    vbuf[0:carry_rows, :] = conv_out_ref[...]
    vbuf[carry_rows:carry_rows + rows, :] = gate_c * h
    conv_y = (cb_ref[...]
              + vbuf[0:rows, :] * cw_ref[0:1, :]
              + vbuf[rb:rb + rows, :] * cw_ref[1:2, :]
              + vbuf[2 * rb:2 * rb + rows, :] * cw_ref[2:3, :])
    conv_out_ref[...] = vbuf[rows:rows + carry_rows, :]
    gate_b = proj(512, 1024)
    z_a = _dot((gate_b * conv_y).astype(_BF16), w_out_a_ref[...])
    merged = _sigmoid(proj(2048, 3072)) * z_a

    u = proj(1536, 2048)
    ub = u.astype(_BF16)
    for c in range(SSM_CHUNKS):
        bu = _dot(ub[:, c * CHUNK_CH:(c + 1) * CHUNK_CH], bblk_ref[c])
        s_re[:, c * CHUNK_ST:(c + 1) * CHUNK_ST] = bu[:, 0:CHUNK_ST]
        s_im[:, c * CHUNK_ST:(c + 1) * CHUNK_ST] = bu[:, CHUNK_ST:2 * CHUNK_ST]

    for c in range(SSM_CHUNKS):
        cols = slice(c * CHUNK_ST, (c + 1) * CHUNK_ST)
        ar = jnp.broadcast_to(abar_re_ref[:, cols], (rb, CHUNK_ST))
        ai = jnp.broadcast_to(abar_im_ref[:, cols], (rb, CHUNK_ST))

        def step(t, carry, cols=cols, ar=ar, ai=ai):
            sr, si = carry
            r0 = pl.multiple_of(t * rb, rb)
            nsr = ar * sr - ai * si + s_re[pl.ds(r0, rb), cols]
            nsi = ar * si + ai * sr + s_im[pl.ds(r0, rb), cols]
            s_re[pl.ds(r0, rb), cols] = nsr
            s_im[pl.ds(r0, rb), cols] = nsi
            return nsr, nsi

        sr, si = lax.fori_loop(0, tb, step, (sre_out_ref[:, cols], sim_out_ref[:, cols]),
                               unroll=True)
        sre_out_ref[:, cols] = sr
        sim_out_ref[:, cols] = si

    ys = []
    for c in range(SSM_CHUNKS):
        cols = slice(c * CHUNK_ST, (c + 1) * CHUNK_ST)
        ys.append(_dot_nt(s_re[:, cols].astype(_BF16), cblk_re_ref[c])
                  + _dot_nt(s_im[:, cols].astype(_BF16), cblk_im_ref[c]))
    ssm_y = jnp.concatenate(ys, axis=1) + dskip_ref[...] * u
    sy = _gelu(ssm_y).astype(_BF16)
    z_b = _dot(sy, glu_a_ref[...]) * _sigmoid(_dot(sy, glu_b_ref[...]))
    merged = (merged + _sigmoid(proj(3072, 4096)) * z_b).astype(_BF16)
    out_ref[...] = x + _dot(merged, w_o_ref[...])


def _ffn_kernel(x_ref, p_ref, f0_ref,
                g2_ref, w_up_ref, fcw_ref, fcb_ref, w_down_ref,
                g3_ref, w_pg_ref, w_ple_ref, gf_ref,
                out_ref, fconv_out_ref,
                fbuf, *dma, rb, tb, layer, final_norm, p_native, out_native):
    rows = rb * tb
    carry_rows = (CONV_WIDTH - 1) * rb
    dma = list(dma)
    p_dma = [dma.pop(0), dma.pop(0)] if p_native else None
    out_dma = [dma.pop(0), dma.pop(0)] if out_native else None

    @pl.when(pl.program_id(0) == 0)
    def _():
        fconv_out_ref[...] = f0_ref[...]

    x = x_ref[...]
    xn = _rms(x, g2_ref).astype(_BF16)
    acc = x
    for lo, hi in FF_CHUNKS:
        wc = hi - lo
        up_a = _dot(xn, w_up_ref[:, lo:hi])
        up_b = _dot(xn, w_up_ref[:, D_FF + lo:D_FF + hi])
        fbuf[0:carry_rows, 0:wc] = fconv_out_ref[:, lo:hi]
        fbuf[carry_rows:carry_rows + rows, 0:wc] = up_a
        conv_a = (fcb_ref[:, lo:hi]
                  + fbuf[0:rows, 0:wc] * fcw_ref[0:1, lo:hi]
                  + fbuf[rb:rb + rows, 0:wc] * fcw_ref[1:2, lo:hi]
                  + fbuf[2 * rb:2 * rb + rows, 0:wc] * fcw_ref[2:3, lo:hi])
        fconv_out_ref[:, lo:hi] = fbuf[rows:rows + carry_rows, 0:wc]
        hid = (_gelu(conv_a) * up_b).astype(_BF16)
        acc = acc + _dot(hid, w_down_ref[lo:hi, :])
    x2 = acc
    xn3 = _rms(x2, g3_ref).astype(_BF16)
    gate = _sigmoid(_dot(xn3, w_pg_ref[...]))
    p = _load_time_major(p_ref, (layer,), *p_dma, tb) if p_native else p_ref[...]
    pe = _dot(p.astype(_BF16), w_ple_ref[...])
    x3 = x2 + gate * pe
    if final_norm:
        x3 = _rms(x3, gf_ref)
    if out_native:
        _store_batch_major(x3, out_ref, *out_dma, tb)
    else:
        out_ref[...] = x3


def _const_spec(shape, layer=None):
    if layer is None:
        return pl.BlockSpec(shape, lambda j: (0,) * len(shape), pipeline_mode=pl.Buffered(1))
    return pl.BlockSpec((None,) + shape, lambda j: (layer,) + (0,) * len(shape),
                        pipeline_mode=pl.Buffered(1))


def _native_scratch(tb, rb, width):
    return [pltpu.VMEM((2, tb, rb, width), _F32), pltpu.SemaphoreType.DMA((2,))]


def _mixer_call(x, conv0, sre0, sim0, wts, layer, rb, tb, x_native):
    n = x.shape[0] * x.shape[1] if x_native else x.shape[0]
    rows = rb * tb
    carry_rows = (CONV_WIDTH - 1) * rb
    row_spec = pl.BlockSpec((rows, D_MODEL), lambda j: (j, 0))
    in_specs = [
        pl.BlockSpec(memory_space=pl.ANY) if x_native else row_spec,
        _const_spec((carry_rows, CONV_A_WIDTH), layer),
        _const_spec((rb, SSM_FLAT), layer),
        _const_spec((rb, SSM_FLAT), layer),
        _const_spec((1, D_MODEL), layer),
        _const_spec((D_MODEL, IN_PROJ_WIDTH), layer),
        _const_spec((CONV_WIDTH, CONV_A_WIDTH), layer),
        _const_spec((1, CONV_A_WIDTH), layer),
        _const_spec((CONV_A_WIDTH, D_MODEL), layer),
        _const_spec((1, SSM_FLAT), layer),
        _const_spec((1, SSM_FLAT), layer),
        _const_spec((SSM_CHUNKS, CHUNK_CH, 2 * CHUNK_ST), layer),
        _const_spec((SSM_CHUNKS, CHUNK_CH, CHUNK_ST), layer),
        _const_spec((SSM_CHUNKS, CHUNK_CH, CHUNK_ST), layer),
        _const_spec((1, SSM_WIDTH), layer),
        _const_spec((SSM_WIDTH, D_MODEL), layer),
        _const_spec((SSM_WIDTH, D_MODEL), layer),
        _const_spec((D_MODEL, D_MODEL), layer),
    ]
    out_specs = [
        row_spec,
        pl.BlockSpec((carry_rows, CONV_A_WIDTH), lambda j: (0, 0)),
        pl.BlockSpec((rb, SSM_FLAT), lambda j: (0, 0)),
        pl.BlockSpec((rb, SSM_FLAT), lambda j: (0, 0)),
    ]
    out_shape = [
        jax.ShapeDtypeStruct((n, D_MODEL), _F32),
        jax.ShapeDtypeStruct((carry_rows, CONV_A_WIDTH), _F32),
        jax.ShapeDtypeStruct((rb, SSM_FLAT), _F32),
        jax.ShapeDtypeStruct((rb, SSM_FLAT), _F32),
    ]
    scratch = [
        pltpu.VMEM((carry_rows + rows, CONV_A_WIDTH), _F32),
        pltpu.VMEM((rows, SSM_FLAT), _F32),
        pltpu.VMEM((rows, SSM_FLAT), _F32),
    ]
    if x_native:
        scratch += _native_scratch(tb, rb, D_MODEL)
    return pl.pallas_call(
        functools.partial(_mixer_kernel, rb=rb, tb=tb),
        grid=(n // rows,),
        in_specs=in_specs,
        out_specs=out_specs,
        out_shape=out_shape,
        scratch_shapes=scratch,
        compiler_params=pltpu.CompilerParams(
            dimension_semantics=("arbitrary",), vmem_limit_bytes=VMEM_LIMIT_BYTES),
        name=f"mixer_l{layer}_rb{rb}",
    )(x, conv0, sre0, sim0,
      wts["norm_mix"], wts["w_in"], wts["conv_a_w"], wts["conv_a_b"], wts["w_out_a"],
      wts["abar_re"], wts["abar_im"], wts["bblk"], wts["cblk_re"], wts["cblk_im"], wts["d_skip"],
      wts["w_glu_a"], wts["w_glu_b"], wts["w_o"])


def _ffn_call(x, p, f0, wts, layer, rb, tb, final_norm, p_native, out_native):
    n = x.shape[0]
    rows = rb * tb
    carry_rows = (CONV_WIDTH - 1) * rb
    row_spec = pl.BlockSpec((rows, D_MODEL), lambda j: (j, 0))
    in_specs = [
        row_spec,
        (pl.BlockSpec(memory_space=pl.ANY) if p_native
         else pl.BlockSpec((None, rows, PLE_DIM), lambda j: (layer, j, 0))),
        _const_spec((carry_rows, D_FF), layer),
        _const_spec((1, D_MODEL), layer),
        _const_spec((D_MODEL, 2 * D_FF), layer),
        _const_spec((CONV_WIDTH, D_FF), layer),
        _const_spec((1, D_FF), layer),
        _const_spec((D_FF, D_MODEL), layer),
        _const_spec((1, D_MODEL), layer),
        _const_spec((D_MODEL, D_MODEL), layer),
        _const_spec((PLE_DIM, D_MODEL), layer),
        _const_spec((1, D_MODEL)),
    ]
    out_specs = [pl.BlockSpec(memory_space=pl.ANY) if out_native else row_spec,
                 pl.BlockSpec((carry_rows, D_FF), lambda j: (0, 0))]
    out_shape = [jax.ShapeDtypeStruct((rb, n // rb, D_MODEL) if out_native else (n, D_MODEL), _F32),
                 jax.ShapeDtypeStruct((carry_rows, D_FF), _F32)]
    scratch = [pltpu.VMEM((carry_rows + rows, FF_CHUNK_MAX), _F32)]
    if p_native:
        scratch += _native_scratch(tb, rb, PLE_DIM)
    if out_native:
        scratch += _native_scratch(tb, rb, D_MODEL)
    return pl.pallas_call(
        functools.partial(_ffn_kernel, rb=rb, tb=tb, layer=layer, final_norm=final_norm,
                          p_native=p_native, out_native=out_native),
        grid=(n // rows,),
        in_specs=in_specs,
        out_specs=out_specs,
        out_shape=out_shape,
        scratch_shapes=scratch,
        compiler_params=pltpu.CompilerParams(
            dimension_semantics=("arbitrary",), vmem_limit_bytes=VMEM_LIMIT_BYTES),
        name=f"ffn_l{layer}_rb{rb}",
    )(x, p, f0,
      wts["norm_ffn"], wts["w_up"], wts["ffn_conv_w"], wts["ffn_conv_b"], wts["w_down"],
      wts["norm_ple"], wts["w_ple_gate"], wts["w_ple"], wts["norm_final"])


def _run_trunk(x, p, conv0, sre0, sim0, f0, wts, rb, tb, native):
    convs, sres, sims, ffns = [], [], [], []
    for layer in range(DEPTH):
        last = layer == DEPTH - 1
        x, conv_n, sre_n, sim_n = _mixer_call(x, conv0, sre0, sim0, wts, layer, rb, tb,
                                              x_native=native and layer == 0)
        x, ffn_n = _ffn_call(x, p, f0, wts, layer, rb, tb, final_norm=last,
                             p_native=native, out_native=native and last)
        convs.append(conv_n)
        sres.append(sre_n)
        sims.append(sim_n)
        ffns.append(ffn_n)
    return x, jnp.stack(convs), jnp.stack(sres), jnp.stack(sims), jnp.stack(ffns)


def _to_time_major_state(s):
    d, b, k, c = s.shape
    return s.transpose(0, 2, 1, 3).reshape(d, k * b, c)


def _from_time_major_state(s, b):
    d, kb, c = s.shape
    return s.reshape(d, kb // b, b, c).transpose(0, 2, 1, 3)


def kernel(x_prompt, x_sample, p_prompt, p_sample, state_conv_a, state_ssm_re, state_ssm_im, state_ffn_conv, norm_mix, w_in, conv_a_w, conv_a_b, w_out_a, log_dt, lam_re, lam_im, b_re, b_im, c_re, c_im, d_skip, w_glu_a, w_glu_b, w_o, norm_ffn, w_up, ffn_conv_w, ffn_conv_b, w_down, norm_ple, w_ple_gate, w_ple, norm_final):
    abar_re, abar_im, bblk, cblk_re, cblk_im = _ssm_prep(log_dt, lam_re, lam_im, b_re, b_im, c_re, c_im)
    wts = {
        "norm_mix": norm_mix.reshape(DEPTH, 1, D_MODEL),
        "w_in": w_in.astype(_BF16),
        "conv_a_w": conv_a_w,
        "conv_a_b": conv_a_b.reshape(DEPTH, 1, CONV_A_WIDTH),
        "w_out_a": w_out_a.astype(_BF16),
        "abar_re": abar_re, "abar_im": abar_im,
        "bblk": bblk, "cblk_re": cblk_re, "cblk_im": cblk_im,
        "d_skip": d_skip.reshape(DEPTH, 1, SSM_WIDTH),
        "w_glu_a": w_glu_a.astype(_BF16),
        "w_glu_b": w_glu_b.astype(_BF16),
        "w_o": w_o.astype(_BF16),
        "norm_ffn": norm_ffn.reshape(DEPTH, 1, D_MODEL),
        "w_up": w_up.astype(_BF16),
        "ffn_conv_w": ffn_conv_w,
        "ffn_conv_b": ffn_conv_b.reshape(DEPTH, 1, D_FF),
        "w_down": w_down.astype(_BF16),
        "norm_ple": norm_ple.reshape(DEPTH, 1, D_MODEL),
        "w_ple_gate": w_ple_gate.astype(_BF16),
        "w_ple": w_ple.astype(_BF16),
        "norm_final": norm_final.reshape(1, D_MODEL),
    }

    bp = x_prompt.shape[0]
    zc = jnp.zeros((DEPTH, (CONV_WIDTH - 1) * bp, CONV_A_WIDTH), _F32)
    zs = jnp.zeros((DEPTH, bp, SSM_FLAT), _F32)
    zf = jnp.zeros((DEPTH, (CONV_WIDTH - 1) * bp, D_FF), _F32)
    y_prompt, conv_p, sre_p, sim_p, ffn_p = _run_trunk(
        x_prompt, p_prompt, zc, zs, zs, zf, wts, bp, PROMPT_TIME_BLOCK, native=True)

    bs, sseq, _ = x_sample.shape
    xs = x_sample.transpose(1, 0, 2).reshape(sseq * bs, D_MODEL)
    ps = p_sample.transpose(0, 2, 1, 3).reshape(DEPTH, sseq * bs, PLE_DIM)
    ys, conv_s, sre_s, sim_s, ffn_s = _run_trunk(
        xs, ps, _to_time_major_state(state_conv_a),
        state_ssm_re.reshape(DEPTH, bs, SSM_FLAT), state_ssm_im.reshape(DEPTH, bs, SSM_FLAT),
        _to_time_major_state(state_ffn_conv), wts, bs, sseq, native=False)
    y_sample = ys.reshape(sseq, bs, D_MODEL).transpose(1, 0, 2)

    grp = (SSM_GROUPS, SSM_STATE)
    return (y_prompt, y_sample,
            _from_time_major_state(conv_p, bp), sre_p.reshape(DEPTH, bp, *grp),
            sim_p.reshape(DEPTH, bp, *grp), _from_time_major_state(ffn_p, bp),
            _from_time_major_state(conv_s, bs), sre_s.reshape(DEPTH, bs, *grp),
            sim_s.reshape(DEPTH, bs, *grp), _from_time_major_state(ffn_s, bs))
```

```python
import functools
import math

import jax
import jax.numpy as jnp
from jax import lax
from jax.experimental import pallas as pl
from jax.experimental.pallas import tpu as pltpu

D_MODEL = 1024
DEPTH = 2
CONV_WIDTH = 3
CONV_A_WIDTH = 512
SSM_WIDTH = 512
SSM_GROUP = 16
SSM_GROUPS = 32
SSM_STATE = 64
SSM_FLAT = SSM_GROUPS * SSM_STATE
D_FF = 2816
PLE_DIM = 256
NORM_EPS = 1e-6
IN_PROJ_WIDTH = 4096

SSM_CHUNKS = 4
CHUNK_CH = SSM_WIDTH // SSM_CHUNKS
CHUNK_ST = SSM_FLAT // SSM_CHUNKS

FF_CHUNKS = ((0, 1024), (1024, 2048), (2048, 2816))
FF_CHUNK_MAX = 1024

MIXER_TIME_BLOCK = 64
FFN_TIME_BLOCK = 128
VMEM_LIMIT_BYTES = 56 * 1024 * 1024

_BF16 = jnp.bfloat16
_F32 = jnp.float32


def _dot(a, b):
    return jnp.dot(a, b, preferred_element_type=_F32)


def _dot_nt(a, b, precision=None):
    return lax.dot_general(a, b, (((1,), (1,)), ((), ())), preferred_element_type=_F32,
                           precision=precision)


def _rms(x, g_ref):
    ms = jnp.mean(x * x, axis=-1, keepdims=True)
    return x * lax.rsqrt(ms + NORM_EPS) * g_ref[...]


def _gelu(x):
    c = math.sqrt(2.0 / math.pi)
    return x * (0.5 * (1.0 + jnp.tanh(c * (x + 0.044715 * (x * x * x)))))


def _sigmoid(x):
    return 0.5 * jnp.tanh(0.5 * x) + 0.5


def _seq_copies(hbm_ref, lead, buf, sem, step, slot, tb, to_hbm):
    copies = []
    for b in range(buf.shape[2]):
        hbm_view = hbm_ref.at[lead + (b, pl.ds(step * tb, tb))]
        vmem_view = buf.at[slot, :, b]
        src, dst = (vmem_view, hbm_view) if to_hbm else (hbm_view, vmem_view)
        copies.append(pltpu.make_async_copy(src, dst, sem.at[slot]))
    return copies


def _load_time_major(hbm_ref, lead, buf, sem, tb):
    j = pl.program_id(0)
    slot = j % 2

    @pl.when(j == 0)
    def _():
        for c in _seq_copies(hbm_ref, lead, buf, sem, 0, 0, tb, False):
            c.start()

    @pl.when(j + 1 < pl.num_programs(0))
    def _():
        for c in _seq_copies(hbm_ref, lead, buf, sem, j + 1, 1 - slot, tb, False):
            c.start()

    for c in _seq_copies(hbm_ref, lead, buf, sem, j, slot, tb, False):
        c.wait()
    return buf[slot].reshape(tb * buf.shape[2], buf.shape[3])


def _store_batch_major(val, hbm_ref, buf, sem, tb):
    j = pl.program_id(0)
    slot = j % 2
    buf[slot] = val.reshape(tb, buf.shape[2], buf.shape[3])
    for c in _seq_copies(hbm_ref, (), buf, sem, j, slot, tb, True):
        c.start()

    @pl.when(j >= 1)
    def _():
        for c in _seq_copies(hbm_ref, (), buf, sem, j - 1, 1 - slot, tb, True):
            c.wait()

    @pl.when(j == pl.num_programs(0) - 1)
    def _():
        for c in _seq_copies(hbm_ref, (), buf, sem, j, slot, tb, True):
            c.wait()


def _ssm_prep_kernel(logdt_ref, lre_ref, lim_ref, bre_ref, bim_ref, cre_ref, cim_ref,
                     abar_ref, b2blk_ref, c2_re_ref, c2_im_ref, fblk_ref):
    dt = jnp.exp(logdt_ref[...])
    lre = lre_ref[...]
    lim = lim_ref[...]
    mag = jnp.exp(lre * dt)
    ang = lim * dt
    ar = mag * jnp.cos(ang)
    ai = mag * jnp.sin(ang)
    abar_ref[0:1, :] = ar
    abar_ref[1:2, :] = ai
    abar_ref[2:3, :] = ar * ar - ai * ai
    abar_ref[3:4, :] = 2.0 * (ar * ai)
    den = lre * lre + lim * lim
    nr = ar - 1.0
    coef_re = (nr * lre + ai * lim) / den
    coef_im = (ai * lre - nr * lim) / den
    bre = bre_ref[...]
    bim = bim_ref[...]
    bbar_re = coef_re * bre - coef_im * bim
    bbar_im = coef_re * bim + coef_im * bre
    abb_re = ar * bbar_re - ai * bbar_im
    abb_im = ar * bbar_im + ai * bbar_re
    row = lax.broadcasted_iota(jnp.int32, (CHUNK_CH, CHUNK_ST), 0)
    lane = lax.broadcasted_iota(jnp.int32, (CHUNK_CH, CHUNK_ST), 1)
    same_group = (row // SSM_GROUP) == (lane // SSM_STATE)
    reps = CHUNK_CH // SSM_GROUP
    for c in range(SSM_CHUNKS):
        cols = slice(c * CHUNK_ST, (c + 1) * CHUNK_ST)
        rows = slice(c * CHUNK_CH, (c + 1) * CHUNK_CH)

        def b_block(v, cols=cols):
            return jnp.where(same_group, jnp.concatenate([v[:, cols]] * reps, axis=0), 0.0)

        b_re = b_block(bbar_re)
        b_im = b_block(bbar_im)
        b2blk_ref[c, 0:CHUNK_CH, 0:CHUNK_ST] = b_block(abb_re).astype(_BF16)
        b2blk_ref[c, 0:CHUNK_CH, CHUNK_ST:2 * CHUNK_ST] = b_block(abb_im).astype(_BF16)
        b2blk_ref[c, CHUNK_CH:2 * CHUNK_CH, 0:CHUNK_ST] = b_re.astype(_BF16)
        b2blk_ref[c, CHUNK_CH:2 * CHUNK_CH, CHUNK_ST:2 * CHUNK_ST] = b_im.astype(_BF16)
        c_re = jnp.where(same_group, cre_ref[rows, :], 0.0)
        c_im = jnp.where(same_group, cim_ref[rows, :], 0.0)
        arc = ar[:, cols]
        aic = ai[:, cols]
        c2_re_ref[c, 0:CHUNK_CH, :] = c_re.astype(_BF16)
        c2_re_ref[c, CHUNK_CH:2 * CHUNK_CH, :] = (c_re * arc - c_im * aic).astype(_BF16)
        c2_im_ref[c, 0:CHUNK_CH, :] = (-c_im).astype(_BF16)
        c2_im_ref[c, CHUNK_CH:2 * CHUNK_CH, :] = (-(c_re * aic + c_im * arc)).astype(_BF16)
        feed = (_dot_nt(b_re, c_re, precision=lax.Precision.HIGHEST)
                - _dot_nt(b_im, c_im, precision=lax.Precision.HIGHEST))
        fblk_ref[c] = feed.astype(_BF16)


def _ssm_prep(log_dt, lam_re, lam_im, b_re, b_im, c_re, c_im):
    logdt = jnp.repeat(log_dt, SSM_STATE, axis=1).reshape(DEPTH, 1, SSM_FLAT)
    lre = lam_re.reshape(DEPTH, 1, SSM_FLAT)
    lim = lam_im.reshape(DEPTH, 1, SSM_FLAT)
    bre = b_re.transpose(0, 3, 1, 2).reshape(DEPTH, SSM_GROUP, SSM_FLAT)
    bim = b_im.transpose(0, 3, 1, 2).reshape(DEPTH, SSM_GROUP, SSM_FLAT)
    reps = CHUNK_ST // SSM_STATE
    cre = jnp.tile(c_re.reshape(DEPTH, SSM_WIDTH, SSM_STATE), (1, 1, reps))
    cim = jnp.tile(c_im.reshape(DEPTH, SSM_WIDTH, SSM_STATE), (1, 1, reps))

    def spec(*shape):
        return pl.BlockSpec((None,) + shape, lambda i: (i,) + (0,) * len(shape))

    return pl.pallas_call(
        _ssm_prep_kernel,
        grid=(DEPTH,),
        in_specs=[spec(1, SSM_FLAT), spec(1, SSM_FLAT), spec(1, SSM_FLAT),
                  spec(SSM_GROUP, SSM_FLAT), spec(SSM_GROUP, SSM_FLAT),
                  spec(SSM_WIDTH, CHUNK_ST), spec(SSM_WIDTH, CHUNK_ST)],
        out_specs=[spec(4, SSM_FLAT),
                   spec(SSM_CHUNKS, 2 * CHUNK_CH, 2 * CHUNK_ST),
                   spec(SSM_CHUNKS, 2 * CHUNK_CH, CHUNK_ST),
                   spec(SSM_CHUNKS, 2 * CHUNK_CH, CHUNK_ST),
                   spec(SSM_CHUNKS, CHUNK_CH, CHUNK_CH)],
        out_shape=[jax.ShapeDtypeStruct((DEPTH, 4, SSM_FLAT), _F32),
                   jax.ShapeDtypeStruct((DEPTH, SSM_CHUNKS, 2 * CHUNK_CH, 2 * CHUNK_ST), _BF16),
                   jax.ShapeDtypeStruct((DEPTH, SSM_CHUNKS, 2 * CHUNK_CH, CHUNK_ST), _BF16),
                   jax.ShapeDtypeStruct((DEPTH, SSM_CHUNKS, 2 * CHUNK_CH, CHUNK_ST), _BF16),
                   jax.ShapeDtypeStruct((DEPTH, SSM_CHUNKS, CHUNK_CH, CHUNK_CH), _BF16)],
        name="ssm_prep",
    )(logdt, lre, lim, bre, bim, cre, cim)


def _mixer_kernel(x_ref, conv0_ref, sre0_ref, sim0_ref,
                  g_ref, w_in_ref, cw_ref, cb_ref, w_out_a_ref,
                  abar_ref, b2blk_ref, c2_re_ref, c2_im_ref, fblk_ref, dskip_ref,
                  glu_a_ref, glu_b_ref, w_o_ref,
                  out_ref, conv_out_ref, sre_out_ref, sim_out_ref,
                  vbuf, s_re, s_im, *x_dma, rb, tb):
    rows = rb * tb
    carry_rows = (CONV_WIDTH - 1) * rb

    @pl.when(pl.program_id(0) == 0)
    def _():
        conv_out_ref[...] = conv0_ref[...]
        sre_out_ref[...] = sre0_ref[...]
        sim_out_ref[...] = sim0_ref[...]

    x = _load_time_major(x_ref, (), *x_dma, tb) if x_dma else x_ref[...]
    xn = _rms(x, g_ref).astype(_BF16)

    def proj(lo, hi):
        return _dot(xn, w_in_ref[:, lo:hi])

    h = proj(0, 512)
    gate_c = proj(1024, 1536)
    vbuf[0:carry_rows, :] = conv_out_ref[...]
    vbuf[carry_rows:carry_rows + rows, :] = gate_c * h
    conv_y = (cb_ref[...]
              + vbuf[0:rows, :] * cw_ref[0:1, :]
              + vbuf[rb:rb + rows, :] * cw_ref[1:2, :]
              + vbuf[2 * rb:2 * rb + rows, :] * cw_ref[2:3, :])
    conv_out_ref[...] = vbuf[rows:rows + carry_rows, :]
    gate_b = proj(512, 1024)
    z_a = _dot((gate_b * conv_y).astype(_BF16), w_out_a_ref[...])
    merged = _sigmoid(proj(2048, 3072)) * z_a

    u = proj(1536, 2048)
    pair = 2 if tb % 2 == 0 else 1
    steps = tb // pair
    srows = steps * rb
    if pair == 2:
        u4 = u.reshape(steps, 2, rb, SSM_WIDTH)
        u_parts = [u4[:, i].reshape(srows, SSM_WIDTH) for i in range(2)]
    else:
        u_parts = [u]
    ub_parts = [v.astype(_BF16) for v in u_parts]
    s_re[0:rb, :] = sre_out_ref[...]
    s_im[0:rb, :] = sim_out_ref[...]
    for c in range(SSM_CHUNKS):
        ch = slice(c * CHUNK_CH, (c + 1) * CHUNK_CH)
        if pair == 2:
            inc = _dot(jnp.concatenate([ub_parts[0][:, ch], ub_parts[1][:, ch]], axis=1), b2blk_ref[c])
        else:
            inc = _dot(ub_parts[0][:, ch], b2blk_ref[c, CHUNK_CH:2 * CHUNK_CH, :])
        s_re[rb:rb + srows, c * CHUNK_ST:(c + 1) * CHUNK_ST] = inc[:, 0:CHUNK_ST]
        s_im[rb:rb + srows, c * CHUNK_ST:(c + 1) * CHUNK_ST] = inc[:, CHUNK_ST:2 * CHUNK_ST]

    a_row = 2 * (pair - 1)
    for c in range(SSM_CHUNKS):
        cols = slice(c * CHUNK_ST, (c + 1) * CHUNK_ST)
        ar = jnp.broadcast_to(abar_ref[a_row:a_row + 1, cols], (rb, CHUNK_ST))
        ai = jnp.broadcast_to(abar_ref[a_row + 1:a_row + 2, cols], (rb, CHUNK_ST))

        def step(k, carry, cols=cols, ar=ar, ai=ai):
            sr, si = carry
            r0 = pl.multiple_of(rb + k * rb, rb)
            nsr = ar * sr - ai * si + s_re[pl.ds(r0, rb), cols]
            nsi = ar * si + ai * sr + s_im[pl.ds(r0, rb), cols]
            s_re[pl.ds(r0, rb), cols] = nsr
            s_im[pl.ds(r0, rb), cols] = nsi
            return nsr, nsi

        sr, si = lax.fori_loop(0, steps, step, (s_re[0:rb, cols], s_im[0:rb, cols]), unroll=True)
        sre_out_ref[:, cols] = sr
        sim_out_ref[:, cols] = si

    y_parts = [[] for _ in range(pair)]
    for c in range(SSM_CHUNKS):
        ch = slice(c * CHUNK_CH, (c + 1) * CHUNK_CH)
        cols = slice(c * CHUNK_ST, (c + 1) * CHUNK_ST)
        if pair == 2:
            y2 = (_dot_nt(s_re[:, cols].astype(_BF16), c2_re_ref[c])
                  + _dot_nt(s_im[:, cols].astype(_BF16), c2_im_ref[c]))
            y_parts[0].append(y2[0:srows, CHUNK_CH:2 * CHUNK_CH] + _dot(ub_parts[0][:, ch], fblk_ref[c]))
            y_parts[1].append(y2[rb:rb + srows, 0:CHUNK_CH])
        else:
            y_parts[0].append(
                _dot_nt(s_re[rb:rb + srows, cols].astype(_BF16), c2_re_ref[c, 0:CHUNK_CH, :])
                + _dot_nt(s_im[rb:rb + srows, cols].astype(_BF16), c2_im_ref[c, 0:CHUNK_CH, :]))
    ssm_parts = [jnp.concatenate(yp, axis=1) + dskip_ref[...] * up for yp, up in zip(y_parts, u_parts)]
    if pair == 2:
        ssm_y = jnp.stack([v.reshape(steps, rb, SSM_WIDTH) for v in ssm_parts],
                          axis=1).reshape(rows, SSM_WIDTH)
    else:
        ssm_y = ssm_parts[0]
    sy = _gelu(ssm_y).astype(_BF16)
    z_b = _dot(sy, glu_a_ref[...]) * _sigmoid(_dot(sy, glu_b_ref[...]))
    merged = (merged + _sigmoid(proj(3072, 4096)) * z_b).astype(_BF16)
    out_ref[...] = x + _dot(merged, w_o_ref[...])


def _ffn_kernel(x_ref, p_ref, f0_ref,
                g2_ref, w_up_ref, fcw_ref, fcb_ref, w_down_ref,
                g3_ref, w_pg_ref, w_ple_ref, gf_ref,
                out_ref, fconv_out_ref,
                fbuf, *dma, rb, tb, layer, final_norm, p_native, out_native):
    rows = rb * tb
    carry_rows = (CONV_WIDTH - 1) * rb
    dma = list(dma)
    p_dma = [dma.pop(0), dma.pop(0)] if p_native else None
    out_dma = [dma.pop(0), dma.pop(0)] if out_native else None

    @pl.when(pl.program_id(0) == 0)
    def _():
        fconv_out_ref[...] = f0_ref[...]

    x = x_ref[...]
    xn = _rms(x, g2_ref).astype(_BF16)
    acc = x
    for lo, hi in FF_CHUNKS:
        wc = hi - lo
        up_a = _dot(xn, w_up_ref[:, lo:hi])
        up_b = _dot(xn, w_up_ref[:, D_FF + lo:D_FF + hi])
        fbuf[0:carry_rows, 0:wc] = fconv_out_ref[:, lo:hi]
        fbuf[carry_rows:carry_rows + rows, 0:wc] = up_a
        conv_a = (fcb_ref[:, lo:hi]
                  + fbuf[0:rows, 0:wc] * fcw_ref[0:1, lo:hi]
                  + fbuf[rb:rb + rows, 0:wc] * fcw_ref[1:2, lo:hi]
                  + fbuf[2 * rb:2 * rb + rows, 0:wc] * fcw_ref[2:3, lo:hi])
        fconv_out_ref[:, lo:hi] = fbuf[rows:rows + carry_rows, 0:wc]
        hid = (_gelu(conv_a) * up_b).astype(_BF16)
        acc = acc + _dot(hid, w_down_ref[lo:hi, :])
    x2 = acc
    xn3 = _rms(x2, g3_ref).astype(_BF16)
    gate = _sigmoid(_dot(xn3, w_pg_ref[...]))
    p = _load_time_major(p_ref, (layer,), *p_dma, tb) if p_native else p_ref[...]
    pe = _dot(p.astype(_BF16), w_ple_ref[...])
    x3 = x2 + gate * pe
    if final_norm:
        x3 = _rms(x3, gf_ref)
    if out_native:
        _store_batch_major(x3, out_ref, *out_dma, tb)
    else:
        out_ref[...] = x3


def _const_spec(shape, layer=None):
    if layer is None:
        return pl.BlockSpec(shape, lambda j: (0,) * len(shape), pipeline_mode=pl.Buffered(1))
    return pl.BlockSpec((None,) + shape, lambda j: (layer,) + (0,) * len(shape),
                        pipeline_mode=pl.Buffered(1))


def _native_scratch(tb, rb, width):
    return [pltpu.VMEM((2, tb, rb, width), _F32), pltpu.SemaphoreType.DMA((2,))]


def _mixer_call(x, conv0, sre0, sim0, wts, layer, rb, tb, x_native):
    n = x.shape[0] * x.shape[1] if x_native else x.shape[0]
    rows = rb * tb
    carry_rows = (CONV_WIDTH - 1) * rb
    row_spec = pl.BlockSpec((rows, D_MODEL), lambda j: (j, 0))
    in_specs = [
        pl.BlockSpec(memory_space=pl.ANY) if x_native else row_spec,
        _const_spec((carry_rows, CONV_A_WIDTH), layer),
        _const_spec((rb, SSM_FLAT), layer),
        _const_spec((rb, SSM_FLAT), layer),
        _const_spec((1, D_MODEL), layer),
        _const_spec((D_MODEL, IN_PROJ_WIDTH), layer),
        _const_spec((CONV_WIDTH, CONV_A_WIDTH), layer),
        _const_spec((1, CONV_A_WIDTH), layer),
        _const_spec((CONV_A_WIDTH, D_MODEL), layer),
        _const_spec((4, SSM_FLAT), layer),
        _const_spec((SSM_CHUNKS, 2 * CHUNK_CH, 2 * CHUNK_ST), layer),
        _const_spec((SSM_CHUNKS, 2 * CHUNK_CH, CHUNK_ST), layer),
        _const_spec((SSM_CHUNKS, 2 * CHUNK_CH, CHUNK_ST), layer),
        _const_spec((SSM_CHUNKS, CHUNK_CH, CHUNK_CH), layer),
        _const_spec((1, SSM_WIDTH), layer),
        _const_spec((SSM_WIDTH, D_MODEL), layer),
        _const_spec((SSM_WIDTH, D_MODEL), layer),
        _const_spec((D_MODEL, D_MODEL), layer),
    ]
    out_specs = [
        row_spec,
        pl.BlockSpec((carry_rows, CONV_A_WIDTH), lambda j: (0, 0)),
        pl.BlockSpec((rb, SSM_FLAT), lambda j: (0, 0)),
        pl.BlockSpec((rb, SSM_FLAT), lambda j: (0, 0)),
    ]
    out_shape = [
        jax.ShapeDtypeStruct((n, D_MODEL), _F32),
        jax.ShapeDtypeStruct((carry_rows, CONV_A_WIDTH), _F32),
        jax.ShapeDtypeStruct((rb, SSM_FLAT), _F32),
        jax.ShapeDtypeStruct((rb, SSM_FLAT), _F32),
    ]
    s_rows = rows // 2 if tb % 2 == 0 else rows
    scratch = [
        pltpu.VMEM((carry_rows + rows, CONV_A_WIDTH), _F32),
        pltpu.VMEM((rb + s_rows, SSM_FLAT), _F32),
        pltpu.VMEM((rb + s_rows, SSM_FLAT), _F32),
    ]
    if x_native:
        scratch += _native_scratch(tb, rb, D_MODEL)
    return pl.pallas_call(
        functools.partial(_mixer_kernel, rb=rb, tb=tb),
        grid=(n // rows,),
        in_specs=in_specs,
        out_specs=out_specs,
        out_shape=out_shape,
        scratch_shapes=scratch,
        compiler_params=pltpu.CompilerParams(
            dimension_semantics=("arbitrary",), vmem_limit_bytes=VMEM_LIMIT_BYTES),
        name=f"mixer_l{layer}_rb{rb}",
    )(x, conv0, sre0, sim0,
      wts["norm_mix"], wts["w_in"], wts["conv_a_w"], wts["conv_a_b"], wts["w_out_a"],
      wts["abar"], wts["b2blk"], wts["c2_re"], wts["c2_im"], wts["fblk"], wts["d_skip"],
      wts["w_glu_a"], wts["w_glu_b"], wts["w_o"])


def _ffn_call(x, p, f0, wts, layer, rb, tb, final_norm, p_native, out_native):
    n = x.shape[0]
    rows = rb * tb
    carry_rows = (CONV_WIDTH - 1) * rb
    row_spec = pl.BlockSpec((rows, D_MODEL), lambda j: (j, 0))
    in_specs = [
        row_spec,
        (pl.BlockSpec(memory_space=pl.ANY) if p_native
         else pl.BlockSpec((None, rows, PLE_DIM), lambda j: (layer, j, 0))),
        _const_spec((carry_rows, D_FF), layer),
        _const_spec((1, D_MODEL), layer),
        _const_spec((D_MODEL, 2 * D_FF), layer),
        _const_spec((CONV_WIDTH, D_FF), layer),
        _const_spec((1, D_FF), layer),
        _const_spec((D_FF, D_MODEL), layer),
        _const_spec((1, D_MODEL), layer),
        _const_spec((D_MODEL, D_MODEL), layer),
        _const_spec((PLE_DIM, D_MODEL), layer),
        _const_spec((1, D_MODEL)),
    ]
    out_specs = [pl.BlockSpec(memory_space=pl.ANY) if out_native else row_spec,
                 pl.BlockSpec((carry_rows, D_FF), lambda j: (0, 0))]
    out_shape = [jax.ShapeDtypeStruct((rb, n // rb, D_MODEL) if out_native else (n, D_MODEL), _F32),
                 jax.ShapeDtypeStruct((carry_rows, D_FF), _F32)]
    scratch = [pltpu.VMEM((carry_rows + rows, FF_CHUNK_MAX), _F32)]
    if p_native:
        scratch += _native_scratch(tb, rb, PLE_DIM)
    if out_native:
        scratch += _native_scratch(tb, rb, D_MODEL)
    return pl.pallas_call(
        functools.partial(_ffn_kernel, rb=rb, tb=tb, layer=layer, final_norm=final_norm,
                          p_native=p_native, out_native=out_native),
        grid=(n // rows,),
        in_specs=in_specs,
        out_specs=out_specs,
        out_shape=out_shape,
        scratch_shapes=scratch,
        compiler_params=pltpu.CompilerParams(
            dimension_semantics=("arbitrary",), vmem_limit_bytes=VMEM_LIMIT_BYTES),
        name=f"ffn_l{layer}_rb{rb}",
    )(x, p, f0,
      wts["norm_ffn"], wts["w_up"], wts["ffn_conv_w"], wts["ffn_conv_b"], wts["w_down"],
      wts["norm_ple"], wts["w_ple_gate"], wts["w_ple"], wts["norm_final"])


def _run_trunk(x, p, conv0, sre0, sim0, f0, wts, rb, tb_mixer, tb_ffn, native):
    convs, sres, sims, ffns = [], [], [], []
    for layer in range(DEPTH):
        last = layer == DEPTH - 1
        x, conv_n, sre_n, sim_n = _mixer_call(x, conv0, sre0, sim0, wts, layer, rb, tb_mixer,
                                              x_native=native and layer == 0)
        x, ffn_n = _ffn_call(x, p, f0, wts, layer, rb, tb_ffn, final_norm=last,
                             p_native=native, out_native=native and last)
        convs.append(conv_n)
        sres.append(sre_n)
        sims.append(sim_n)
        ffns.append(ffn_n)
    return x, jnp.stack(convs), jnp.stack(sres), jnp.stack(sims), jnp.stack(ffns)


def _to_time_major_state(s):
    d, b, k, c = s.shape
    return s.transpose(0, 2, 1, 3).reshape(d, k * b, c)


def _from_time_major_state(s, b):
    d, kb, c = s.shape
    return s.reshape(d, kb // b, b, c).transpose(0, 2, 1, 3)


def kernel(x_prompt, x_sample, p_prompt, p_sample, state_conv_a, state_ssm_re, state_ssm_im, state_ffn_conv, norm_mix, w_in, conv_a_w, conv_a_b, w_out_a, log_dt, lam_re, lam_im, b_re, b_im, c_re, c_im, d_skip, w_glu_a, w_glu_b, w_o, norm_ffn, w_up, ffn_conv_w, ffn_conv_b, w_down, norm_ple, w_ple_gate, w_ple, norm_final):
    abar, b2blk, c2_re, c2_im, fblk = _ssm_prep(log_dt, lam_re, lam_im, b_re, b_im, c_re, c_im)
    wts = {
        "norm_mix": norm_mix.reshape(DEPTH, 1, D_MODEL),
        "w_in": w_in.astype(_BF16),
        "conv_a_w": conv_a_w,
        "conv_a_b": conv_a_b.reshape(DEPTH, 1, CONV_A_WIDTH),
        "w_out_a": w_out_a.astype(_BF16),
        "abar": abar, "b2blk": b2blk, "c2_re": c2_re, "c2_im": c2_im, "fblk": fblk,
        "d_skip": d_skip.reshape(DEPTH, 1, SSM_WIDTH),
        "w_glu_a": w_glu_a.astype(_BF16),
        "w_glu_b": w_glu_b.astype(_BF16),
        "w_o": w_o.astype(_BF16),
        "norm_ffn": norm_ffn.reshape(DEPTH, 1, D_MODEL),
        "w_up": w_up.astype(_BF16),
        "ffn_conv_w": ffn_conv_w,
        "ffn_conv_b": ffn_conv_b.reshape(DEPTH, 1, D_FF),
        "w_down": w_down.astype(_BF16),
        "norm_ple": norm_ple.reshape(DEPTH, 1, D_MODEL),
        "w_ple_gate": w_ple_gate.astype(_BF16),
        "w_ple": w_ple.astype(_BF16),
        "norm_final": norm_final.reshape(1, D_MODEL),
    }

    bp = x_prompt.shape[0]
    zc = jnp.zeros((DEPTH, (CONV_WIDTH - 1) * bp, CONV_A_WIDTH), _F32)
    zs = jnp.zeros((DEPTH, bp, SSM_FLAT), _F32)
    zf = jnp.zeros((DEPTH, (CONV_WIDTH - 1) * bp, D_FF), _F32)
    y_prompt, conv_p, sre_p, sim_p, ffn_p = _run_trunk(
        x_prompt, p_prompt, zc, zs, zs, zf, wts, bp, MIXER_TIME_BLOCK, FFN_TIME_BLOCK, native=True)

    bs, sseq, _ = x_sample.shape
    xs = x_sample.transpose(1, 0, 2).reshape(sseq * bs, D_MODEL)
    ps = p_sample.transpose(0, 2, 1, 3).reshape(DEPTH, sseq * bs, PLE_DIM)
    ys, conv_s, sre_s, sim_s, ffn_s = _run_trunk(
        xs, ps, _to_time_major_state(state_conv_a),
        state_ssm_re.reshape(DEPTH, bs, SSM_FLAT), state_ssm_im.reshape(DEPTH, bs, SSM_FLAT),
        _to_time_major_state(state_ffn_conv), wts, bs, sseq, sseq, native=False)
    y_sample = ys.reshape(sseq, bs, D_MODEL).transpose(1, 0, 2)

    grp = (SSM_GROUPS, SSM_STATE)
    return (y_prompt, y_sample,
            _from_time_major_state(conv_p, bp), sre_p.reshape(DEPTH, bp, *grp),
            sim_p.reshape(DEPTH, bp, *grp), _from_time_major_state(ffn_p, bp),
            _from_time_major_state(conv_s, bs), sre_s.reshape(DEPTH, bs, *grp),
            sim_s.reshape(DEPTH, bs, *grp), _from_time_major_state(ffn_s, bs))
```

```python
import functools
import math

import jax
import jax.numpy as jnp
from jax import lax
from jax.experimental import pallas as pl
from jax.experimental.pallas import tpu as pltpu

D_MODEL = 1024
DEPTH = 2
CONV_WIDTH = 3
CONV_A_WIDTH = 512
SSM_WIDTH = 512
SSM_GROUP = 16
SSM_GROUPS = 32
SSM_STATE = 64
SSM_FLAT = SSM_GROUPS * SSM_STATE
D_FF = 2816
PLE_DIM = 256
NORM_EPS = 1e-6
IN_PROJ_WIDTH = 4096

SSM_CHUNKS = 4
CHUNK_CH = SSM_WIDTH // SSM_CHUNKS
CHUNK_ST = SSM_FLAT // SSM_CHUNKS

FF_CHUNKS = ((0, 1024), (1024, 2048), (2048, 2816))
FF_CHUNK_MAX = 1024

MIXER_TIME_BLOCK = 128
FFN_TIME_BLOCK = 128
VMEM_LIMIT_BYTES = 56 * 1024 * 1024

_BF16 = jnp.bfloat16
_F32 = jnp.float32


def _dot(a, b):
    return jnp.dot(a, b, preferred_element_type=_F32)


def _dot_nt(a, b, precision=None):
    return lax.dot_general(a, b, (((1,), (1,)), ((), ())), preferred_element_type=_F32,
                           precision=precision)


def _rms(x, g_ref):
    ms = jnp.mean(x * x, axis=-1, keepdims=True)
    return x * lax.rsqrt(ms + NORM_EPS) * g_ref[...]


def _gelu(x):
    c = math.sqrt(2.0 / math.pi)
    return x * (0.5 * (1.0 + jnp.tanh(c * (x + 0.044715 * (x * x * x)))))


def _sigmoid(x):
    return 0.5 * jnp.tanh(0.5 * x) + 0.5


def _seq_copies(hbm_ref, lead, buf, sem, step, slot, tb, to_hbm):
    copies = []
    for b in range(buf.shape[2]):
        hbm_view = hbm_ref.at[lead + (b, pl.ds(step * tb, tb))]
        vmem_view = buf.at[slot, :, b]
        src, dst = (vmem_view, hbm_view) if to_hbm else (hbm_view, vmem_view)
        copies.append(pltpu.make_async_copy(src, dst, sem.at[slot]))
    return copies


def _load_time_major(hbm_ref, lead, buf, sem, tb):
    j = pl.program_id(0)
    slot = j % 2

    @pl.when(j == 0)
    def _():
        for c in _seq_copies(hbm_ref, lead, buf, sem, 0, 0, tb, False):
            c.start()

    @pl.when(j + 1 < pl.num_programs(0))
    def _():
        for c in _seq_copies(hbm_ref, lead, buf, sem, j + 1, 1 - slot, tb, False):
            c.start()

    for c in _seq_copies(hbm_ref, lead, buf, sem, j, slot, tb, False):
        c.wait()
    return buf[slot].reshape(tb * buf.shape[2], buf.shape[3])


def _store_batch_major(val, hbm_ref, buf, sem, tb):
    j = pl.program_id(0)
    slot = j % 2
    buf[slot] = val.reshape(tb, buf.shape[2], buf.shape[3])
    for c in _seq_copies(hbm_ref, (), buf, sem, j, slot, tb, True):
        c.start()

    @pl.when(j >= 1)
    def _():
        for c in _seq_copies(hbm_ref, (), buf, sem, j - 1, 1 - slot, tb, True):
            c.wait()

    @pl.when(j == pl.num_programs(0) - 1)
    def _():
        for c in _seq_copies(hbm_ref, (), buf, sem, j, slot, tb, True):
            c.wait()


def _ssm_prep_kernel(logdt_ref, lre_ref, lim_ref, bre_ref, bim_ref, cre_ref, cim_ref,
                     abar_ref, b2blk_ref, c2_re_ref, c2_im_ref, fblk_ref):
    dt = jnp.exp(logdt_ref[...])
    lre = lre_ref[...]
    lim = lim_ref[...]
    mag = jnp.exp(lre * dt)
    ang = lim * dt
    ar = mag * jnp.cos(ang)
    ai = mag * jnp.sin(ang)
    abar_ref[0:1, :] = ar
    abar_ref[1:2, :] = ai
    abar_ref[2:3, :] = ar * ar - ai * ai
    abar_ref[3:4, :] = 2.0 * (ar * ai)
    den = lre * lre + lim * lim
    nr = ar - 1.0
    coef_re = (nr * lre + ai * lim) / den
    coef_im = (ai * lre - nr * lim) / den
    bre = bre_ref[...]
    bim = bim_ref[...]
    bbar_re = coef_re * bre - coef_im * bim
    bbar_im = coef_re * bim + coef_im * bre
    abb_re = ar * bbar_re - ai * bbar_im
    abb_im = ar * bbar_im + ai * bbar_re
    row = lax.broadcasted_iota(jnp.int32, (CHUNK_CH, CHUNK_ST), 0)
    lane = lax.broadcasted_iota(jnp.int32, (CHUNK_CH, CHUNK_ST), 1)
    same_group = (row // SSM_GROUP) == (lane // SSM_STATE)
    reps = CHUNK_CH // SSM_GROUP
    for c in range(SSM_CHUNKS):
        cols = slice(c * CHUNK_ST, (c + 1) * CHUNK_ST)
        rows = slice(c * CHUNK_CH, (c + 1) * CHUNK_CH)

        def b_block(v, cols=cols):
            return jnp.where(same_group, jnp.concatenate([v[:, cols]] * reps, axis=0), 0.0)

        b_re = b_block(bbar_re)
        b_im = b_block(bbar_im)
        b2blk_ref[c, 0:CHUNK_CH, 0:CHUNK_ST] = b_block(abb_re).astype(_BF16)
        b2blk_ref[c, 0:CHUNK_CH, CHUNK_ST:2 * CHUNK_ST] = b_block(abb_im).astype(_BF16)
        b2blk_ref[c, CHUNK_CH:2 * CHUNK_CH, 0:CHUNK_ST] = b_re.astype(_BF16)
        b2blk_ref[c, CHUNK_CH:2 * CHUNK_CH, CHUNK_ST:2 * CHUNK_ST] = b_im.astype(_BF16)
        c_re = jnp.where(same_group, cre_ref[rows, :], 0.0)
        c_im = jnp.where(same_group, cim_ref[rows, :], 0.0)
        arc = ar[:, cols]
        aic = ai[:, cols]
        c2_re_ref[c, 0:CHUNK_CH, :] = c_re.astype(_BF16)
        c2_re_ref[c, CHUNK_CH:2 * CHUNK_CH, :] = (c_re * arc - c_im * aic).astype(_BF16)
        c2_im_ref[c, 0:CHUNK_CH, :] = (-c_im).astype(_BF16)
        c2_im_ref[c, CHUNK_CH:2 * CHUNK_CH, :] = (-(c_re * aic + c_im * arc)).astype(_BF16)
        feed = (_dot_nt(b_re, c_re, precision=lax.Precision.HIGHEST)
                - _dot_nt(b_im, c_im, precision=lax.Precision.HIGHEST))
        fblk_ref[c] = feed.astype(_BF16)


def _ssm_prep(log_dt, lam_re, lam_im, b_re, b_im, c_re, c_im):
    logdt = jnp.repeat(log_dt, SSM_STATE, axis=1).reshape(DEPTH, 1, SSM_FLAT)
    lre = lam_re.reshape(DEPTH, 1, SSM_FLAT)
    lim = lam_im.reshape(DEPTH, 1, SSM_FLAT)
    bre = b_re.transpose(0, 3, 1, 2).reshape(DEPTH, SSM_GROUP, SSM_FLAT)
    bim = b_im.transpose(0, 3, 1, 2).reshape(DEPTH, SSM_GROUP, SSM_FLAT)
    reps = CHUNK_ST // SSM_STATE
    cre = jnp.tile(c_re.reshape(DEPTH, SSM_WIDTH, SSM_STATE), (1, 1, reps))
    cim = jnp.tile(c_im.reshape(DEPTH, SSM_WIDTH, SSM_STATE), (1, 1, reps))

    def spec(*shape):
        return pl.BlockSpec((None,) + shape, lambda i: (i,) + (0,) * len(shape))

    return pl.pallas_call(
        _ssm_prep_kernel,
        grid=(DEPTH,),
        in_specs=[spec(1, SSM_FLAT), spec(1, SSM_FLAT), spec(1, SSM_FLAT),
                  spec(SSM_GROUP, SSM_FLAT), spec(SSM_GROUP, SSM_FLAT),
                  spec(SSM_WIDTH, CHUNK_ST), spec(SSM_WIDTH, CHUNK_ST)],
        out_specs=[spec(4, SSM_FLAT),
                   spec(SSM_CHUNKS, 2 * CHUNK_CH, 2 * CHUNK_ST),
                   spec(SSM_CHUNKS, 2 * CHUNK_CH, CHUNK_ST),
                   spec(SSM_CHUNKS, 2 * CHUNK_CH, CHUNK_ST),
                   spec(SSM_CHUNKS, CHUNK_CH, CHUNK_CH)],
        out_shape=[jax.ShapeDtypeStruct((DEPTH, 4, SSM_FLAT), _F32),
                   jax.ShapeDtypeStruct((DEPTH, SSM_CHUNKS, 2 * CHUNK_CH, 2 * CHUNK_ST), _BF16),
                   jax.ShapeDtypeStruct((DEPTH, SSM_CHUNKS, 2 * CHUNK_CH, CHUNK_ST), _BF16),
                   jax.ShapeDtypeStruct((DEPTH, SSM_CHUNKS, 2 * CHUNK_CH, CHUNK_ST), _BF16),
                   jax.ShapeDtypeStruct((DEPTH, SSM_CHUNKS, CHUNK_CH, CHUNK_CH), _BF16)],
        name="ssm_prep",
    )(logdt, lre, lim, bre, bim, cre, cim)


def _mixer_kernel(x_ref, conv0_ref, sre0_ref, sim0_ref,
                  g_ref, w_in_ref, cw_ref, cb_ref, w_out_a_ref,
                  abar_ref, b2blk_ref, c2_re_ref, c2_im_ref, fblk_ref, dskip_ref,
                  glu_a_ref, glu_b_ref, w_o_ref,
                  out_ref, conv_out_ref, sre_out_ref, sim_out_ref,
                  vbuf, s_re, s_im, *x_dma, rb, tb):
    rows = rb * tb
    carry_rows = (CONV_WIDTH - 1) * rb

    @pl.when(pl.program_id(0) == 0)
    def _():
        conv_out_ref[...] = conv0_ref[...]
        sre_out_ref[...] = sre0_ref[...]
        sim_out_ref[...] = sim0_ref[...]

    x = _load_time_major(x_ref, (), *x_dma, tb) if x_dma else x_ref[...]
    xn = _rms(x, g_ref).astype(_BF16)

    def proj(lo, hi):
        return _dot(xn, w_in_ref[:, lo:hi])

    h = proj(0, 512)
    gate_c = proj(1024, 1536)
    vbuf[0:carry_rows, :] = conv_out_ref[...]
    vbuf[carry_rows:carry_rows + rows, :] = gate_c * h
    conv_y = (cb_ref[...]
              + vbuf[0:rows, :] * cw_ref[0:1, :]
              + vbuf[rb:rb + rows, :] * cw_ref[1:2, :]
              + vbuf[2 * rb:2 * rb + rows, :] * cw_ref[2:3, :])
    conv_out_ref[...] = vbuf[rows:rows + carry_rows, :]
    gate_b = proj(512, 1024)
    z_a = _dot((gate_b * conv_y).astype(_BF16), w_out_a_ref[...])
    merged = _sigmoid(proj(2048, 3072)) * z_a

    u = proj(1536, 2048)
    pair = 2 if tb % 2 == 0 else 1
    steps = tb // pair
    srows = steps * rb
    if pair == 2:
        u4 = u.reshape(steps, 2, rb, SSM_WIDTH)
        u_parts = [u4[:, i].reshape(srows, SSM_WIDTH) for i in range(2)]
    else:
        u_parts = [u]
    ub_parts = [v.astype(_BF16) for v in u_parts]
    s_re[0:rb, :] = sre_out_ref[...]
    s_im[0:rb, :] = sim_out_ref[...]
    for c in range(SSM_CHUNKS):
        ch = slice(c * CHUNK_CH, (c + 1) * CHUNK_CH)
        if pair == 2:
            inc = _dot(jnp.concatenate([ub_parts[0][:, ch], ub_parts[1][:, ch]], axis=1), b2blk_ref[c])
        else:
            inc = _dot(ub_parts[0][:, ch], b2blk_ref[c, CHUNK_CH:2 * CHUNK_CH, :])
        s_re[rb:rb + srows, c * CHUNK_ST:(c + 1) * CHUNK_ST] = inc[:, 0:CHUNK_ST]
        s_im[rb:rb + srows, c * CHUNK_ST:(c + 1) * CHUNK_ST] = inc[:, CHUNK_ST:2 * CHUNK_ST]

    a_row = 2 * (pair - 1)
    for c in range(SSM_CHUNKS):
        cols = slice(c * CHUNK_ST, (c + 1) * CHUNK_ST)
        ar = jnp.broadcast_to(abar_ref[a_row:a_row + 1, cols], (rb, CHUNK_ST))
        ai = jnp.broadcast_to(abar_ref[a_row + 1:a_row + 2, cols], (rb, CHUNK_ST))

        def step(k, carry, cols=cols, ar=ar, ai=ai):
            sr, si = carry
            r0 = pl.multiple_of(rb + k * rb, rb)
            nsr = ar * sr - ai * si + s_re[pl.ds(r0, rb), cols]
            nsi = ar * si + ai * sr + s_im[pl.ds(r0, rb), cols]
            s_re[pl.ds(r0, rb), cols] = nsr
            s_im[pl.ds(r0, rb), cols] = nsi
            return nsr, nsi

        sr, si = lax.fori_loop(0, steps, step, (s_re[0:rb, cols], s_im[0:rb, cols]), unroll=True)
        sre_out_ref[:, cols] = sr
        sim_out_ref[:, cols] = si

    y_parts = [[] for _ in range(pair)]
    for c in range(SSM_CHUNKS):
        ch = slice(c * CHUNK_CH, (c + 1) * CHUNK_CH)
        cols = slice(c * CHUNK_ST, (c + 1) * CHUNK_ST)
        if pair == 2:
            y2 = (_dot_nt(s_re[:, cols].astype(_BF16), c2_re_ref[c])
                  + _dot_nt(s_im[:, cols].astype(_BF16), c2_im_ref[c]))
            y_parts[0].append(y2[0:srows, CHUNK_CH:2 * CHUNK_CH] + _dot(ub_parts[0][:, ch], fblk_ref[c]))
            y_parts[1].append(y2[rb:rb + srows, 0:CHUNK_CH])
        else:
            y_parts[0].append(
                _dot_nt(s_re[rb:rb + srows, cols].astype(_BF16), c2_re_ref[c, 0:CHUNK_CH, :])
                + _dot_nt(s_im[rb:rb + srows, cols].astype(_BF16), c2_im_ref[c, 0:CHUNK_CH, :]))
    ssm_parts = [jnp.concatenate(yp, axis=1) + dskip_ref[...] * up for yp, up in zip(y_parts, u_parts)]
    if pair == 2:
        ssm_y = jnp.stack([v.reshape(steps, rb, SSM_WIDTH) for v in ssm_parts],
                          axis=1).reshape(rows, SSM_WIDTH)
    else:
        ssm_y = ssm_parts[0]
    sy = _gelu(ssm_y).astype(_BF16)
    z_b = _dot(sy, glu_a_ref[...]) * _sigmoid(_dot(sy, glu_b_ref[...]))
    merged = (merged + _sigmoid(proj(3072, 4096)) * z_b).astype(_BF16)
    out_ref[...] = x + _dot(merged, w_o_ref[...])


def _ffn_kernel(x_ref, p_ref, f0_ref,
                g2_ref, w_up_ref, fcw_ref, fcb_ref, w_down_ref,
                g3_ref, w_pg_ref, w_ple_ref, gf_ref,
                out_ref, fconv_out_ref,
                fbuf, *dma, rb, tb, layer, final_norm, p_native, out_native):
    rows = rb * tb
    carry_rows = (CONV_WIDTH - 1) * rb
    dma = list(dma)
    p_dma = [dma.pop(0), dma.pop(0)] if p_native else None
    out_dma = [dma.pop(0), dma.pop(0)] if out_native else None

    @pl.when(pl.program_id(0) == 0)
    def _():
        fconv_out_ref[...] = f0_ref[...]

    x = x_ref[...]
    xn = _rms(x, g2_ref).astype(_BF16)
    acc = x
    for lo, hi in FF_CHUNKS:
        wc = hi - lo
        up_a = _dot(xn, w_up_ref[:, lo:hi])
        up_b = _dot(xn, w_up_ref[:, D_FF + lo:D_FF + hi])
        fbuf[0:carry_rows, 0:wc] = fconv_out_ref[:, lo:hi]
        fbuf[carry_rows:carry_rows + rows, 0:wc] = up_a
        conv_a = (fcb_ref[:, lo:hi]
                  + fbuf[0:rows, 0:wc] * fcw_ref[0:1, lo:hi]
                  + fbuf[rb:rb + rows, 0:wc] * fcw_ref[1:2, lo:hi]
                  + fbuf[2 * rb:2 * rb + rows, 0:wc] * fcw_ref[2:3, lo:hi])
        fconv_out_ref[:, lo:hi] = fbuf[rows:rows + carry_rows, 0:wc]
        hid = (_gelu(conv_a) * up_b).astype(_BF16)
        acc = acc + _dot(hid, w_down_ref[lo:hi, :])
    x2 = acc
    xn3 = _rms(x2, g3_ref).astype(_BF16)
    gate = _sigmoid(_dot(xn3, w_pg_ref[...]))
    p = _load_time_major(p_ref, (layer,), *p_dma, tb) if p_native else p_ref[...]
    pe = _dot(p.astype(_BF16), w_ple_ref[...])
    x3 = x2 + gate * pe
    if final_norm:
        x3 = _rms(x3, gf_ref)
    if out_native:
        _store_batch_major(x3, out_ref, *out_dma, tb)
    else:
        out_ref[...] = x3


def _const_spec(shape, layer=None):
    if layer is None:
        return pl.BlockSpec(shape, lambda j: (0,) * len(shape), pipeline_mode=pl.Buffered(1))
    return pl.BlockSpec((None,) + shape, lambda j: (layer,) + (0,) * len(shape),
                        pipeline_mode=pl.Buffered(1))


def _native_scratch(tb, rb, width):
    return [pltpu.VMEM((2, tb, rb, width), _F32), pltpu.SemaphoreType.DMA((2,))]


def _mixer_call(x, conv0, sre0, sim0, wts, layer, rb, tb, x_native):
    n = x.shape[0] * x.shape[1] if x_native else x.shape[0]
    rows = rb * tb
    carry_rows = (CONV_WIDTH - 1) * rb
    row_spec = pl.BlockSpec((rows, D_MODEL), lambda j: (j, 0))
    in_specs = [
        pl.BlockSpec(memory_space=pl.ANY) if x_native else row_spec,
        _const_spec((carry_rows, CONV_A_WIDTH), layer),
        _const_spec((rb, SSM_FLAT), layer),
        _const_spec((rb, SSM_FLAT), layer),
        _const_spec((1, D_MODEL), layer),
        _const_spec((D_MODEL, IN_PROJ_WIDTH), layer),
        _const_spec((CONV_WIDTH, CONV_A_WIDTH), layer),
        _const_spec((1, CONV_A_WIDTH), layer),
        _const_spec((CONV_A_WIDTH, D_MODEL), layer),
        _const_spec((4, SSM_FLAT), layer),
        _const_spec((SSM_CHUNKS, 2 * CHUNK_CH, 2 * CHUNK_ST), layer),
        _const_spec((SSM_CHUNKS, 2 * CHUNK_CH, CHUNK_ST), layer),
        _const_spec((SSM_CHUNKS, 2 * CHUNK_CH, CHUNK_ST), layer),
        _const_spec((SSM_CHUNKS, CHUNK_CH, CHUNK_CH), layer),
        _const_spec((1, SSM_WIDTH), layer),
        _const_spec((SSM_WIDTH, D_MODEL), layer),
        _const_spec((SSM_WIDTH, D_MODEL), layer),
        _const_spec((D_MODEL, D_MODEL), layer),
    ]
    out_specs = [
        row_spec,
        pl.BlockSpec((carry_rows, CONV_A_WIDTH), lambda j: (0, 0)),
        pl.BlockSpec((rb, SSM_FLAT), lambda j: (0, 0)),
        pl.BlockSpec((rb, SSM_FLAT), lambda j: (0, 0)),
    ]
    out_shape = [
        jax.ShapeDtypeStruct((n, D_MODEL), _F32),
        jax.ShapeDtypeStruct((carry_rows, CONV_A_WIDTH), _F32),
        jax.ShapeDtypeStruct((rb, SSM_FLAT), _F32),
        jax.ShapeDtypeStruct((rb, SSM_FLAT), _F32),
    ]
    s_rows = rows // 2 if tb % 2 == 0 else rows
    scratch = [
        pltpu.VMEM((carry_rows + rows, CONV_A_WIDTH), _F32),
        pltpu.VMEM((rb + s_rows, SSM_FLAT), _F32),
        pltpu.VMEM((rb + s_rows, SSM_FLAT), _F32),
    ]
    if x_native:
        scratch += _native_scratch(tb, rb, D_MODEL)
    return pl.pallas_call(
        functools.partial(_mixer_kernel, rb=rb, tb=tb),
        grid=(n // rows,),
        in_specs=in_specs,
        out_specs=out_specs,
        out_shape=out_shape,
        scratch_shapes=scratch,
        compiler_params=pltpu.CompilerParams(
            dimension_semantics=("arbitrary",), vmem_limit_bytes=VMEM_LIMIT_BYTES),
        name=f"mixer_l{layer}_rb{rb}",
    )(x, conv0, sre0, sim0,
      wts["norm_mix"], wts["w_in"], wts["conv_a_w"], wts["conv_a_b"], wts["w_out_a"],
      wts["abar"], wts["b2blk"], wts["c2_re"], wts["c2_im"], wts["fblk"], wts["d_skip"],
      wts["w_glu_a"], wts["w_glu_b"], wts["w_o"])


def _ffn_call(x, p, f0, wts, layer, rb, tb, final_norm, p_native, out_native):
    n = x.shape[0]
    rows = rb * tb
    carry_rows = (CONV_WIDTH - 1) * rb
    row_spec = pl.BlockSpec((rows, D_MODEL), lambda j: (j, 0))
    in_specs = [
        row_spec,
        (pl.BlockSpec(memory_space=pl.ANY) if p_native
         else pl.BlockSpec((None, rows, PLE_DIM), lambda j: (layer, j, 0))),
        _const_spec((carry_rows, D_FF), layer),
        _const_spec((1, D_MODEL), layer),
        _const_spec((D_MODEL, 2 * D_FF), layer),
        _const_spec((CONV_WIDTH, D_FF), layer),
        _const_spec((1, D_FF), layer),
        _const_spec((D_FF, D_MODEL), layer),
        _const_spec((1, D_MODEL), layer),
        _const_spec((D_MODEL, D_MODEL), layer),
        _const_spec((PLE_DIM, D_MODEL), layer),
        _const_spec((1, D_MODEL)),
    ]
    out_specs = [pl.BlockSpec(memory_space=pl.ANY) if out_native else row_spec,
                 pl.BlockSpec((carry_rows, D_FF), lambda j: (0, 0))]
    out_shape = [jax.ShapeDtypeStruct((rb, n // rb, D_MODEL) if out_native else (n, D_MODEL), _F32),
                 jax.ShapeDtypeStruct((carry_rows, D_FF), _F32)]
    scratch = [pltpu.VMEM((carry_rows + rows, FF_CHUNK_MAX), _F32)]
    if p_native:
        scratch += _native_scratch(tb, rb, PLE_DIM)
    if out_native:
        scratch += _native_scratch(tb, rb, D_MODEL)
    return pl.pallas_call(
        functools.partial(_ffn_kernel, rb=rb, tb=tb, layer=layer, final_norm=final_norm,
                          p_native=p_native, out_native=out_native),
        grid=(n // rows,),
        in_specs=in_specs,
        out_specs=out_specs,
        out_shape=out_shape,
        scratch_shapes=scratch,
        compiler_params=pltpu.CompilerParams(
            dimension_semantics=("arbitrary",), vmem_limit_bytes=VMEM_LIMIT_BYTES),
        name=f"ffn_l{layer}_rb{rb}",
    )(x, p, f0,
      wts["norm_ffn"], wts["w_up"], wts["ffn_conv_w"], wts["ffn_conv_b"], wts["w_down"],
      wts["norm_ple"], wts["w_ple_gate"], wts["w_ple"], wts["norm_final"])


def _run_trunk(x, p, conv0, sre0, sim0, f0, wts, rb, tb_mixer, tb_ffn, native):
    convs, sres, sims, ffns = [], [], [], []
    for layer in range(DEPTH):
        last = layer == DEPTH - 1
        x, conv_n, sre_n, sim_n = _mixer_call(x, conv0, sre0, sim0, wts, layer, rb, tb_mixer,
                                              x_native=native and layer == 0)
        x, ffn_n = _ffn_call(x, p, f0, wts, layer, rb, tb_ffn, final_norm=last,
                             p_native=native, out_native=native and last)
        convs.append(conv_n)
        sres.append(sre_n)
        sims.append(sim_n)
        ffns.append(ffn_n)
    return x, jnp.stack(convs), jnp.stack(sres), jnp.stack(sims), jnp.stack(ffns)


def _to_time_major_state(s):
    d, b, k, c = s.shape
    return s.transpose(0, 2, 1, 3).reshape(d, k * b, c)


def _from_time_major_state(s, b):
    d, kb, c = s.shape
    return s.reshape(d, kb // b, b, c).transpose(0, 2, 1, 3)


def kernel(x_prompt, x_sample, p_prompt, p_sample, state_conv_a, state_ssm_re, state_ssm_im, state_ffn_conv, norm_mix, w_in, conv_a_w, conv_a_b, w_out_a, log_dt, lam_re, lam_im, b_re, b_im, c_re, c_im, d_skip, w_glu_a, w_glu_b, w_o, norm_ffn, w_up, ffn_conv_w, ffn_conv_b, w_down, norm_ple, w_ple_gate, w_ple, norm_final):
    abar, b2blk, c2_re, c2_im, fblk = _ssm_prep(log_dt, lam_re, lam_im, b_re, b_im, c_re, c_im)
    wts = {
        "norm_mix": norm_mix.reshape(DEPTH, 1, D_MODEL),
        "w_in": w_in.astype(_BF16),
        "conv_a_w": conv_a_w,
        "conv_a_b": conv_a_b.reshape(DEPTH, 1, CONV_A_WIDTH),
        "w_out_a": w_out_a.astype(_BF16),
        "abar": abar, "b2blk": b2blk, "c2_re": c2_re, "c2_im": c2_im, "fblk": fblk,
        "d_skip": d_skip.reshape(DEPTH, 1, SSM_WIDTH),
        "w_glu_a": w_glu_a.astype(_BF16),
        "w_glu_b": w_glu_b.astype(_BF16),
        "w_o": w_o.astype(_BF16),
        "norm_ffn": norm_ffn.reshape(DEPTH, 1, D_MODEL),
        "w_up": w_up.astype(_BF16),
        "ffn_conv_w": ffn_conv_w,
        "ffn_conv_b": ffn_conv_b.reshape(DEPTH, 1, D_FF),
        "w_down": w_down.astype(_BF16),
        "norm_ple": norm_ple.reshape(DEPTH, 1, D_MODEL),
        "w_ple_gate": w_ple_gate.astype(_BF16),
        "w_ple": w_ple.astype(_BF16),
        "norm_final": norm_final.reshape(1, D_MODEL),
    }

    bp = x_prompt.shape[0]
    zc = jnp.zeros((DEPTH, (CONV_WIDTH - 1) * bp, CONV_A_WIDTH), _F32)
    zs = jnp.zeros((DEPTH, bp, SSM_FLAT), _F32)
    zf = jnp.zeros((DEPTH, (CONV_WIDTH - 1) * bp, D_FF), _F32)
    y_prompt, conv_p, sre_p, sim_p, ffn_p = _run_trunk(
        x_prompt, p_prompt, zc, zs, zs, zf, wts, bp, MIXER_TIME_BLOCK, FFN_TIME_BLOCK, native=True)

    bs, sseq, _ = x_sample.shape
    xs = x_sample.transpose(1, 0, 2).reshape(sseq * bs, D_MODEL)
    ps = p_sample.transpose(0, 2, 1, 3).reshape(DEPTH, sseq * bs, PLE_DIM)
    ys, conv_s, sre_s, sim_s, ffn_s = _run_trunk(
        xs, ps, _to_time_major_state(state_conv_a),
        state_ssm_re.reshape(DEPTH, bs, SSM_FLAT), state_ssm_im.reshape(DEPTH, bs, SSM_FLAT),
        _to_time_major_state(state_ffn_conv), wts, bs, sseq, sseq, native=False)
    y_sample = ys.reshape(sseq, bs, D_MODEL).transpose(1, 0, 2)

    grp = (SSM_GROUPS, SSM_STATE)
    return (y_prompt, y_sample,
            _from_time_major_state(conv_p, bp), sre_p.reshape(DEPTH, bp, *grp),
            sim_p.reshape(DEPTH, bp, *grp), _from_time_major_state(ffn_p, bp),
            _from_time_major_state(conv_s, bs), sre_s.reshape(DEPTH, bs, *grp),
            sim_s.reshape(DEPTH, bs, *grp), _from_time_major_state(ffn_s, bs))
```

```python
import functools
import math

import jax
import jax.numpy as jnp
from jax import lax
from jax.experimental import pallas as pl
from jax.experimental.pallas import tpu as pltpu

D_MODEL = 1024
DEPTH = 2
CONV_WIDTH = 3
CONV_A_WIDTH = 512
SSM_WIDTH = 512
SSM_GROUP = 16
SSM_GROUPS = 32
SSM_STATE = 64
SSM_FLAT = SSM_GROUPS * SSM_STATE
D_FF = 2816
PLE_DIM = 256
NORM_EPS = 1e-6
IN_PROJ_WIDTH = 4096

SSM_CHUNKS = 4
CHUNK_CH = SSM_WIDTH // SSM_CHUNKS
CHUNK_ST = SSM_FLAT // SSM_CHUNKS

FF_CHUNKS = ((0, 1024), (1024, 2048), (2048, 2816))
FF_CHUNK_MAX = 1024

MIXER_TIME_BLOCK = 128
FFN_TIME_BLOCK = 128
VMEM_LIMIT_BYTES = 56 * 1024 * 1024

_BF16 = jnp.bfloat16
_F32 = jnp.float32


def _dot(a, b):
    return jnp.dot(a, b, preferred_element_type=_F32)


def _dot_nt(a, b):
    return lax.dot_general(a, b, (((1,), (1,)), ((), ())), preferred_element_type=_F32)


def _dot_nt_split(a, b):
    a_hi = a.astype(_BF16)
    b_hi = b.astype(_BF16)
    a_lo = (a - a_hi.astype(_F32)).astype(_BF16)
    b_lo = (b - b_hi.astype(_F32)).astype(_BF16)
    return _dot_nt(a_hi, b_hi) + _dot_nt(a_hi, b_lo) + _dot_nt(a_lo, b_hi)


def _rms(x, g):
    ms = jnp.mean(x * x, axis=-1, keepdims=True)
    return x * lax.rsqrt(ms + NORM_EPS) * g


def _gelu(x):
    c = math.sqrt(2.0 / math.pi)
    t = jnp.tanh(x * (c + (0.044715 * c) * (x * x)))
    return x * (0.5 + 0.5 * t)


def _sigmoid(x):
    return 0.5 * jnp.tanh(0.5 * x) + 0.5


def _seq_copies(hbm_ref, lead, buf, sem, step, slot, tb, to_hbm):
    copies = []
    for b in range(buf.shape[2]):
        hbm_view = hbm_ref.at[lead + (b, pl.ds(step * tb, tb))]
        vmem_view = buf.at[slot, :, b]
        src, dst = (vmem_view, hbm_view) if to_hbm else (hbm_view, vmem_view)
        copies.append(pltpu.make_async_copy(src, dst, sem.at[slot]))
    return copies


def _load_time_major(hbm_ref, lead, buf, sem, tb):
    j = pl.program_id(0)
    slot = j % 2

    @pl.when(j == 0)
    def _():
        for c in _seq_copies(hbm_ref, lead, buf, sem, 0, 0, tb, False):
            c.start()

    @pl.when(j + 1 < pl.num_programs(0))
    def _():
        for c in _seq_copies(hbm_ref, lead, buf, sem, j + 1, 1 - slot, tb, False):
            c.start()

    for c in _seq_copies(hbm_ref, lead, buf, sem, j, slot, tb, False):
        c.wait()
    return buf[slot].reshape(tb * buf.shape[2], buf.shape[3])


def _store_batch_major(val, hbm_ref, buf, sem, tb):
    j = pl.program_id(0)
    slot = j % 2
    buf[slot] = val.reshape(tb, buf.shape[2], buf.shape[3])
    for c in _seq_copies(hbm_ref, (), buf, sem, j, slot, tb, True):
        c.start()

    @pl.when(j >= 1)
    def _():
        for c in _seq_copies(hbm_ref, (), buf, sem, j - 1, 1 - slot, tb, True):
            c.wait()

    @pl.when(j == pl.num_programs(0) - 1)
    def _():
        for c in _seq_copies(hbm_ref, (), buf, sem, j, slot, tb, True):
            c.wait()


def _ssm_prep_kernel(logdt_ref, lre_ref, lim_ref, bre_ref, bim_ref, cre_ref, cim_ref,
                     abar_ref, b2blk_ref, c2_re_ref, c2_im_ref, fblk_ref):
    dt = jnp.exp(logdt_ref[...])
    lre = lre_ref[...]
    lim = lim_ref[...]
    mag = jnp.exp(lre * dt)
    ang = lim * dt
    ar = mag * jnp.cos(ang)
    ai = mag * jnp.sin(ang)
    abar_ref[0:1, :] = ar
    abar_ref[1:2, :] = ai
    abar_ref[2:3, :] = ar * ar - ai * ai
    abar_ref[3:4, :] = 2.0 * (ar * ai)
    den = lre * lre + lim * lim
    nr = ar - 1.0
    coef_re = (nr * lre + ai * lim) / den
    coef_im = (ai * lre - nr * lim) / den
    bre = bre_ref[...]
    bim = bim_ref[...]
    bbar_re = coef_re * bre - coef_im * bim
    bbar_im = coef_re * bim + coef_im * bre
    abb_re = ar * bbar_re - ai * bbar_im
    abb_im = ar * bbar_im + ai * bbar_re
    row = lax.broadcasted_iota(jnp.int32, (CHUNK_CH, CHUNK_ST), 0)
    lane = lax.broadcasted_iota(jnp.int32, (CHUNK_CH, CHUNK_ST), 1)
    same_group = (row // SSM_GROUP) == (lane // SSM_STATE)
    reps = CHUNK_CH // SSM_GROUP
    for c in range(SSM_CHUNKS):
        cols = slice(c * CHUNK_ST, (c + 1) * CHUNK_ST)
        rows = slice(c * CHUNK_CH, (c + 1) * CHUNK_CH)

        def b_block(v, cols=cols):
            return jnp.where(same_group, jnp.concatenate([v[:, cols]] * reps, axis=0), 0.0)

        b_re = b_block(bbar_re)
        b_im = b_block(bbar_im)
        b2blk_ref[c, 0:CHUNK_CH, 0:CHUNK_ST] = b_block(abb_re).astype(_BF16)
        b2blk_ref[c, 0:CHUNK_CH, CHUNK_ST:2 * CHUNK_ST] = b_block(abb_im).astype(_BF16)
        b2blk_ref[c, CHUNK_CH:2 * CHUNK_CH, 0:CHUNK_ST] = b_re.astype(_BF16)
        b2blk_ref[c, CHUNK_CH:2 * CHUNK_CH, CHUNK_ST:2 * CHUNK_ST] = b_im.astype(_BF16)
        c_re = jnp.where(same_group, cre_ref[rows, :], 0.0)
        c_im = jnp.where(same_group, cim_ref[rows, :], 0.0)
        arc = ar[:, cols]
        aic = ai[:, cols]
        c2_re_ref[c, 0:CHUNK_CH, :] = c_re.astype(_BF16)
        c2_re_ref[c, CHUNK_CH:2 * CHUNK_CH, :] = (c_re * arc - c_im * aic).astype(_BF16)
        c2_im_ref[c, 0:CHUNK_CH, :] = (-c_im).astype(_BF16)
        c2_im_ref[c, CHUNK_CH:2 * CHUNK_CH, :] = (-(c_re * aic + c_im * arc)).astype(_BF16)
        feed = _dot_nt_split(b_re, c_re) - _dot_nt_split(b_im, c_im)
        fblk_ref[c] = feed.astype(_BF16)


def _ssm_prep(log_dt, lam_re, lam_im, b_re, b_im, c_re, c_im):
    logdt = jnp.repeat(log_dt, SSM_STATE, axis=1).reshape(DEPTH, 1, SSM_FLAT)
    lre = lam_re.reshape(DEPTH, 1, SSM_FLAT)
    lim = lam_im.reshape(DEPTH, 1, SSM_FLAT)
    bre = b_re.transpose(0, 3, 1, 2).reshape(DEPTH, SSM_GROUP, SSM_FLAT)
    bim = b_im.transpose(0, 3, 1, 2).reshape(DEPTH, SSM_GROUP, SSM_FLAT)
    reps = CHUNK_ST // SSM_STATE
    cre = jnp.tile(c_re.reshape(DEPTH, SSM_WIDTH, SSM_STATE), (1, 1, reps))
    cim = jnp.tile(c_im.reshape(DEPTH, SSM_WIDTH, SSM_STATE), (1, 1, reps))

    def spec(*shape):
        return pl.BlockSpec((None,) + shape, lambda i: (i,) + (0,) * len(shape))

    return pl.pallas_call(
        _ssm_prep_kernel,
        grid=(DEPTH,),
        in_specs=[spec(1, SSM_FLAT), spec(1, SSM_FLAT), spec(1, SSM_FLAT),
                  spec(SSM_GROUP, SSM_FLAT), spec(SSM_GROUP, SSM_FLAT),
                  spec(SSM_WIDTH, CHUNK_ST), spec(SSM_WIDTH, CHUNK_ST)],
        out_specs=[spec(4, SSM_FLAT),
                   spec(SSM_CHUNKS, 2 * CHUNK_CH, 2 * CHUNK_ST),
                   spec(SSM_CHUNKS, 2 * CHUNK_CH, CHUNK_ST),
                   spec(SSM_CHUNKS, 2 * CHUNK_CH, CHUNK_ST),
                   spec(SSM_CHUNKS, CHUNK_CH, CHUNK_CH)],
        out_shape=[jax.ShapeDtypeStruct((DEPTH, 4, SSM_FLAT), _F32),
                   jax.ShapeDtypeStruct((DEPTH, SSM_CHUNKS, 2 * CHUNK_CH, 2 * CHUNK_ST), _BF16),
                   jax.ShapeDtypeStruct((DEPTH, SSM_CHUNKS, 2 * CHUNK_CH, CHUNK_ST), _BF16),
                   jax.ShapeDtypeStruct((DEPTH, SSM_CHUNKS, 2 * CHUNK_CH, CHUNK_ST), _BF16),
                   jax.ShapeDtypeStruct((DEPTH, SSM_CHUNKS, CHUNK_CH, CHUNK_CH), _BF16)],
        name="ssm_prep",
    )(logdt, lre, lim, bre, bim, cre, cim)


def _mixer_kernel(x_ref, conv0_ref, sre0_ref, sim0_ref, *refs, rb, tb, layer):
    refs = list(refs)
    prev = [refs.pop(0) for _ in range(3)] if layer > 0 else []
    (g_ref, w_in_ref, cw_ref, cb_ref, w_out_a_ref,
     abar_ref, b2blk_ref, c2_re_ref, c2_im_ref, fblk_ref, dskip_ref,
     glu_a_ref, glu_b_ref, w_o_ref,
     out_ref, conv_stack_ref, sre_stack_ref, sim_stack_ref,
     vbuf, s_re, s_im, *x_dma) = refs
    rows = rb * tb
    carry_rows = (CONV_WIDTH - 1) * rb
    conv_out_ref = conv_stack_ref.at[layer]
    sre_out_ref = sre_stack_ref.at[layer]
    sim_out_ref = sim_stack_ref.at[layer]

    @pl.when(pl.program_id(0) == 0)
    def _():
        for stack_ref, prev_ref in zip((conv_stack_ref, sre_stack_ref, sim_stack_ref), prev):
            stack_ref[0:layer] = prev_ref[...]
        conv_out_ref[...] = conv0_ref[...]
        sre_out_ref[...] = sre0_ref[...]
        sim_out_ref[...] = sim0_ref[...]

    x = _load_time_major(x_ref, (), *x_dma, tb) if x_dma else x_ref[...]
    xn = _rms(x, g_ref[layer:layer + 1, :]).astype(_BF16)

    def proj(lo, hi):
        return _dot(xn, w_in_ref[:, lo:hi])

    u = proj(1536, 2048)
    pair = 2 if tb % 2 == 0 else 1
    steps = tb // pair
    srows = steps * rb
    if pair == 2:
        u4 = u.reshape(steps, 2, rb, SSM_WIDTH)
        u_parts = [u4[:, i].reshape(srows, SSM_WIDTH) for i in range(2)]
    else:
        u_parts = [u]
    ub_parts = [v.astype(_BF16) for v in u_parts]
    s_re[0:rb, :] = sre_out_ref[...]
    s_im[0:rb, :] = sim_out_ref[...]
    for c in range(SSM_CHUNKS):
        ch = slice(c * CHUNK_CH, (c + 1) * CHUNK_CH)
        if pair == 2:
            inc = _dot(jnp.concatenate([ub_parts[0][:, ch], ub_parts[1][:, ch]], axis=1), b2blk_ref[c])
        else:
            inc = _dot(ub_parts[0][:, ch], b2blk_ref[c, CHUNK_CH:2 * CHUNK_CH, :])
        s_re[rb:rb + srows, c * CHUNK_ST:(c + 1) * CHUNK_ST] = inc[:, 0:CHUNK_ST]
        s_im[rb:rb + srows, c * CHUNK_ST:(c + 1) * CHUNK_ST] = inc[:, CHUNK_ST:2 * CHUNK_ST]

    a_row = 2 * (pair - 1)
    for c in range(SSM_CHUNKS):
        cols = slice(c * CHUNK_ST, (c + 1) * CHUNK_ST)
        ar = jnp.broadcast_to(abar_ref[a_row:a_row + 1, cols], (rb, CHUNK_ST))
        ai = jnp.broadcast_to(abar_ref[a_row + 1:a_row + 2, cols], (rb, CHUNK_ST))

        def step(k, carry, cols=cols, ar=ar, ai=ai):
            sr, si = carry
            r0 = pl.multiple_of(rb + k * rb, rb)
            nsr = ar * sr - ai * si + s_re[pl.ds(r0, rb), cols]
            nsi = ar * si + ai * sr + s_im[pl.ds(r0, rb), cols]
            s_re[pl.ds(r0, rb), cols] = nsr
            s_im[pl.ds(r0, rb), cols] = nsi
            return nsr, nsi

        sr, si = lax.fori_loop(0, steps, step, (s_re[0:rb, cols], s_im[0:rb, cols]), unroll=True)
        sre_out_ref[:, cols] = sr
        sim_out_ref[:, cols] = si

    h = proj(0, 512)
    gate_c = proj(1024, 1536)
    vbuf[0:rb, :] = conv_out_ref[:, 0:CONV_A_WIDTH]
    vbuf[rb:carry_rows, :] = conv_out_ref[:, CONV_A_WIDTH:2 * CONV_A_WIDTH]
    vbuf[carry_rows:carry_rows + rows, :] = gate_c * h
    conv_y = (cb_ref[layer:layer + 1, :]
              + vbuf[0:rows, :] * cw_ref[0:1, :]
              + vbuf[rb:rb + rows, :] * cw_ref[1:2, :]
              + vbuf[2 * rb:2 * rb + rows, :] * cw_ref[2:3, :])
    conv_out_ref[:, 0:CONV_A_WIDTH] = vbuf[rows:rows + rb, :]
    conv_out_ref[:, CONV_A_WIDTH:2 * CONV_A_WIDTH] = vbuf[rows + rb:rows + carry_rows, :]
    gate_b = proj(512, 1024)
    z_a = _dot((gate_b * conv_y).astype(_BF16), w_out_a_ref[...])
    merged = _sigmoid(proj(2048, 3072)) * z_a

    y_parts = [[] for _ in range(pair)]
    for c in range(SSM_CHUNKS):
        ch = slice(c * CHUNK_CH, (c + 1) * CHUNK_CH)
        cols = slice(c * CHUNK_ST, (c + 1) * CHUNK_ST)
        if pair == 2:
            y2 = (_dot_nt(s_re[:, cols].astype(_BF16), c2_re_ref[c])
                  + _dot_nt(s_im[:, cols].astype(_BF16), c2_im_ref[c]))
            y_parts[0].append(y2[0:srows, CHUNK_CH:2 * CHUNK_CH] + _dot(ub_parts[0][:, ch], fblk_ref[c]))
            y_parts[1].append(y2[rb:rb + srows, 0:CHUNK_CH])
        else:
            y_parts[0].append(
                _dot_nt(s_re[rb:rb + srows, cols].astype(_BF16), c2_re_ref[c, 0:CHUNK_CH, :])
                + _dot_nt(s_im[rb:rb + srows, cols].astype(_BF16), c2_im_ref[c, 0:CHUNK_CH, :]))
    d_skip = dskip_ref[layer:layer + 1, :]
    ssm_parts = [jnp.concatenate(yp, axis=1) + d_skip * up for yp, up in zip(y_parts, u_parts)]
    if pair == 2:
        ssm_y = jnp.stack([v.reshape(steps, rb, SSM_WIDTH) for v in ssm_parts],
                          axis=1).reshape(rows, SSM_WIDTH)
    else:
        ssm_y = ssm_parts[0]
    sy = _gelu(ssm_y).astype(_BF16)
    z_b = _dot(sy, glu_a_ref[...]) * _sigmoid(_dot(sy, glu_b_ref[...]))
    merged = (merged + _sigmoid(proj(3072, 4096)) * z_b).astype(_BF16)
    out_ref[...] = x + _dot(merged, w_o_ref[...])


def _ffn_kernel(x_ref, p_ref, f0_ref, *refs, rb, tb, layer, final_norm, p_native, out_native):
    refs = list(refs)
    prev = [refs.pop(0)] if layer > 0 else []
    (g2_ref, w_up_ref, fcw_ref, fcb_ref, w_down_ref,
     g3_ref, w_pg_ref, w_ple_ref, gf_ref,
     out_ref, fconv_stack_ref,
     fbuf, *dma) = refs
    rows = rb * tb
    carry_rows = (CONV_WIDTH - 1) * rb
    p_dma = [dma.pop(0), dma.pop(0)] if p_native else None
    out_dma = [dma.pop(0), dma.pop(0)] if out_native else None
    fconv_out_ref = fconv_stack_ref.at[layer]

    @pl.when(pl.program_id(0) == 0)
    def _():
        for prev_ref in prev:
            fconv_stack_ref[0:layer] = prev_ref[...]
        fconv_out_ref[...] = f0_ref[...]

    x = x_ref[...]
    xn = _rms(x, g2_ref[layer:layer + 1, :]).astype(_BF16)
    acc = x
    for lo, hi in FF_CHUNKS:
        wc = hi - lo
        up_a = _dot(xn, w_up_ref[:, lo:hi])
        up_b = _dot(xn, w_up_ref[:, D_FF + lo:D_FF + hi])
        fbuf[0:rb, 0:wc] = fconv_out_ref[:, lo:hi]
        fbuf[rb:carry_rows, 0:wc] = fconv_out_ref[:, D_FF + lo:D_FF + hi]
        fbuf[carry_rows:carry_rows + rows, 0:wc] = up_a
        conv_a = (fcb_ref[layer:layer + 1, lo:hi]
                  + fbuf[0:rows, 0:wc] * fcw_ref[0:1, lo:hi]
                  + fbuf[rb:rb + rows, 0:wc] * fcw_ref[1:2, lo:hi]
                  + fbuf[2 * rb:2 * rb + rows, 0:wc] * fcw_ref[2:3, lo:hi])
        fconv_out_ref[:, lo:hi] = fbuf[rows:rows + rb, 0:wc]
        fconv_out_ref[:, D_FF + lo:D_FF + hi] = fbuf[rows + rb:rows + carry_rows, 0:wc]
        hid = (_gelu(conv_a) * up_b).astype(_BF16)
        acc = acc + _dot(hid, w_down_ref[lo:hi, :])
    x2 = acc
    xn3 = _rms(x2, g3_ref[layer:layer + 1, :]).astype(_BF16)
    gate = _sigmoid(_dot(xn3, w_pg_ref[...]))
    p = _load_time_major(p_ref, (layer,), *p_dma, tb) if p_native else p_ref[...]
    pe = _dot(p.astype(_BF16), w_ple_ref[...])
    x3 = x2 + gate * pe
    if final_norm:
        x3 = _rms(x3, gf_ref[...])
    if out_native:
        _store_batch_major(x3, out_ref, *out_dma, tb)
    else:
        out_ref[...] = x3


def _const_spec(shape, layer=None):
    if layer is None:
        return pl.BlockSpec(shape, lambda j: (0,) * len(shape), pipeline_mode=pl.Buffered(1))
    return pl.BlockSpec((None,) + shape, lambda j: (layer,) + (0,) * len(shape),
                        pipeline_mode=pl.Buffered(1))


def _native_scratch(tb, rb, width):
    return [pltpu.VMEM((2, tb, rb, width), _F32), pltpu.SemaphoreType.DMA((2,))]


def _stack_spec(n_layers, rb, width):
    return pl.BlockSpec((n_layers, rb, width), lambda j: (0, 0, 0))


def _mixer_call(x, state0, prev, wts, layer, state_layer, rb, tb, x_native):
    n = x.shape[0] * x.shape[1] if x_native else x.shape[0]
    rows = rb * tb
    carry_rows = (CONV_WIDTH - 1) * rb
    widths = ((CONV_WIDTH - 1) * CONV_A_WIDTH, SSM_FLAT, SSM_FLAT)
    row_spec = pl.BlockSpec((rows, D_MODEL), lambda j: (j, 0))
    in_specs = [pl.BlockSpec(memory_space=pl.ANY) if x_native else row_spec]
    in_specs += [_const_spec((rb, w), state_layer) for w in widths]
    in_specs += [_const_spec((layer, rb, w)) for w in widths[:len(prev)]]
    in_specs += [
        _const_spec((DEPTH, D_MODEL)),
        _const_spec((D_MODEL, IN_PROJ_WIDTH), layer),
        _const_spec((CONV_WIDTH, CONV_A_WIDTH), layer),
        _const_spec((DEPTH, CONV_A_WIDTH)),
        _const_spec((CONV_A_WIDTH, D_MODEL), layer),
        _const_spec((4, SSM_FLAT), layer),
        _const_spec((SSM_CHUNKS, 2 * CHUNK_CH, 2 * CHUNK_ST), layer),
        _const_spec((SSM_CHUNKS, 2 * CHUNK_CH, CHUNK_ST), layer),
        _const_spec((SSM_CHUNKS, 2 * CHUNK_CH, CHUNK_ST), layer),
        _const_spec((SSM_CHUNKS, CHUNK_CH, CHUNK_CH), layer),
        _const_spec((DEPTH, SSM_WIDTH)),
        _const_spec((SSM_WIDTH, D_MODEL), layer),
        _const_spec((SSM_WIDTH, D_MODEL), layer),
        _const_spec((D_MODEL, D_MODEL), layer),
    ]
    out_specs = [row_spec] + [_stack_spec(layer + 1, rb, w) for w in widths]
    out_shape = ([jax.ShapeDtypeStruct((n, D_MODEL), _F32)]
                 + [jax.ShapeDtypeStruct((layer + 1, rb, w), _F32) for w in widths])
    s_rows = rows // 2 if tb % 2 == 0 else rows
    scratch = [
        pltpu.VMEM((carry_rows + rows, CONV_A_WIDTH), _F32),
        pltpu.VMEM((rb + s_rows, SSM_FLAT), _F32),
        pltpu.VMEM((rb + s_rows, SSM_FLAT), _F32),
    ]
    if x_native:
        scratch += _native_scratch(tb, rb, D_MODEL)
    return pl.pallas_call(
        functools.partial(_mixer_kernel, rb=rb, tb=tb, layer=layer),
        grid=(n // rows,),
        in_specs=in_specs,
        out_specs=out_specs,
        out_shape=out_shape,
        scratch_shapes=scratch,
        compiler_params=pltpu.CompilerParams(
            dimension_semantics=("arbitrary",), vmem_limit_bytes=VMEM_LIMIT_BYTES),
        name=f"mixer_l{layer}_rb{rb}",
    )(x, *state0, *prev,
      wts["norm_mix"], wts["w_in"], wts["conv_a_w"], wts["conv_a_b"], wts["w_out_a"],
      wts["abar"], wts["b2blk"], wts["c2_re"], wts["c2_im"], wts["fblk"], wts["d_skip"],
      wts["w_glu_a"], wts["w_glu_b"], wts["w_o"])


def _ffn_call(x, p, f0, prev, wts, layer, state_layer, rb, tb, final_norm, p_native, out_native):
    n = x.shape[0]
    rows = rb * tb
    carry_rows = (CONV_WIDTH - 1) * rb
    width = (CONV_WIDTH - 1) * D_FF
    row_spec = pl.BlockSpec((rows, D_MODEL), lambda j: (j, 0))
    in_specs = [
        row_spec,
        (pl.BlockSpec(memory_space=pl.ANY) if p_native
         else pl.BlockSpec((None, rows, PLE_DIM), lambda j: (layer, j, 0))),
        _const_spec((rb, width), state_layer),
    ]
    in_specs += [_const_spec((layer, rb, width)) for _ in prev]
    in_specs += [
        _const_spec((DEPTH, D_MODEL)),
        _const_spec((D_MODEL, 2 * D_FF), layer),
        _const_spec((CONV_WIDTH, D_FF), layer),
        _const_spec((DEPTH, D_FF)),
        _const_spec((D_FF, D_MODEL), layer),
        _const_spec((DEPTH, D_MODEL)),
        _const_spec((D_MODEL, D_MODEL), layer),
        _const_spec((PLE_DIM, D_MODEL), layer),
        _const_spec((1, D_MODEL)),
    ]
    out_specs = [pl.BlockSpec(memory_space=pl.ANY) if out_native else row_spec,
                 _stack_spec(layer + 1, rb, width)]
    out_shape = [jax.ShapeDtypeStruct((rb, n // rb, D_MODEL) if out_native else (n, D_MODEL), _F32),
                 jax.ShapeDtypeStruct((layer + 1, rb, width), _F32)]
    scratch = [pltpu.VMEM((carry_rows + rows, FF_CHUNK_MAX), _F32)]
    if p_native:
        scratch += _native_scratch(tb, rb, PLE_DIM)
    if out_native:
        scratch += _native_scratch(tb, rb, D_MODEL)
    return pl.pallas_call(
        functools.partial(_ffn_kernel, rb=rb, tb=tb, layer=layer, final_norm=final_norm,
                          p_native=p_native, out_native=out_native),
        grid=(n // rows,),
        in_specs=in_specs,
        out_specs=out_specs,
        out_shape=out_shape,
        scratch_shapes=scratch,
        compiler_params=pltpu.CompilerParams(
            dimension_semantics=("arbitrary",), vmem_limit_bytes=VMEM_LIMIT_BYTES),
        name=f"ffn_l{layer}_rb{rb}",
    )(x, p, f0, *prev,
      wts["norm_ffn"], wts["w_up"], wts["ffn_conv_w"], wts["ffn_conv_b"], wts["w_down"],
      wts["norm_ple"], wts["w_ple_gate"], wts["w_ple"], wts["norm_final"])


def _run_trunk(x, p, conv0, sre0, sim0, f0, wts, rb, tb_mixer, tb_ffn, native, zero_state):
    mixer_states, ffn_state = [], []
    for layer in range(DEPTH):
        last = layer == DEPTH - 1
        state_layer = 0 if zero_state else layer
        x, *mixer_states = _mixer_call(x, (conv0, sre0, sim0), mixer_states, wts, layer, state_layer,
                                       rb, tb_mixer, x_native=native and layer == 0)
        x, *ffn_state = _ffn_call(x, p, f0, ffn_state, wts, layer, state_layer, rb, tb_ffn,
                                  final_norm=last, p_native=native, out_native=native and last)
    return (x, *mixer_states, *ffn_state)


def kernel(x_prompt, x_sample, p_prompt, p_sample, state_conv_a, state_ssm_re, state_ssm_im, state_ffn_conv, norm_mix, w_in, conv_a_w, conv_a_b, w_out_a, log_dt, lam_re, lam_im, b_re, b_im, c_re, c_im, d_skip, w_glu_a, w_glu_b, w_o, norm_ffn, w_up, ffn_conv_w, ffn_conv_b, w_down, norm_ple, w_ple_gate, w_ple, norm_final):
    abar, b2blk, c2_re, c2_im, fblk = _ssm_prep(log_dt, lam_re, lam_im, b_re, b_im, c_re, c_im)
    wts = {
        "norm_mix": norm_mix,
        "w_in": w_in.astype(_BF16),
        "conv_a_w": conv_a_w,
        "conv_a_b": conv_a_b,
        "w_out_a": w_out_a.astype(_BF16),
        "abar": abar, "b2blk": b2blk, "c2_re": c2_re, "c2_im": c2_im, "fblk": fblk,
        "d_skip": d_skip,
        "w_glu_a": w_glu_a.astype(_BF16),
        "w_glu_b": w_glu_b.astype(_BF16),
        "w_o": w_o.astype(_BF16),
        "norm_ffn": norm_ffn,
        "w_up": w_up.astype(_BF16),
        "ffn_conv_w": ffn_conv_w,
        "ffn_conv_b": ffn_conv_b,
        "w_down": w_down.astype(_BF16),
        "norm_ple": norm_ple,
        "w_ple_gate": w_ple_gate.astype(_BF16),
        "w_ple": w_ple.astype(_BF16),
        "norm_final": norm_final.reshape(1, D_MODEL),
    }
    conv_w = (CONV_WIDTH - 1) * CONV_A_WIDTH
    ffn_w = (CONV_WIDTH - 1) * D_FF

    bp = x_prompt.shape[0]
    zc = jnp.zeros((1, bp, conv_w), _F32)
    zs = jnp.zeros((1, bp, SSM_FLAT), _F32)
    zf = jnp.zeros((1, bp, ffn_w), _F32)
    y_prompt, conv_p, sre_p, sim_p, ffn_p = _run_trunk(
        x_prompt, p_prompt, zc, zs, zs, zf, wts, bp, MIXER_TIME_BLOCK, FFN_TIME_BLOCK,
        native=True, zero_state=True)

    bs, sseq, _ = x_sample.shape
    xs = x_sample.transpose(1, 0, 2).reshape(sseq * bs, D_MODEL)
    ps = p_sample.transpose(0, 2, 1, 3).reshape(DEPTH, sseq * bs, PLE_DIM)
    ys, conv_s, sre_s, sim_s, ffn_s = _run_trunk(
        xs, ps, state_conv_a.reshape(DEPTH, bs, conv_w),
        state_ssm_re.reshape(DEPTH, bs, SSM_FLAT), state_ssm_im.reshape(DEPTH, bs, SSM_FLAT),
        state_ffn_conv.reshape(DEPTH, bs, ffn_w), wts, bs, sseq, sseq,
        native=False, zero_state=False)
    y_sample = ys.reshape(sseq, bs, D_MODEL).transpose(1, 0, 2)

    def conv_state(s, b, c):
        return s.reshape(DEPTH, b, CONV_WIDTH - 1, c)

    def ssm_state(s, b):
        return s.reshape(DEPTH, b, SSM_GROUPS, SSM_STATE)

    return (y_prompt, y_sample,
            conv_state(conv_p, bp, CONV_A_WIDTH), ssm_state(sre_p, bp), ssm_state(sim_p, bp),
            conv_state(ffn_p, bp, D_FF),
            conv_state(conv_s, bs, CONV_A_WIDTH), ssm_state(sre_s, bs), ssm_state(sim_s, bs),
            conv_state(ffn_s, bs, D_FF))
```

```python
import functools
import math

import jax
import jax.numpy as jnp
from jax import lax
from jax.experimental import pallas as pl
from jax.experimental.pallas import tpu as pltpu

D_MODEL = 1024
DEPTH = 2
CONV_WIDTH = 3
CONV_A_WIDTH = 512
SSM_WIDTH = 512
SSM_GROUP = 16
SSM_GROUPS = 32
SSM_STATE = 64
SSM_FLAT = SSM_GROUPS * SSM_STATE
D_FF = 2816
PLE_DIM = 256
NORM_EPS = 1e-6
IN_PROJ_WIDTH = 4096

SSM_CHUNKS = 4
CHUNK_CH = SSM_WIDTH // SSM_CHUNKS
CHUNK_ST = SSM_FLAT // SSM_CHUNKS

FF_CHUNKS = ((0, 1024), (1024, 2048), (2048, 2816))
FF_CHUNK_MAX = 1024

MIXER_TIME_BLOCK = 128
FFN_TIME_BLOCK = 128
VMEM_LIMIT_BYTES = 56 * 1024 * 1024

_BF16 = jnp.bfloat16
_F32 = jnp.float32


def _dot(a, b):
    return jnp.dot(a, b, preferred_element_type=_F32)


def _dot_nt(a, b):
    return lax.dot_general(a, b, (((1,), (1,)), ((), ())), preferred_element_type=_F32)


def _dot_nt_split(a, b):
    a_hi = a.astype(_BF16)
    b_hi = b.astype(_BF16)
    a_lo = (a - a_hi.astype(_F32)).astype(_BF16)
    b_lo = (b - b_hi.astype(_F32)).astype(_BF16)
    return _dot_nt(a_hi, b_hi) + _dot_nt(a_hi, b_lo) + _dot_nt(a_lo, b_hi)


def _rms(x, g):
    ms = jnp.mean(x * x, axis=-1, keepdims=True)
    return x * lax.rsqrt(ms + NORM_EPS) * g


def _gelu(x):
    c = math.sqrt(2.0 / math.pi)
    t = jnp.tanh(x * (c + (0.044715 * c) * (x * x)))
    return x * (0.5 + 0.5 * t)


def _sigmoid(x):
    return 0.5 * jnp.tanh(0.5 * x) + 0.5


def _seq_copies(hbm_ref, lead, buf, sem, step, slot, tb, to_hbm):
    copies = []
    for b in range(buf.shape[2]):
        hbm_view = hbm_ref.at[lead + (b, pl.ds(step * tb, tb))]
        vmem_view = buf.at[slot, :, b]
        src, dst = (vmem_view, hbm_view) if to_hbm else (hbm_view, vmem_view)
        copies.append(pltpu.make_async_copy(src, dst, sem.at[slot]))
    return copies


def _load_time_major(hbm_ref, lead, buf, sem, tb):
    j = pl.program_id(0)
    slot = j % 2

    @pl.when(j == 0)
    def _():
        for c in _seq_copies(hbm_ref, lead, buf, sem, 0, 0, tb, False):
            c.start()

    @pl.when(j + 1 < pl.num_programs(0))
    def _():
        for c in _seq_copies(hbm_ref, lead, buf, sem, j + 1, 1 - slot, tb, False):
            c.start()

    for c in _seq_copies(hbm_ref, lead, buf, sem, j, slot, tb, False):
        c.wait()
    return buf[slot].reshape(tb * buf.shape[2], buf.shape[3])


def _store_batch_major(val, hbm_ref, buf, sem, tb):
    j = pl.program_id(0)
    slot = j % 2
    buf[slot] = val.reshape(tb, buf.shape[2], buf.shape[3])
    for c in _seq_copies(hbm_ref, (), buf, sem, j, slot, tb, True):
        c.start()

    @pl.when(j >= 1)
    def _():
        for c in _seq_copies(hbm_ref, (), buf, sem, j - 1, 1 - slot, tb, True):
            c.wait()

    @pl.when(j == pl.num_programs(0) - 1)
    def _():
        for c in _seq_copies(hbm_ref, (), buf, sem, j, slot, tb, True):
            c.wait()


def _conv_state_copies(hbm_ref, layer, carry, sem, to_hbm):
    copies = []
    for k in range(CONV_WIDTH - 1):
        hbm_view = hbm_ref.at[layer, :, k, :]
        src, dst = (carry.at[k], hbm_view) if to_hbm else (hbm_view, carry.at[k])
        copies.append(pltpu.make_async_copy(src, dst, sem.at[k]))
    return copies


def _conv_state_begin(state_hbm, layer, carry, sem):
    @pl.when(pl.program_id(0) == 0)
    def _():
        if state_hbm is None:
            carry[...] = jnp.zeros(carry.shape, carry.dtype)
        else:
            copies = _conv_state_copies(state_hbm, layer, carry, sem, False)
            for c in copies:
                c.start()
            for c in copies:
                c.wait()


def _conv_state_end(carry, out_hbm, layer, sem):
    @pl.when(pl.program_id(0) == pl.num_programs(0) - 1)
    def _():
        for slot in range(layer, out_hbm.shape[0]):
            copies = _conv_state_copies(out_hbm, slot, carry, sem, True)
            for c in copies:
                c.start()
            for c in copies:
                c.wait()


def _ssm_prep_kernel(logdt_ref, lre_ref, lim_ref, bre_ref, bim_ref, cre_ref, cim_ref,
                     abar_ref, b2blk_ref, c2_re_ref, c2_im_ref, fblk_ref):
    dt = jnp.exp(logdt_ref[...])
    lre = lre_ref[...]
    lim = lim_ref[...]
    mag = jnp.exp(lre * dt)
    ang = lim * dt
    ar = mag * jnp.cos(ang)
    ai = mag * jnp.sin(ang)
    abar_ref[0:1, :] = ar
    abar_ref[1:2, :] = ai
    abar_ref[2:3, :] = ar * ar - ai * ai
    abar_ref[3:4, :] = 2.0 * (ar * ai)
    den = lre * lre + lim * lim
    nr = ar - 1.0
    coef_re = (nr * lre + ai * lim) / den
    coef_im = (ai * lre - nr * lim) / den
    bre = bre_ref[...]
    bim = bim_ref[...]
    bbar_re = coef_re * bre - coef_im * bim
    bbar_im = coef_re * bim + coef_im * bre
    abb_re = ar * bbar_re - ai * bbar_im
    abb_im = ar * bbar_im + ai * bbar_re
    row = lax.broadcasted_iota(jnp.int32, (CHUNK_CH, CHUNK_ST), 0)
    lane = lax.broadcasted_iota(jnp.int32, (CHUNK_CH, CHUNK_ST), 1)
    same_group = (row // SSM_GROUP) == (lane // SSM_STATE)
    reps = CHUNK_CH // SSM_GROUP
    for c in range(SSM_CHUNKS):
        cols = slice(c * CHUNK_ST, (c + 1) * CHUNK_ST)
        rows = slice(c * CHUNK_CH, (c + 1) * CHUNK_CH)

        def b_block(v, cols=cols):
            return jnp.where(same_group, jnp.concatenate([v[:, cols]] * reps, axis=0), 0.0)

        b_re = b_block(bbar_re)
        b_im = b_block(bbar_im)
        b2blk_ref[c, 0:CHUNK_CH, 0:CHUNK_ST] = b_block(abb_re).astype(_BF16)
        b2blk_ref[c, 0:CHUNK_CH, CHUNK_ST:2 * CHUNK_ST] = b_block(abb_im).astype(_BF16)
        b2blk_ref[c, CHUNK_CH:2 * CHUNK_CH, 0:CHUNK_ST] = b_re.astype(_BF16)
        b2blk_ref[c, CHUNK_CH:2 * CHUNK_CH, CHUNK_ST:2 * CHUNK_ST] = b_im.astype(_BF16)
        c_re = jnp.where(same_group, cre_ref[rows, :], 0.0)
        c_im = jnp.where(same_group, cim_ref[rows, :], 0.0)
        arc = ar[:, cols]
        aic = ai[:, cols]
        c2_re_ref[c, 0:CHUNK_CH, :] = c_re.astype(_BF16)
        c2_re_ref[c, CHUNK_CH:2 * CHUNK_CH, :] = (c_re * arc - c_im * aic).astype(_BF16)
        c2_im_ref[c, 0:CHUNK_CH, :] = (-c_im).astype(_BF16)
        c2_im_ref[c, CHUNK_CH:2 * CHUNK_CH, :] = (-(c_re * aic + c_im * arc)).astype(_BF16)
        feed = _dot_nt_split(b_re, c_re) - _dot_nt_split(b_im, c_im)
        fblk_ref[c] = feed.astype(_BF16)


def _ssm_prep(log_dt, lam_re, lam_im, b_re, b_im, c_re, c_im):
    logdt = jnp.repeat(log_dt, SSM_STATE, axis=1).reshape(DEPTH, 1, SSM_FLAT)
    lre = lam_re.reshape(DEPTH, 1, SSM_FLAT)
    lim = lam_im.reshape(DEPTH, 1, SSM_FLAT)
    bre = b_re.transpose(0, 3, 1, 2).reshape(DEPTH, SSM_GROUP, SSM_FLAT)
    bim = b_im.transpose(0, 3, 1, 2).reshape(DEPTH, SSM_GROUP, SSM_FLAT)
    reps = CHUNK_ST // SSM_STATE
    cre = jnp.tile(c_re.reshape(DEPTH, SSM_WIDTH, SSM_STATE), (1, 1, reps))
    cim = jnp.tile(c_im.reshape(DEPTH, SSM_WIDTH, SSM_STATE), (1, 1, reps))

    def spec(*shape):
        return pl.BlockSpec((None,) + shape, lambda i: (i,) + (0,) * len(shape))

    return pl.pallas_call(
        _ssm_prep_kernel,
        grid=(DEPTH,),
        in_specs=[spec(1, SSM_FLAT), spec(1, SSM_FLAT), spec(1, SSM_FLAT),
                  spec(SSM_GROUP, SSM_FLAT), spec(SSM_GROUP, SSM_FLAT),
                  spec(SSM_WIDTH, CHUNK_ST), spec(SSM_WIDTH, CHUNK_ST)],
        out_specs=[spec(4, SSM_FLAT),
                   spec(SSM_CHUNKS, 2 * CHUNK_CH, 2 * CHUNK_ST),
                   spec(SSM_CHUNKS, 2 * CHUNK_CH, CHUNK_ST),
                   spec(SSM_CHUNKS, 2 * CHUNK_CH, CHUNK_ST),
                   spec(SSM_CHUNKS, CHUNK_CH, CHUNK_CH)],
        out_shape=[jax.ShapeDtypeStruct((DEPTH, 4, SSM_FLAT), _F32),
                   jax.ShapeDtypeStruct((DEPTH, SSM_CHUNKS, 2 * CHUNK_CH, 2 * CHUNK_ST), _BF16),
                   jax.ShapeDtypeStruct((DEPTH, SSM_CHUNKS, 2 * CHUNK_CH, CHUNK_ST), _BF16),
                   jax.ShapeDtypeStruct((DEPTH, SSM_CHUNKS, 2 * CHUNK_CH, CHUNK_ST), _BF16),
                   jax.ShapeDtypeStruct((DEPTH, SSM_CHUNKS, CHUNK_CH, CHUNK_CH), _BF16)],
        name="ssm_prep",
    )(logdt, lre, lim, bre, bim, cre, cim)


def _mixer_kernel(x_ref, *refs, rb, tb, layer, zero_conv):
    refs = list(refs)
    conv0_hbm = None if zero_conv else refs.pop(0)
    sre0_ref, sim0_ref = refs.pop(0), refs.pop(0)
    prev = [refs.pop(0) for _ in range(3)] if layer > 0 else []
    (g_ref, w_in_ref, cw_ref, cb_ref, w_out_a_ref,
     abar_ref, b2blk_ref, c2_re_ref, c2_im_ref, fblk_ref, dskip_ref,
     glu_a_ref, glu_b_ref, w_o_ref,
     out_ref, conv_out_hbm, sre_stack_ref, sim_stack_ref,
     vbuf, s_re, s_im, conv_carry, conv_sem, *x_dma) = refs
    rows = rb * tb
    carry_rows = (CONV_WIDTH - 1) * rb
    sre_out_ref = sre_stack_ref.at[layer]
    sim_out_ref = sim_stack_ref.at[layer]

    @pl.when(pl.program_id(0) == 0)
    def _():
        for stack_ref, prev_ref in zip((sre_stack_ref, sim_stack_ref), prev[1:]):
            stack_ref[0:layer] = prev_ref[...]
        sre_out_ref[...] = sre0_ref[...]
        sim_out_ref[...] = sim0_ref[...]

    _conv_state_begin(conv0_hbm, layer, conv_carry, conv_sem)

    x = _load_time_major(x_ref, (), *x_dma, tb) if x_dma else x_ref[...]
    xn = _rms(x, g_ref[layer:layer + 1, :]).astype(_BF16)

    def proj(lo, hi):
        return _dot(xn, w_in_ref[:, lo:hi])

    u = proj(1536, 2048)
    pair = 2 if tb % 2 == 0 else 1
    steps = tb // pair
    srows = steps * rb
    if pair == 2:
        u4 = u.reshape(steps, 2, rb, SSM_WIDTH)
        u_parts = [u4[:, i].reshape(srows, SSM_WIDTH) for i in range(2)]
    else:
        u_parts = [u]
    ub_parts = [v.astype(_BF16) for v in u_parts]
    s_re[0:rb, :] = sre_out_ref[...]
    s_im[0:rb, :] = sim_out_ref[...]
    for c in range(SSM_CHUNKS):
        ch = slice(c * CHUNK_CH, (c + 1) * CHUNK_CH)
        if pair == 2:
            inc = _dot(jnp.concatenate([ub_parts[0][:, ch], ub_parts[1][:, ch]], axis=1), b2blk_ref[c])
        else:
            inc = _dot(ub_parts[0][:, ch], b2blk_ref[c, CHUNK_CH:2 * CHUNK_CH, :])
        s_re[rb:rb + srows, c * CHUNK_ST:(c + 1) * CHUNK_ST] = inc[:, 0:CHUNK_ST]
        s_im[rb:rb + srows, c * CHUNK_ST:(c + 1) * CHUNK_ST] = inc[:, CHUNK_ST:2 * CHUNK_ST]

    a_row = 2 * (pair - 1)
    for c in range(SSM_CHUNKS):
        cols = slice(c * CHUNK_ST, (c + 1) * CHUNK_ST)
        ar = jnp.broadcast_to(abar_ref[a_row:a_row + 1, cols], (rb, CHUNK_ST))
        ai = jnp.broadcast_to(abar_ref[a_row + 1:a_row + 2, cols], (rb, CHUNK_ST))

        def step(k, carry, cols=cols, ar=ar, ai=ai):
            sr, si = carry
            r0 = pl.multiple_of(rb + k * rb, rb)
            nsr = ar * sr - ai * si + s_re[pl.ds(r0, rb), cols]
            nsi = ar * si + ai * sr + s_im[pl.ds(r0, rb), cols]
            s_re[pl.ds(r0, rb), cols] = nsr
            s_im[pl.ds(r0, rb), cols] = nsi
            return nsr, nsi

        sr, si = lax.fori_loop(0, steps, step, (s_re[0:rb, cols], s_im[0:rb, cols]), unroll=True)
        sre_out_ref[:, cols] = sr
        sim_out_ref[:, cols] = si

    h = proj(0, 512)
    gate_c = proj(1024, 1536)
    vbuf[0:rb, :] = conv_carry[0]
    vbuf[rb:carry_rows, :] = conv_carry[1]
    vbuf[carry_rows:carry_rows + rows, :] = gate_c * h
    conv_y = (cb_ref[layer:layer + 1, :]
              + vbuf[0:rows, :] * cw_ref[0:1, :]
              + vbuf[rb:rb + rows, :] * cw_ref[1:2, :]
              + vbuf[2 * rb:2 * rb + rows, :] * cw_ref[2:3, :])
    conv_carry[0] = vbuf[rows:rows + rb, :]
    conv_carry[1] = vbuf[rows + rb:rows + carry_rows, :]
    gate_b = proj(512, 1024)
    z_a = _dot((gate_b * conv_y).astype(_BF16), w_out_a_ref[...])
    merged = _sigmoid(proj(2048, 3072)) * z_a

    y_parts = [[] for _ in range(pair)]
    for c in range(SSM_CHUNKS):
        ch = slice(c * CHUNK_CH, (c + 1) * CHUNK_CH)
        cols = slice(c * CHUNK_ST, (c + 1) * CHUNK_ST)
        if pair == 2:
            y2 = (_dot_nt(s_re[:, cols].astype(_BF16), c2_re_ref[c])
                  + _dot_nt(s_im[:, cols].astype(_BF16), c2_im_ref[c]))
            y_parts[0].append(y2[0:srows, CHUNK_CH:2 * CHUNK_CH] + _dot(ub_parts[0][:, ch], fblk_ref[c]))
            y_parts[1].append(y2[rb:rb + srows, 0:CHUNK_CH])
        else:
            y_parts[0].append(
                _dot_nt(s_re[rb:rb + srows, cols].astype(_BF16), c2_re_ref[c, 0:CHUNK_CH, :])
                + _dot_nt(s_im[rb:rb + srows, cols].astype(_BF16), c2_im_ref[c, 0:CHUNK_CH, :]))
    d_skip = dskip_ref[layer:layer + 1, :]
    ssm_parts = [jnp.concatenate(yp, axis=1) + d_skip * up for yp, up in zip(y_parts, u_parts)]
    if pair == 2:
        ssm_y = jnp.stack([v.reshape(steps, rb, SSM_WIDTH) for v in ssm_parts],
                          axis=1).reshape(rows, SSM_WIDTH)
    else:
        ssm_y = ssm_parts[0]
    sy = _gelu(ssm_y).astype(_BF16)
    z_b = _dot(sy, glu_a_ref[...]) * _sigmoid(_dot(sy, glu_b_ref[...]))
    merged = (merged + _sigmoid(proj(3072, 4096)) * z_b).astype(_BF16)
    out_ref[...] = x + _dot(merged, w_o_ref[...])
    _conv_state_end(conv_carry, conv_out_hbm, layer, conv_sem)


def _ffn_kernel(x_ref, p_ref, *refs, rb, tb, layer, zero_conv, final_norm, p_native, out_native):
    refs = list(refs)
    f0_hbm = None if zero_conv else refs.pop(0)
    if layer > 0:
        refs.pop(0)
    (g2_ref, w_up_ref, fcw_ref, fcb_ref, w_down_ref,
     g3_ref, w_pg_ref, w_ple_ref, gf_ref,
     out_ref, fconv_out_hbm,
     fbuf, fcarry, fsem, *dma) = refs
    rows = rb * tb
    carry_rows = (CONV_WIDTH - 1) * rb
    p_dma = [dma.pop(0), dma.pop(0)] if p_native else None
    out_dma = [dma.pop(0), dma.pop(0)] if out_native else None

    _conv_state_begin(f0_hbm, layer, fcarry, fsem)

    x = x_ref[...]
    xn = _rms(x, g2_ref[layer:layer + 1, :]).astype(_BF16)
    acc = x
    for lo, hi in FF_CHUNKS:
        wc = hi - lo
        up_a = _dot(xn, w_up_ref[:, lo:hi])
        up_b = _dot(xn, w_up_ref[:, D_FF + lo:D_FF + hi])
        fbuf[0:rb, 0:wc] = fcarry[0, :, lo:hi]
        fbuf[rb:carry_rows, 0:wc] = fcarry[1, :, lo:hi]
        fbuf[carry_rows:carry_rows + rows, 0:wc] = up_a
        conv_a = (fcb_ref[layer:layer + 1, lo:hi]
                  + fbuf[0:rows, 0:wc] * fcw_ref[0:1, lo:hi]
                  + fbuf[rb:rb + rows, 0:wc] * fcw_ref[1:2, lo:hi]
                  + fbuf[2 * rb:2 * rb + rows, 0:wc] * fcw_ref[2:3, lo:hi])
        fcarry[0, :, lo:hi] = fbuf[rows:rows + rb, 0:wc]
        fcarry[1, :, lo:hi] = fbuf[rows + rb:rows + carry_rows, 0:wc]
        hid = (_gelu(conv_a) * up_b).astype(_BF16)
        acc = acc + _dot(hid, w_down_ref[lo:hi, :])
    x2 = acc
    xn3 = _rms(x2, g3_ref[layer:layer + 1, :]).astype(_BF16)
    gate = _sigmoid(_dot(xn3, w_pg_ref[...]))
    p = _load_time_major(p_ref, (layer,), *p_dma, tb) if p_native else p_ref[...]
    pe = _dot(p.astype(_BF16), w_ple_ref[...])
    x3 = x2 + gate * pe
    if final_norm:
        x3 = _rms(x3, gf_ref[...])
    if out_native:
        _store_batch_major(x3, out_ref, *out_dma, tb)
    else:
        out_ref[...] = x3
    _conv_state_end(fcarry, fconv_out_hbm, layer, fsem)


def _const_spec(shape, layer=None):
    if layer is None:
        return pl.BlockSpec(shape, lambda j: (0,) * len(shape), pipeline_mode=pl.Buffered(1))
    return pl.BlockSpec((None,) + shape, lambda j: (layer,) + (0,) * len(shape),
                        pipeline_mode=pl.Buffered(1))


def _native_scratch(tb, rb, width):
    return [pltpu.VMEM((2, tb, rb, width), _F32), pltpu.SemaphoreType.DMA((2,))]


def _stack_spec(n_layers, rb, width):
    return pl.BlockSpec((n_layers, rb, width), lambda j: (0, 0, 0))


def _conv_state_scratch(rb, width):
    return [pltpu.VMEM((CONV_WIDTH - 1, rb, width), _F32), pltpu.SemaphoreType.DMA((CONV_WIDTH - 1,))]


def _mixer_call(x, conv0, ssm0, prev, wts, layer, ssm_layer, rb, tb, x_native):
    n = x.shape[0] * x.shape[1] if x_native else x.shape[0]
    rows = rb * tb
    carry_rows = (CONV_WIDTH - 1) * rb
    hbm_spec = pl.BlockSpec(memory_space=pl.ANY)
    row_spec = pl.BlockSpec((rows, D_MODEL), lambda j: (j, 0))
    operands = [x]
    in_specs = [hbm_spec if x_native else row_spec]
    if conv0 is not None:
        operands.append(conv0)
        in_specs.append(hbm_spec)
    operands += list(ssm0)
    in_specs += [_const_spec((rb, SSM_FLAT), ssm_layer)] * 2
    aliases = {}
    if prev:
        aliases[len(operands)] = 1
        operands += list(prev)
        in_specs += [hbm_spec] + [_const_spec((layer, rb, SSM_FLAT))] * 2
    in_specs += [
        _const_spec((DEPTH, D_MODEL)),
        _const_spec((D_MODEL, IN_PROJ_WIDTH), layer),
        _const_spec((CONV_WIDTH, CONV_A_WIDTH), layer),
        _const_spec((DEPTH, CONV_A_WIDTH)),
        _const_spec((CONV_A_WIDTH, D_MODEL), layer),
        _const_spec((4, SSM_FLAT), layer),
        _const_spec((SSM_CHUNKS, 2 * CHUNK_CH, 2 * CHUNK_ST), layer),
        _const_spec((SSM_CHUNKS, 2 * CHUNK_CH, CHUNK_ST), layer),
        _const_spec((SSM_CHUNKS, 2 * CHUNK_CH, CHUNK_ST), layer),
        _const_spec((SSM_CHUNKS, CHUNK_CH, CHUNK_CH), layer),
        _const_spec((DEPTH, SSM_WIDTH)),
        _const_spec((SSM_WIDTH, D_MODEL), layer),
        _const_spec((SSM_WIDTH, D_MODEL), layer),
        _const_spec((D_MODEL, D_MODEL), layer),
    ]
    out_specs = [row_spec, hbm_spec] + [_stack_spec(layer + 1, rb, SSM_FLAT)] * 2
    out_shape = ([jax.ShapeDtypeStruct((n, D_MODEL), _F32),
                  jax.ShapeDtypeStruct((DEPTH, rb, CONV_WIDTH - 1, CONV_A_WIDTH), _F32)]
                 + [jax.ShapeDtypeStruct((layer + 1, rb, SSM_FLAT), _F32)] * 2)
    s_rows = rows // 2 if tb % 2 == 0 else rows
    scratch = [
        pltpu.VMEM((carry_rows + rows, CONV_A_WIDTH), _F32),
        pltpu.VMEM((rb + s_rows, SSM_FLAT), _F32),
        pltpu.VMEM((rb + s_rows, SSM_FLAT), _F32),
    ]
    scratch += _conv_state_scratch(rb, CONV_A_WIDTH)
    if x_native:
        scratch += _native_scratch(tb, rb, D_MODEL)
    return pl.pallas_call(
        functools.partial(_mixer_kernel, rb=rb, tb=tb, layer=layer, zero_conv=conv0 is None),
        grid=(n // rows,),
        in_specs=in_specs,
        out_specs=out_specs,
        out_shape=out_shape,
        scratch_shapes=scratch,
        input_output_aliases=aliases,
        compiler_params=pltpu.CompilerParams(
            dimension_semantics=("arbitrary",), vmem_limit_bytes=VMEM_LIMIT_BYTES),
        name=f"mixer_l{layer}_rb{rb}",
    )(*operands,
      wts["norm_mix"], wts["w_in"], wts["conv_a_w"], wts["conv_a_b"], wts["w_out_a"],
      wts["abar"], wts["b2blk"], wts["c2_re"], wts["c2_im"], wts["fblk"], wts["d_skip"],
      wts["w_glu_a"], wts["w_glu_b"], wts["w_o"])


def _ffn_call(x, p, f0, prev, wts, layer, rb, tb, final_norm, p_native, out_native):
    n = x.shape[0]
    rows = rb * tb
    carry_rows = (CONV_WIDTH - 1) * rb
    hbm_spec = pl.BlockSpec(memory_space=pl.ANY)
    row_spec = pl.BlockSpec((rows, D_MODEL), lambda j: (j, 0))
    operands = [x, p]
    in_specs = [
        row_spec,
        hbm_spec if p_native else pl.BlockSpec((None, rows, PLE_DIM), lambda j: (layer, j, 0)),
    ]
    if f0 is not None:
        operands.append(f0)
        in_specs.append(hbm_spec)
    aliases = {}
    if prev is not None:
        aliases[len(operands)] = 1
        operands.append(prev)
        in_specs.append(hbm_spec)
    in_specs += [
        _const_spec((DEPTH, D_MODEL)),
        _const_spec((D_MODEL, 2 * D_FF), layer),
        _const_spec((CONV_WIDTH, D_FF), layer),
        _const_spec((DEPTH, D_FF)),
        _const_spec((D_FF, D_MODEL), layer),
        _const_spec((DEPTH, D_MODEL)),
        _const_spec((D_MODEL, D_MODEL), layer),
        _const_spec((PLE_DIM, D_MODEL), layer),
        _const_spec((1, D_MODEL)),
    ]
    out_specs = [hbm_spec if out_native else row_spec, hbm_spec]
    out_shape = [jax.ShapeDtypeStruct((rb, n // rb, D_MODEL) if out_native else (n, D_MODEL), _F32),
                 jax.ShapeDtypeStruct((DEPTH, rb, CONV_WIDTH - 1, D_FF), _F32)]
    scratch = [pltpu.VMEM((carry_rows + rows, FF_CHUNK_MAX), _F32)]
    scratch += _conv_state_scratch(rb, D_FF)
    if p_native:
        scratch += _native_scratch(tb, rb, PLE_DIM)
    if out_native:
        scratch += _native_scratch(tb, rb, D_MODEL)
    return pl.pallas_call(
        functools.partial(_ffn_kernel, rb=rb, tb=tb, layer=layer, zero_conv=f0 is None,
                          final_norm=final_norm, p_native=p_native, out_native=out_native),
        grid=(n // rows,),
        in_specs=in_specs,
        out_specs=out_specs,
        out_shape=out_shape,
        scratch_shapes=scratch,
        input_output_aliases=aliases,
        compiler_params=pltpu.CompilerParams(
            dimension_semantics=("arbitrary",), vmem_limit_bytes=VMEM_LIMIT_BYTES),
        name=f"ffn_l{layer}_rb{rb}",
    )(*operands,
      wts["norm_ffn"], wts["w_up"], wts["ffn_conv_w"], wts["ffn_conv_b"], wts["w_down"],
      wts["norm_ple"], wts["w_ple_gate"], wts["w_ple"], wts["norm_final"])


def _run_trunk(x, p, conv0, sre0, sim0, f0, wts, rb, tb_mixer, tb_ffn, native):
    mixer_states, ffn_state = [], None
    for layer in range(DEPTH):
        last = layer == DEPTH - 1
        ssm_layer = 0 if conv0 is None else layer
        x, *mixer_states = _mixer_call(x, conv0, (sre0, sim0), mixer_states, wts, layer, ssm_layer,
                                       rb, tb_mixer, x_native=native and layer == 0)
        x, ffn_state = _ffn_call(x, p, f0, ffn_state, wts, layer, rb, tb_ffn,
                                 final_norm=last, p_native=native, out_native=native and last)
    return (x, *mixer_states, ffn_state)


def kernel(x_prompt, x_sample, p_prompt, p_sample, state_conv_a, state_ssm_re, state_ssm_im, state_ffn_conv, norm_mix, w_in, conv_a_w, conv_a_b, w_out_a, log_dt, lam_re, lam_im, b_re, b_im, c_re, c_im, d_skip, w_glu_a, w_glu_b, w_o, norm_ffn, w_up, ffn_conv_w, ffn_conv_b, w_down, norm_ple, w_ple_gate, w_ple, norm_final):
    abar, b2blk, c2_re, c2_im, fblk = _ssm_prep(log_dt, lam_re, lam_im, b_re, b_im, c_re, c_im)
    wts = {
        "norm_mix": norm_mix,
        "w_in": w_in.astype(_BF16),
        "conv_a_w": conv_a_w,
        "conv_a_b": conv_a_b,
        "w_out_a": w_out_a.astype(_BF16),
        "abar": abar, "b2blk": b2blk, "c2_re": c2_re, "c2_im": c2_im, "fblk": fblk,
        "d_skip": d_skip,
        "w_glu_a": w_glu_a.astype(_BF16),
        "w_glu_b": w_glu_b.astype(_BF16),
        "w_o": w_o.astype(_BF16),
        "norm_ffn": norm_ffn,
        "w_up": w_up.astype(_BF16),
        "ffn_conv_w": ffn_conv_w,
        "ffn_conv_b": ffn_conv_b,
        "w_down": w_down.astype(_BF16),
        "norm_ple": norm_ple,
        "w_ple_gate": w_ple_gate.astype(_BF16),
        "w_ple": w_ple.astype(_BF16),
        "norm_final": norm_final.reshape(1, D_MODEL),
    }

    bp = x_prompt.shape[0]
    zs = jnp.zeros((1, bp, SSM_FLAT), _F32)
    y_prompt, conv_p, sre_p, sim_p, ffn_p = _run_trunk(
        x_prompt, p_prompt, None, zs, zs, None, wts, bp, MIXER_TIME_BLOCK, FFN_TIME_BLOCK,
        native=True)

    bs, sseq, _ = x_sample.shape
    xs = x_sample.transpose(1, 0, 2).reshape(sseq * bs, D_MODEL)
    ps = p_sample.transpose(0, 2, 1, 3).reshape(DEPTH, sseq * bs, PLE_DIM)
    ys, conv_s, sre_s, sim_s, ffn_s = _run_trunk(
        xs, ps, state_conv_a,
        state_ssm_re.reshape(DEPTH, bs, SSM_FLAT), state_ssm_im.reshape(DEPTH, bs, SSM_FLAT),
        state_ffn_conv, wts, bs, sseq, sseq, native=False)
    y_sample = ys.reshape(sseq, bs, D_MODEL).transpose(1, 0, 2)

    def ssm_state(s, b):
        return s.reshape(DEPTH, b, SSM_GROUPS, SSM_STATE)

    return (y_prompt, y_sample,
            conv_p, ssm_state(sre_p, bp), ssm_state(sim_p, bp), ffn_p,
            conv_s, ssm_state(sre_s, bs), ssm_state(sim_s, bs), ffn_s)
```

```python
import functools
import math

import jax
import jax.numpy as jnp
from jax import lax
from jax.experimental import pallas as pl
from jax.experimental.pallas import tpu as pltpu

D_MODEL = 1024
DEPTH = 2
CONV_WIDTH = 3
CONV_A_WIDTH = 512
SSM_WIDTH = 512
SSM_GROUP = 16
SSM_GROUPS = 32
SSM_STATE = 64
SSM_FLAT = SSM_GROUPS * SSM_STATE
D_FF = 2816
PLE_DIM = 256
NORM_EPS = 1e-6
IN_PROJ_WIDTH = 4096

SSM_CHUNKS = 4
CHUNK_CH = SSM_WIDTH // SSM_CHUNKS
CHUNK_ST = SSM_FLAT // SSM_CHUNKS

FF_CHUNKS = ((0, 1024), (1024, 2048), (2048, 2816))
FF_CHUNK_MAX = 1024

MIXER_TIME_BLOCK = 128
FFN_TIME_BLOCK = 128
VMEM_LIMIT_BYTES = 56 * 1024 * 1024

_BF16 = jnp.bfloat16
_F32 = jnp.float32


def _dot(a, b):
    return jnp.dot(a, b, preferred_element_type=_F32)


def _dot_nt(a, b):
    return lax.dot_general(a, b, (((1,), (1,)), ((), ())), preferred_element_type=_F32)


def _dot_nt_split(a, b):
    a_hi = a.astype(_BF16)
    b_hi = b.astype(_BF16)
    a_lo = (a - a_hi.astype(_F32)).astype(_BF16)
    b_lo = (b - b_hi.astype(_F32)).astype(_BF16)
    return _dot_nt(a_hi, b_hi) + _dot_nt(a_hi, b_lo) + _dot_nt(a_lo, b_hi)


def _rms(x, g):
    ms = jnp.mean(x * x, axis=-1, keepdims=True)
    return x * lax.rsqrt(ms + NORM_EPS) * g


def _gelu(x):
    c = math.sqrt(2.0 / math.pi)
    t = jnp.tanh(x * (c + (0.044715 * c) * (x * x)))
    return x * (0.5 + 0.5 * t)


def _sigmoid(x):
    return 0.5 * jnp.tanh(0.5 * x) + 0.5


def _seq_copies(hbm_ref, lead, buf, sem, step, slot, tb, to_hbm):
    copies = []
    for b in range(buf.shape[2]):
        hbm_view = hbm_ref.at[lead + (b, pl.ds(step * tb, tb))]
        vmem_view = buf.at[slot, :, b]
        src, dst = (vmem_view, hbm_view) if to_hbm else (hbm_view, vmem_view)
        copies.append(pltpu.make_async_copy(src, dst, sem.at[slot]))
    return copies


def _fetch_time_major(hbm_ref, lead, buf, sem, tb):
    j = pl.program_id(0)
    slot = j % 2

    @pl.when(j == 0)
    def _():
        for c in _seq_copies(hbm_ref, lead, buf, sem, 0, 0, tb, False):
            c.start()

    @pl.when(j + 1 < pl.num_programs(0))
    def _():
        for c in _seq_copies(hbm_ref, lead, buf, sem, j + 1, 1 - slot, tb, False):
            c.start()

    for c in _seq_copies(hbm_ref, lead, buf, sem, j, slot, tb, False):
        c.wait()


def _read_time_major(buf):
    tb, nb, width = buf.shape[1:]
    return buf[pl.program_id(0) % 2].reshape(tb * nb, width)


def _store_batch_major(val, hbm_ref, buf, sem, tb):
    j = pl.program_id(0)
    slot = j % 2
    buf[slot] = val.reshape(tb, buf.shape[2], buf.shape[3])
    for c in _seq_copies(hbm_ref, (), buf, sem, j, slot, tb, True):
        c.start()

    @pl.when(j >= 1)
    def _():
        for c in _seq_copies(hbm_ref, (), buf, sem, j - 1, 1 - slot, tb, True):
            c.wait()

    @pl.when(j == pl.num_programs(0) - 1)
    def _():
        for c in _seq_copies(hbm_ref, (), buf, sem, j, slot, tb, True):
            c.wait()


def _conv_state_copies(hbm_ref, slot, carry, sem, to_hbm):
    copies = []
    for k in range(CONV_WIDTH - 1):
        hbm_view = hbm_ref.at[slot, :, k, :]
        src, dst = (carry.at[k], hbm_view) if to_hbm else (hbm_view, carry.at[k])
        copies.append(pltpu.make_async_copy(src, dst, sem.at[slot, k]))
    return copies


def _at_step(first, single_step, body):
    if single_step:
        body()
    else:
        step = 0 if first else pl.num_programs(0) - 1
        pl.when(pl.program_id(0) == step)(body)


def _conv_state_fetch(state_hbm, layer, carry, sem, single_step, wait):
    def body():
        if state_hbm is None:
            if not wait:
                carry[...] = jnp.zeros(carry.shape, carry.dtype)
        else:
            for c in _conv_state_copies(state_hbm, layer, carry, sem, False):
                c.wait() if wait else c.start()

    _at_step(True, single_step, body)


def _conv_state_store(carry, out_hbm, layer, sem, single_step, wait):
    def body():
        for slot in range(layer, out_hbm.shape[0]):
            for c in _conv_state_copies(out_hbm, slot, carry, sem, True):
                c.wait() if wait else c.start()

    _at_step(False, single_step, body)


def _ssm_prep_kernel(logdt_ref, lre_ref, lim_ref, bre_ref, bim_ref, cre_ref, cim_ref,
                     abar_ref, b2blk_ref, c2_re_ref, c2_im_ref, fblk_ref):
    dt = jnp.exp(logdt_ref[...])
    lre = lre_ref[...]
    lim = lim_ref[...]
    mag = jnp.exp(lre * dt)
    ang = lim * dt
    ar = mag * jnp.cos(ang)
    ai = mag * jnp.sin(ang)
    abar_ref[0:1, :] = ar
    abar_ref[1:2, :] = ai
    abar_ref[2:3, :] = ar * ar - ai * ai
    abar_ref[3:4, :] = 2.0 * (ar * ai)
    den = lre * lre + lim * lim
    nr = ar - 1.0
    coef_re = (nr * lre + ai * lim) / den
    coef_im = (ai * lre - nr * lim) / den
    bre = bre_ref[...]
    bim = bim_ref[...]
    bbar_re = coef_re * bre - coef_im * bim
    bbar_im = coef_re * bim + coef_im * bre
    abb_re = ar * bbar_re - ai * bbar_im
    abb_im = ar * bbar_im + ai * bbar_re
    row = lax.broadcasted_iota(jnp.int32, (CHUNK_CH, CHUNK_ST), 0)
    lane = lax.broadcasted_iota(jnp.int32, (CHUNK_CH, CHUNK_ST), 1)
    same_group = (row // SSM_GROUP) == (lane // SSM_STATE)
    reps = CHUNK_CH // SSM_GROUP
    for c in range(SSM_CHUNKS):
        cols = slice(c * CHUNK_ST, (c + 1) * CHUNK_ST)
        rows = slice(c * CHUNK_CH, (c + 1) * CHUNK_CH)

        def b_block(v, cols=cols):
            return jnp.where(same_group, jnp.concatenate([v[:, cols]] * reps, axis=0), 0.0)

        b_re = b_block(bbar_re)
        b_im = b_block(bbar_im)
        b2blk_ref[c, 0:CHUNK_CH, 0:CHUNK_ST] = b_block(abb_re).astype(_BF16)
        b2blk_ref[c, 0:CHUNK_CH, CHUNK_ST:2 * CHUNK_ST] = b_block(abb_im).astype(_BF16)
        b2blk_ref[c, CHUNK_CH:2 * CHUNK_CH, 0:CHUNK_ST] = b_re.astype(_BF16)
        b2blk_ref[c, CHUNK_CH:2 * CHUNK_CH, CHUNK_ST:2 * CHUNK_ST] = b_im.astype(_BF16)
        c_re = jnp.where(same_group, cre_ref[rows, :], 0.0)
        c_im = jnp.where(same_group, cim_ref[rows, :], 0.0)
        arc = ar[:, cols]
        aic = ai[:, cols]
        c2_re_ref[c, 0:CHUNK_CH, :] = c_re.astype(_BF16)
        c2_re_ref[c, CHUNK_CH:2 * CHUNK_CH, :] = (c_re * arc - c_im * aic).astype(_BF16)
        c2_im_ref[c, 0:CHUNK_CH, :] = (-c_im).astype(_BF16)
        c2_im_ref[c, CHUNK_CH:2 * CHUNK_CH, :] = (-(c_re * aic + c_im * arc)).astype(_BF16)
        feed = _dot_nt_split(b_re, c_re) - _dot_nt_split(b_im, c_im)
        fblk_ref[c] = feed.astype(_BF16)


def _ssm_prep(log_dt, lam_re, lam_im, b_re, b_im, c_re, c_im):
    logdt = jnp.repeat(log_dt, SSM_STATE, axis=1).reshape(DEPTH, 1, SSM_FLAT)
    lre = lam_re.reshape(DEPTH, 1, SSM_FLAT)
    lim = lam_im.reshape(DEPTH, 1, SSM_FLAT)
    bre = b_re.transpose(0, 3, 1, 2).reshape(DEPTH, SSM_GROUP, SSM_FLAT)
    bim = b_im.transpose(0, 3, 1, 2).reshape(DEPTH, SSM_GROUP, SSM_FLAT)
    reps = CHUNK_ST // SSM_STATE
    cre = jnp.tile(c_re.reshape(DEPTH, SSM_WIDTH, SSM_STATE), (1, 1, reps))
    cim = jnp.tile(c_im.reshape(DEPTH, SSM_WIDTH, SSM_STATE), (1, 1, reps))

    def spec(*shape):
        return pl.BlockSpec((None,) + shape, lambda i: (i,) + (0,) * len(shape))

    return pl.pallas_call(
        _ssm_prep_kernel,
        grid=(DEPTH,),
        in_specs=[spec(1, SSM_FLAT), spec(1, SSM_FLAT), spec(1, SSM_FLAT),
                  spec(SSM_GROUP, SSM_FLAT), spec(SSM_GROUP, SSM_FLAT),
                  spec(SSM_WIDTH, CHUNK_ST), spec(SSM_WIDTH, CHUNK_ST)],
        out_specs=[spec(4, SSM_FLAT),
                   spec(SSM_CHUNKS, 2 * CHUNK_CH, 2 * CHUNK_ST),
                   spec(SSM_CHUNKS, 2 * CHUNK_CH, CHUNK_ST),
                   spec(SSM_CHUNKS, 2 * CHUNK_CH, CHUNK_ST),
                   spec(SSM_CHUNKS, CHUNK_CH, CHUNK_CH)],
        out_shape=[jax.ShapeDtypeStruct((DEPTH, 4, SSM_FLAT), _F32),
                   jax.ShapeDtypeStruct((DEPTH, SSM_CHUNKS, 2 * CHUNK_CH, 2 * CHUNK_ST), _BF16),
                   jax.ShapeDtypeStruct((DEPTH, SSM_CHUNKS, 2 * CHUNK_CH, CHUNK_ST), _BF16),
                   jax.ShapeDtypeStruct((DEPTH, SSM_CHUNKS, 2 * CHUNK_CH, CHUNK_ST), _BF16),
                   jax.ShapeDtypeStruct((DEPTH, SSM_CHUNKS, CHUNK_CH, CHUNK_CH), _BF16)],
        name="ssm_prep",
    )(logdt, lre, lim, bre, bim, cre, cim)


def _mixer_kernel(x_ref, *refs, rb, tb, layer, zero_conv, single_step):
    refs = list(refs)
    conv0_hbm = None if zero_conv else refs.pop(0)
    sre0_ref, sim0_ref = refs.pop(0), refs.pop(0)
    prev = [refs.pop(0) for _ in range(3)] if layer > 0 else []
    (g_ref, w_in_ref, cw_ref, cb_ref, w_out_a_ref,
     abar_ref, b2blk_ref, c2_re_ref, c2_im_ref, fblk_ref, dskip_ref,
     glu_a_ref, glu_b_ref, w_o_ref,
     out_ref, conv_out_hbm, sre_stack_ref, sim_stack_ref,
     vbuf, s_re, s_im, conv_carry, conv_sem, *x_dma) = refs
    rows = rb * tb
    carry_rows = (CONV_WIDTH - 1) * rb
    sre_out_ref = sre_stack_ref.at[layer]
    sim_out_ref = sim_stack_ref.at[layer]

    @pl.when(pl.program_id(0) == 0)
    def _():
        for stack_ref, prev_ref in zip((sre_stack_ref, sim_stack_ref), prev[1:]):
            stack_ref[0:layer] = prev_ref[...]
        sre_out_ref[...] = sre0_ref[...]
        sim_out_ref[...] = sim0_ref[...]

    conv_io = (layer, conv_carry, conv_sem, single_step)
    _conv_state_fetch(conv0_hbm, *conv_io, wait=False)
    if not single_step:
        _conv_state_fetch(conv0_hbm, *conv_io, wait=True)
    if x_dma:
        _fetch_time_major(x_ref, (), *x_dma, tb)
        x = _read_time_major(x_dma[0])
    else:
        x = x_ref[...]
    xn = _rms(x, g_ref[layer:layer + 1, :]).astype(_BF16)

    def proj(lo, hi):
        return _dot(xn, w_in_ref[:, lo:hi])

    u = proj(1536, 2048)
    pair = 2 if tb % 2 == 0 else 1
    steps = tb // pair
    srows = steps * rb
    if pair == 2:
        u4 = u.reshape(steps, 2, rb, SSM_WIDTH)
        u_parts = [u4[:, i].reshape(srows, SSM_WIDTH) for i in range(2)]
    else:
        u_parts = [u]
    ub_parts = [v.astype(_BF16) for v in u_parts]
    s_re[0:rb, :] = sre_out_ref[...]
    s_im[0:rb, :] = sim_out_ref[...]
    for c in range(SSM_CHUNKS):
        ch = slice(c * CHUNK_CH, (c + 1) * CHUNK_CH)
        if pair == 2:
            inc = _dot(jnp.concatenate([ub_parts[0][:, ch], ub_parts[1][:, ch]], axis=1), b2blk_ref[c])
        else:
            inc = _dot(ub_parts[0][:, ch], b2blk_ref[c, CHUNK_CH:2 * CHUNK_CH, :])
        s_re[rb:rb + srows, c * CHUNK_ST:(c + 1) * CHUNK_ST] = inc[:, 0:CHUNK_ST]
        s_im[rb:rb + srows, c * CHUNK_ST:(c + 1) * CHUNK_ST] = inc[:, CHUNK_ST:2 * CHUNK_ST]

    a_row = 2 * (pair - 1)
    for c in range(SSM_CHUNKS):
        cols = slice(c * CHUNK_ST, (c + 1) * CHUNK_ST)
        ar = jnp.broadcast_to(abar_ref[a_row:a_row + 1, cols], (rb, CHUNK_ST))
        ai = jnp.broadcast_to(abar_ref[a_row + 1:a_row + 2, cols], (rb, CHUNK_ST))

        def step(k, carry, cols=cols, ar=ar, ai=ai):
            sr, si = carry
            r0 = pl.multiple_of(rb + k * rb, rb)
            nsr = ar * sr - ai * si + s_re[pl.ds(r0, rb), cols]
            nsi = ar * si + ai * sr + s_im[pl.ds(r0, rb), cols]
            s_re[pl.ds(r0, rb), cols] = nsr
            s_im[pl.ds(r0, rb), cols] = nsi
            return nsr, nsi

        sr, si = lax.fori_loop(0, steps, step, (s_re[0:rb, cols], s_im[0:rb, cols]), unroll=True)
        sre_out_ref[:, cols] = sr
        sim_out_ref[:, cols] = si

    h = proj(0, 512)
    gate_c = proj(1024, 1536)
    if single_step:
        _conv_state_fetch(conv0_hbm, *conv_io, wait=True)
    vbuf[0:rb, :] = conv_carry[0]
    vbuf[rb:carry_rows, :] = conv_carry[1]
    vbuf[carry_rows:carry_rows + rows, :] = gate_c * h
    conv_y = (cb_ref[layer:layer + 1, :]
              + vbuf[0:rows, :] * cw_ref[0:1, :]
              + vbuf[rb:rb + rows, :] * cw_ref[1:2, :]
              + vbuf[2 * rb:2 * rb + rows, :] * cw_ref[2:3, :])
    conv_carry[0] = vbuf[rows:rows + rb, :]
    conv_carry[1] = vbuf[rows + rb:rows + carry_rows, :]
    if single_step:
        _conv_state_store(conv_carry, conv_out_hbm, layer, conv_sem, single_step, wait=False)
    gate_b = proj(512, 1024)
    z_a = _dot((gate_b * conv_y).astype(_BF16), w_out_a_ref[...])
    merged = _sigmoid(proj(2048, 3072)) * z_a

    y_parts = [[] for _ in range(pair)]
    for c in range(SSM_CHUNKS):
        ch = slice(c * CHUNK_CH, (c + 1) * CHUNK_CH)
        cols = slice(c * CHUNK_ST, (c + 1) * CHUNK_ST)
        if pair == 2:
            y2 = (_dot_nt(s_re[:, cols].astype(_BF16), c2_re_ref[c])
                  + _dot_nt(s_im[:, cols].astype(_BF16), c2_im_ref[c]))
            y_parts[0].append(y2[0:srows, CHUNK_CH:2 * CHUNK_CH] + _dot(ub_parts[0][:, ch], fblk_ref[c]))
            y_parts[1].append(y2[rb:rb + srows, 0:CHUNK_CH])
        else:
            y_parts[0].append(
                _dot_nt(s_re[rb:rb + srows, cols].astype(_BF16), c2_re_ref[c, 0:CHUNK_CH, :])
                + _dot_nt(s_im[rb:rb + srows, cols].astype(_BF16), c2_im_ref[c, 0:CHUNK_CH, :]))
    d_skip = dskip_ref[layer:layer + 1, :]
    ssm_parts = [jnp.concatenate(yp, axis=1) + d_skip * up for yp, up in zip(y_parts, u_parts)]
    if pair == 2:
        ssm_y = jnp.stack([v.reshape(steps, rb, SSM_WIDTH) for v in ssm_parts],
                          axis=1).reshape(rows, SSM_WIDTH)
    else:
        ssm_y = ssm_parts[0]
    sy = _gelu(ssm_y).astype(_BF16)
    z_b = _dot(sy, glu_a_ref[...]) * _sigmoid(_dot(sy, glu_b_ref[...]))
    merged = (merged + _sigmoid(proj(3072, 4096)) * z_b).astype(_BF16)
    out_ref[...] = x + _dot(merged, w_o_ref[...])
    if not single_step:
        _conv_state_store(conv_carry, conv_out_hbm, layer, conv_sem, single_step, wait=False)
    _conv_state_store(conv_carry, conv_out_hbm, layer, conv_sem, single_step, wait=True)


def _ffn_kernel(x_ref, p_ref, *refs, rb, tb, layer, zero_conv, single_step, final_norm, p_native,
                out_native):
    refs = list(refs)
    f0_hbm = None if zero_conv else refs.pop(0)
    if layer > 0:
        refs.pop(0)
    (g2_ref, w_up_ref, fcw_ref, fcb_ref, w_down_ref,
     g3_ref, w_pg_ref, w_ple_ref, gf_ref,
     out_ref, fconv_out_hbm,
     fbuf, fcarry, fsem, *dma) = refs
    rows = rb * tb
    carry_rows = (CONV_WIDTH - 1) * rb
    p_dma = [dma.pop(0), dma.pop(0)] if p_native else None
    out_dma = [dma.pop(0), dma.pop(0)] if out_native else None

    conv_io = (layer, fcarry, fsem, single_step)
    _conv_state_fetch(f0_hbm, *conv_io, wait=False)
    if not single_step:
        _conv_state_fetch(f0_hbm, *conv_io, wait=True)
    if p_native:
        _fetch_time_major(p_ref, (layer,), *p_dma, tb)

    x = x_ref[...]
    xn = _rms(x, g2_ref[layer:layer + 1, :]).astype(_BF16)
    acc = x
    if single_step:
        _conv_state_fetch(f0_hbm, *conv_io, wait=True)
    for lo, hi in FF_CHUNKS:
        wc = hi - lo
        up_a = _dot(xn, w_up_ref[:, lo:hi])
        up_b = _dot(xn, w_up_ref[:, D_FF + lo:D_FF + hi])
        fbuf[0:rb, 0:wc] = fcarry[0, :, lo:hi]
        fbuf[rb:carry_rows, 0:wc] = fcarry[1, :, lo:hi]
        fbuf[carry_rows:carry_rows + rows, 0:wc] = up_a
        conv_a = (fcb_ref[layer:layer + 1, lo:hi]
                  + fbuf[0:rows, 0:wc] * fcw_ref[0:1, lo:hi]
                  + fbuf[rb:rb + rows, 0:wc] * fcw_ref[1:2, lo:hi]
                  + fbuf[2 * rb:2 * rb + rows, 0:wc] * fcw_ref[2:3, lo:hi])
        fcarry[0, :, lo:hi] = fbuf[rows:rows + rb, 0:wc]
        fcarry[1, :, lo:hi] = fbuf[rows + rb:rows + carry_rows, 0:wc]
        hid = (_gelu(conv_a) * up_b).astype(_BF16)
        acc = acc + _dot(hid, w_down_ref[lo:hi, :])
    if single_step:
        _conv_state_store(fcarry, fconv_out_hbm, layer, fsem, single_step, wait=False)
    x2 = acc
    p = _read_time_major(p_dma[0]) if p_native else p_ref[...]
    pe = _dot(p.astype(_BF16), w_ple_ref[...])
    xn3 = _rms(x2, g3_ref[layer:layer + 1, :]).astype(_BF16)
    gate = _sigmoid(_dot(xn3, w_pg_ref[...]))
    x3 = x2 + gate * pe
    if final_norm:
        x3 = _rms(x3, gf_ref[...])
    if out_native:
        _store_batch_major(x3, out_ref, *out_dma, tb)
    else:
        out_ref[...] = x3
    if not single_step:
        _conv_state_store(fcarry, fconv_out_hbm, layer, fsem, single_step, wait=False)
    _conv_state_store(fcarry, fconv_out_hbm, layer, fsem, single_step, wait=True)


def _const_spec(shape, layer=None):
    if layer is None:
        return pl.BlockSpec(shape, lambda j: (0,) * len(shape), pipeline_mode=pl.Buffered(1))
    return pl.BlockSpec((None,) + shape, lambda j: (layer,) + (0,) * len(shape),
                        pipeline_mode=pl.Buffered(1))


def _native_scratch(tb, rb, width):
    return [pltpu.VMEM((2, tb, rb, width), _F32), pltpu.SemaphoreType.DMA((2,))]


def _stack_spec(n_layers, rb, width):
    return pl.BlockSpec((n_layers, rb, width), lambda j: (0, 0, 0))


def _conv_state_scratch(rb, width):
    return [pltpu.VMEM((CONV_WIDTH - 1, rb, width), _F32), pltpu.SemaphoreType.DMA((DEPTH, CONV_WIDTH - 1))]


def _mixer_call(x, conv0, ssm0, prev, wts, layer, ssm_layer, rb, tb, x_native):
    n = x.shape[0] * x.shape[1] if x_native else x.shape[0]
    rows = rb * tb
    carry_rows = (CONV_WIDTH - 1) * rb
    hbm_spec = pl.BlockSpec(memory_space=pl.ANY)
    row_spec = pl.BlockSpec((rows, D_MODEL), lambda j: (j, 0))
    operands = [x]
    in_specs = [hbm_spec if x_native else row_spec]
    if conv0 is not None:
        operands.append(conv0)
        in_specs.append(hbm_spec)
    operands += list(ssm0)
    in_specs += [_const_spec((rb, SSM_FLAT), ssm_layer)] * 2
    aliases = {}
    if prev:
        aliases[len(operands)] = 1
        operands += list(prev)
        in_specs += [hbm_spec] + [_const_spec((layer, rb, SSM_FLAT))] * 2
    in_specs += [
        _const_spec((DEPTH, D_MODEL)),
        _const_spec((D_MODEL, IN_PROJ_WIDTH), layer),
        _const_spec((CONV_WIDTH, CONV_A_WIDTH), layer),
        _const_spec((DEPTH, CONV_A_WIDTH)),
        _const_spec((CONV_A_WIDTH, D_MODEL), layer),
        _const_spec((4, SSM_FLAT), layer),
        _const_spec((SSM_CHUNKS, 2 * CHUNK_CH, 2 * CHUNK_ST), layer),
        _const_spec((SSM_CHUNKS, 2 * CHUNK_CH, CHUNK_ST), layer),
        _const_spec((SSM_CHUNKS, 2 * CHUNK_CH, CHUNK_ST), layer),
        _const_spec((SSM_CHUNKS, CHUNK_CH, CHUNK_CH), layer),
        _const_spec((DEPTH, SSM_WIDTH)),
        _const_spec((SSM_WIDTH, D_MODEL), layer),
        _const_spec((SSM_WIDTH, D_MODEL), layer),
        _const_spec((D_MODEL, D_MODEL), layer),
    ]
    out_specs = [row_spec, hbm_spec] + [_stack_spec(layer + 1, rb, SSM_FLAT)] * 2
    out_shape = ([jax.ShapeDtypeStruct((n, D_MODEL), _F32),
                  jax.ShapeDtypeStruct((DEPTH, rb, CONV_WIDTH - 1, CONV_A_WIDTH), _F32)]
                 + [jax.ShapeDtypeStruct((layer + 1, rb, SSM_FLAT), _F32)] * 2)
    s_rows = rows // 2 if tb % 2 == 0 else rows
    scratch = [
        pltpu.VMEM((carry_rows + rows, CONV_A_WIDTH), _F32),
        pltpu.VMEM((rb + s_rows, SSM_FLAT), _F32),
        pltpu.VMEM((rb + s_rows, SSM_FLAT), _F32),
    ]
    scratch += _conv_state_scratch(rb, CONV_A_WIDTH)
    if x_native:
        scratch += _native_scratch(tb, rb, D_MODEL)
    return pl.pallas_call(
        functools.partial(_mixer_kernel, rb=rb, tb=tb, layer=layer, zero_conv=conv0 is None,
                          single_step=n == rows),
        grid=(n // rows,),
        in_specs=in_specs,
        out_specs=out_specs,
        out_shape=out_shape,
        scratch_shapes=scratch,
        input_output_aliases=aliases,
        compiler_params=pltpu.CompilerParams(
            dimension_semantics=("arbitrary",), vmem_limit_bytes=VMEM_LIMIT_BYTES),
        name=f"mixer_l{layer}_rb{rb}",
    )(*operands,
      wts["norm_mix"], wts["w_in"], wts["conv_a_w"], wts["conv_a_b"], wts["w_out_a"],
      wts["abar"], wts["b2blk"], wts["c2_re"], wts["c2_im"], wts["fblk"], wts["d_skip"],
      wts["w_glu_a"], wts["w_glu_b"], wts["w_o"])


def _ffn_call(x, p, f0, prev, wts, layer, rb, tb, final_norm, p_native, out_native):
    n = x.shape[0]
    rows = rb * tb
    carry_rows = (CONV_WIDTH - 1) * rb
    hbm_spec = pl.BlockSpec(memory_space=pl.ANY)
    row_spec = pl.BlockSpec((rows, D_MODEL), lambda j: (j, 0))
    operands = [x, p]
    in_specs = [
        row_spec,
        hbm_spec if p_native else pl.BlockSpec((None, rows, PLE_DIM), lambda j: (layer, j, 0)),
    ]
    if f0 is not None:
        operands.append(f0)
        in_specs.append(hbm_spec)
    aliases = {}
    if prev is not None:
        aliases[len(operands)] = 1
        operands.append(prev)
        in_specs.append(hbm_spec)
    in_specs += [
        _const_spec((DEPTH, D_MODEL)),
        _const_spec((D_MODEL, 2 * D_FF), layer),
        _const_spec((CONV_WIDTH, D_FF), layer),
        _const_spec((DEPTH, D_FF)),
        _const_spec((D_FF, D_MODEL), layer),
        _const_spec((DEPTH, D_MODEL)),
        _const_spec((D_MODEL, D_MODEL), layer),
        _const_spec((PLE_DIM, D_MODEL), layer),
        _const_spec((1, D_MODEL)),
    ]
    out_specs = [hbm_spec if out_native else row_spec, hbm_spec]
    out_shape = [jax.ShapeDtypeStruct((rb, n // rb, D_MODEL) if out_native else (n, D_MODEL), _F32),
                 jax.ShapeDtypeStruct((DEPTH, rb, CONV_WIDTH - 1, D_FF), _F32)]
    scratch = [pltpu.VMEM((carry_rows + rows, FF_CHUNK_MAX), _F32)]
    scratch += _conv_state_scratch(rb, D_FF)
    if p_native:
        scratch += _native_scratch(tb, rb, PLE_DIM)
    if out_native:
        scratch += _native_scratch(tb, rb, D_MODEL)
    return pl.pallas_call(
        functools.partial(_ffn_kernel, rb=rb, tb=tb, layer=layer, zero_conv=f0 is None,
                          single_step=n == rows,
                          final_norm=final_norm, p_native=p_native, out_native=out_native),
        grid=(n // rows,),
        in_specs=in_specs,
        out_specs=out_specs,
        out_shape=out_shape,
        scratch_shapes=scratch,
        input_output_aliases=aliases,
        compiler_params=pltpu.CompilerParams(
            dimension_semantics=("arbitrary",), vmem_limit_bytes=VMEM_LIMIT_BYTES),
        name=f"ffn_l{layer}_rb{rb}",
    )(*operands,
      wts["norm_ffn"], wts["w_up"], wts["ffn_conv_w"], wts["ffn_conv_b"], wts["w_down"],
      wts["norm_ple"], wts["w_ple_gate"], wts["w_ple"], wts["norm_final"])


def _run_trunk(x, p, conv0, sre0, sim0, f0, wts, rb, tb_mixer, tb_ffn, native):
    mixer_states, ffn_state = [], None
    for layer in range(DEPTH):
        last = layer == DEPTH - 1
        ssm_layer = 0 if conv0 is None else layer
        x, *mixer_states = _mixer_call(x, conv0, (sre0, sim0), mixer_states, wts, layer, ssm_layer,
                                       rb, tb_mixer, x_native=native and layer == 0)
        x, ffn_state = _ffn_call(x, p, f0, ffn_state, wts, layer, rb, tb_ffn,
                                 final_norm=last, p_native=native, out_native=native and last)
    return (x, *mixer_states, ffn_state)


def kernel(x_prompt, x_sample, p_prompt, p_sample, state_conv_a, state_ssm_re, state_ssm_im, state_ffn_conv, norm_mix, w_in, conv_a_w, conv_a_b, w_out_a, log_dt, lam_re, lam_im, b_re, b_im, c_re, c_im, d_skip, w_glu_a, w_glu_b, w_o, norm_ffn, w_up, ffn_conv_w, ffn_conv_b, w_down, norm_ple, w_ple_gate, w_ple, norm_final):
    abar, b2blk, c2_re, c2_im, fblk = _ssm_prep(log_dt, lam_re, lam_im, b_re, b_im, c_re, c_im)
    wts = {
        "norm_mix": norm_mix,
        "w_in": w_in.astype(_BF16),
        "conv_a_w": conv_a_w,
        "conv_a_b": conv_a_b,
        "w_out_a": w_out_a.astype(_BF16),
        "abar": abar, "b2blk": b2blk, "c2_re": c2_re, "c2_im": c2_im, "fblk": fblk,
        "d_skip": d_skip,
        "w_glu_a": w_glu_a.astype(_BF16),
        "w_glu_b": w_glu_b.astype(_BF16),
        "w_o": w_o.astype(_BF16),
        "norm_ffn": norm_ffn,
        "w_up": w_up.astype(_BF16),
        "ffn_conv_w": ffn_conv_w,
        "ffn_conv_b": ffn_conv_b,
        "w_down": w_down.astype(_BF16),
        "norm_ple": norm_ple,
        "w_ple_gate": w_ple_gate.astype(_BF16),
        "w_ple": w_ple.astype(_BF16),
        "norm_final": norm_final.reshape(1, D_MODEL),
    }

    bp = x_prompt.shape[0]
    zs = jnp.zeros((1, bp, SSM_FLAT), _F32)
    y_prompt, conv_p, sre_p, sim_p, ffn_p = _run_trunk(
        x_prompt, p_prompt, None, zs, zs, None, wts, bp, MIXER_TIME_BLOCK, FFN_TIME_BLOCK,
        native=True)

    bs, sseq, _ = x_sample.shape
    xs = x_sample.transpose(1, 0, 2).reshape(sseq * bs, D_MODEL)
    ps = p_sample.transpose(0, 2, 1, 3).reshape(DEPTH, sseq * bs, PLE_DIM)
    ys, conv_s, sre_s, sim_s, ffn_s = _run_trunk(
        xs, ps, state_conv_a,
        state_ssm_re.reshape(DEPTH, bs, SSM_FLAT), state_ssm_im.reshape(DEPTH, bs, SSM_FLAT),
        state_ffn_conv, wts, bs, sseq, sseq, native=False)
    y_sample = ys.reshape(sseq, bs, D_MODEL).transpose(1, 0, 2)

    def ssm_state(s, b):
        return s.reshape(DEPTH, b, SSM_GROUPS, SSM_STATE)

    return (y_prompt, y_sample,
            conv_p, ssm_state(sre_p, bp), ssm_state(sim_p, bp), ffn_p,
            conv_s, ssm_state(sre_s, bs), ssm_state(sim_s, bs), ffn_s)
```

```python
import functools
import math

import jax
import jax.numpy as jnp
from jax import lax
from jax.experimental import pallas as pl
from jax.experimental.pallas import tpu as pltpu

D_MODEL = 1024
DEPTH = 2
CONV_WIDTH = 3
CONV_A_WIDTH = 512
SSM_WIDTH = 512
SSM_GROUP = 16
SSM_GROUPS = 32
SSM_STATE = 64
SSM_FLAT = SSM_GROUPS * SSM_STATE
D_FF = 2816
PLE_DIM = 256
NORM_EPS = 1e-6
IN_PROJ_WIDTH = 4096

SSM_CHUNKS = 4
CHUNK_CH = SSM_WIDTH // SSM_CHUNKS
CHUNK_ST = SSM_FLAT // SSM_CHUNKS

FF_CHUNKS = ((0, 1024), (1024, 2048), (2048, 2816))
FF_CHUNK_MAX = 1024

MIXER_TIME_BLOCK = 128
FFN_TIME_BLOCK = 128
VMEM_LIMIT_BYTES = 56 * 1024 * 1024

_BF16 = jnp.bfloat16
_F32 = jnp.float32


def _dot(a, b):
    return jnp.dot(a, b, preferred_element_type=_F32)


def _dot_nt(a, b):
    return lax.dot_general(a, b, (((1,), (1,)), ((), ())), preferred_element_type=_F32)


def _dot_nt_split(a, b):
    a_hi = a.astype(_BF16)
    b_hi = b.astype(_BF16)
    a_lo = (a - a_hi.astype(_F32)).astype(_BF16)
    b_lo = (b - b_hi.astype(_F32)).astype(_BF16)
    return _dot_nt(a_hi, b_hi) + _dot_nt(a_hi, b_lo) + _dot_nt(a_lo, b_hi)


def _rms(x, g):
    ms = jnp.mean(x * x, axis=-1, keepdims=True)
    return x * lax.rsqrt(ms + NORM_EPS) * g


def _gelu(x):
    c = math.sqrt(2.0 / math.pi)
    t = jnp.tanh(x * (c + (0.044715 * c) * (x * x)))
    return x * (0.5 + 0.5 * t)


def _sigmoid(x):
    return 0.5 * jnp.tanh(0.5 * x) + 0.5


def _seq_copies(hbm_ref, lead, buf, sem, step, slot, tb, to_hbm):
    copies = []
    for b in range(buf.shape[2]):
        hbm_view = hbm_ref.at[lead + (b, pl.ds(step * tb, tb))]
        vmem_view = buf.at[slot, :, b]
        src, dst = (vmem_view, hbm_view) if to_hbm else (hbm_view, vmem_view)
        copies.append(pltpu.make_async_copy(src, dst, sem.at[slot]))
    return copies


def _fetch_time_major(hbm_ref, lead, buf, sem, tb):
    j = pl.program_id(0)
    slot = j % 2

    @pl.when(j == 0)
    def _():
        for c in _seq_copies(hbm_ref, lead, buf, sem, 0, 0, tb, False):
            c.start()

    @pl.when(j + 1 < pl.num_programs(0))
    def _():
        for c in _seq_copies(hbm_ref, lead, buf, sem, j + 1, 1 - slot, tb, False):
            c.start()

    for c in _seq_copies(hbm_ref, lead, buf, sem, j, slot, tb, False):
        c.wait()


def _read_time_major(buf):
    tb, nb, width = buf.shape[1:]
    return buf[pl.program_id(0) % 2].reshape(tb * nb, width)


def _store_batch_major(val, hbm_ref, buf, sem, tb):
    j = pl.program_id(0)
    slot = j % 2
    buf[slot] = val.reshape(tb, buf.shape[2], buf.shape[3])
    for c in _seq_copies(hbm_ref, (), buf, sem, j, slot, tb, True):
        c.start()

    @pl.when(j >= 1)
    def _():
        for c in _seq_copies(hbm_ref, (), buf, sem, j - 1, 1 - slot, tb, True):
            c.wait()

    @pl.when(j == pl.num_programs(0) - 1)
    def _():
        for c in _seq_copies(hbm_ref, (), buf, sem, j, slot, tb, True):
            c.wait()


def _conv_state_copies(hbm_ref, slot, carry, sem, to_hbm):
    copies = []
    for k in range(CONV_WIDTH - 1):
        hbm_view = hbm_ref.at[slot, :, k, :]
        src, dst = (carry.at[k], hbm_view) if to_hbm else (hbm_view, carry.at[k])
        copies.append(pltpu.make_async_copy(src, dst, sem.at[slot, k]))
    return copies


def _at_step(first, single_step, body):
    if single_step:
        body()
    else:
        step = 0 if first else pl.num_programs(0) - 1
        pl.when(pl.program_id(0) == step)(body)


def _conv_state_fetch(state_hbm, layer, carry, sem, single_step, wait):
    def body():
        if state_hbm is None:
            if not wait:
                carry[...] = jnp.zeros(carry.shape, carry.dtype)
        else:
            for c in _conv_state_copies(state_hbm, layer, carry, sem, False):
                c.wait() if wait else c.start()

    _at_step(True, single_step, body)


def _conv_state_store(carry, out_hbm, layer, sem, single_step, wait):
    def body():
        for slot in range(layer, out_hbm.shape[0]):
            for c in _conv_state_copies(out_hbm, slot, carry, sem, True):
                c.wait() if wait else c.start()

    _at_step(False, single_step, body)


def _ssm_prep_kernel(logdt_ref, lre_ref, lim_ref, bre_ref, bim_ref, cre_ref, cim_ref,
                     abar_ref, b2blk_ref, c2_re_ref, c2_im_ref, fblk_ref):
    dt = jnp.exp(logdt_ref[...])
    lre = lre_ref[...]
    lim = lim_ref[...]
    mag = jnp.exp(lre * dt)
    ang = lim * dt
    ar = mag * jnp.cos(ang)
    ai = mag * jnp.sin(ang)
    abar_ref[0:1, :] = ar
    abar_ref[1:2, :] = ai
    abar_ref[2:3, :] = ar * ar - ai * ai
    abar_ref[3:4, :] = 2.0 * (ar * ai)
    den = lre * lre + lim * lim
    nr = ar - 1.0
    coef_re = (nr * lre + ai * lim) / den
    coef_im = (ai * lre - nr * lim) / den
    bre = bre_ref[...]
    bim = bim_ref[...]
    bbar_re = coef_re * bre - coef_im * bim
    bbar_im = coef_re * bim + coef_im * bre
    abb_re = ar * bbar_re - ai * bbar_im
    abb_im = ar * bbar_im + ai * bbar_re
    row = lax.broadcasted_iota(jnp.int32, (CHUNK_CH, CHUNK_ST), 0)
    lane = lax.broadcasted_iota(jnp.int32, (CHUNK_CH, CHUNK_ST), 1)
    same_group = (row // SSM_GROUP) == (lane // SSM_STATE)
    reps = CHUNK_CH // SSM_GROUP
    for c in range(SSM_CHUNKS):
        cols = slice(c * CHUNK_ST, (c + 1) * CHUNK_ST)
        rows = slice(c * CHUNK_CH, (c + 1) * CHUNK_CH)

        def b_block(v, cols=cols):
            return jnp.where(same_group, jnp.concatenate([v[:, cols]] * reps, axis=0), 0.0)

        b_re = b_block(bbar_re)
        b_im = b_block(bbar_im)
        b2blk_ref[c, 0:CHUNK_CH, 0:CHUNK_ST] = b_block(abb_re).astype(_BF16)
        b2blk_ref[c, 0:CHUNK_CH, CHUNK_ST:2 * CHUNK_ST] = b_block(abb_im).astype(_BF16)
        b2blk_ref[c, CHUNK_CH:2 * CHUNK_CH, 0:CHUNK_ST] = b_re.astype(_BF16)
        b2blk_ref[c, CHUNK_CH:2 * CHUNK_CH, CHUNK_ST:2 * CHUNK_ST] = b_im.astype(_BF16)
        c_re = jnp.where(same_group, cre_ref[rows, :], 0.0)
        c_im = jnp.where(same_group, cim_ref[rows, :], 0.0)
        arc = ar[:, cols]
        aic = ai[:, cols]
        c2_re_ref[c, 0:CHUNK_CH, :] = c_re.astype(_BF16)
        c2_re_ref[c, CHUNK_CH:2 * CHUNK_CH, :] = (c_re * arc - c_im * aic).astype(_BF16)
        c2_im_ref[c, 0:CHUNK_CH, :] = (-c_im).astype(_BF16)
        c2_im_ref[c, CHUNK_CH:2 * CHUNK_CH, :] = (-(c_re * aic + c_im * arc)).astype(_BF16)
        feed = _dot_nt_split(b_re, c_re) - _dot_nt_split(b_im, c_im)
        fblk_ref[c] = feed.astype(_BF16)


def _ssm_prep(log_dt, lam_re, lam_im, b_re, b_im, c_re, c_im):
    logdt = jnp.repeat(log_dt, SSM_STATE, axis=1).reshape(DEPTH, 1, SSM_FLAT)
    lre = lam_re.reshape(DEPTH, 1, SSM_FLAT)
    lim = lam_im.reshape(DEPTH, 1, SSM_FLAT)
    bre = b_re.transpose(0, 3, 1, 2).reshape(DEPTH, SSM_GROUP, SSM_FLAT)
    bim = b_im.transpose(0, 3, 1, 2).reshape(DEPTH, SSM_GROUP, SSM_FLAT)
    reps = CHUNK_ST // SSM_STATE
    cre = jnp.tile(c_re.reshape(DEPTH, SSM_WIDTH, SSM_STATE), (1, 1, reps))
    cim = jnp.tile(c_im.reshape(DEPTH, SSM_WIDTH, SSM_STATE), (1, 1, reps))

    def spec(*shape):
        return pl.BlockSpec((None,) + shape, lambda i: (i,) + (0,) * len(shape))

    return pl.pallas_call(
        _ssm_prep_kernel,
        grid=(DEPTH,),
        in_specs=[spec(1, SSM_FLAT), spec(1, SSM_FLAT), spec(1, SSM_FLAT),
                  spec(SSM_GROUP, SSM_FLAT), spec(SSM_GROUP, SSM_FLAT),
                  spec(SSM_WIDTH, CHUNK_ST), spec(SSM_WIDTH, CHUNK_ST)],
        out_specs=[spec(4, SSM_FLAT),
                   spec(SSM_CHUNKS, 2 * CHUNK_CH, 2 * CHUNK_ST),
                   spec(SSM_CHUNKS, 2 * CHUNK_CH, CHUNK_ST),
                   spec(SSM_CHUNKS, 2 * CHUNK_CH, CHUNK_ST),
                   spec(SSM_CHUNKS, CHUNK_CH, CHUNK_CH)],
        out_shape=[jax.ShapeDtypeStruct((DEPTH, 4, SSM_FLAT), _F32),
                   jax.ShapeDtypeStruct((DEPTH, SSM_CHUNKS, 2 * CHUNK_CH, 2 * CHUNK_ST), _BF16),
                   jax.ShapeDtypeStruct((DEPTH, SSM_CHUNKS, 2 * CHUNK_CH, CHUNK_ST), _BF16),
                   jax.ShapeDtypeStruct((DEPTH, SSM_CHUNKS, 2 * CHUNK_CH, CHUNK_ST), _BF16),
                   jax.ShapeDtypeStruct((DEPTH, SSM_CHUNKS, CHUNK_CH, CHUNK_CH), _BF16)],
        name="ssm_prep",
    )(logdt, lre, lim, bre, bim, cre, cim)


def _mixer_kernel(x_ref, *refs, rb, tb, layer, zero_conv, single_step, n_after):
    refs = list(refs)
    conv0_hbm = None if zero_conv else refs.pop(0)
    sre0_ref, sim0_ref = refs.pop(0), refs.pop(0)
    prev = [refs.pop(0) for _ in range(3)] if layer > 0 else []
    del refs[:n_after]
    (g_ref, w_in_ref, cw_ref, cb_ref, w_out_a_ref,
     abar_ref, b2blk_ref, c2_re_ref, c2_im_ref, fblk_ref, dskip_ref,
     glu_a_ref, glu_b_ref, w_o_ref,
     out_ref, conv_out_hbm, sre_stack_ref, sim_stack_ref,
     vbuf, s_re, s_im, conv_carry, conv_sem, *x_dma) = refs
    rows = rb * tb
    carry_rows = (CONV_WIDTH - 1) * rb
    sre_out_ref = sre_stack_ref.at[layer]
    sim_out_ref = sim_stack_ref.at[layer]

    @pl.when(pl.program_id(0) == 0)
    def _():
        for stack_ref, prev_ref in zip((sre_stack_ref, sim_stack_ref), prev[1:]):
            stack_ref[0:layer] = prev_ref[...]
        sre_out_ref[...] = sre0_ref[...]
        sim_out_ref[...] = sim0_ref[...]

    conv_io = (layer, conv_carry, conv_sem, single_step)
    _conv_state_fetch(conv0_hbm, *conv_io, wait=False)
    if not single_step:
        _conv_state_fetch(conv0_hbm, *conv_io, wait=True)
    if x_dma:
        _fetch_time_major(x_ref, (), *x_dma, tb)
        x = _read_time_major(x_dma[0])
    else:
        x = x_ref[...]
    xn = _rms(x, g_ref[layer:layer + 1, :]).astype(_BF16)

    def proj(lo, hi):
        return _dot(xn, w_in_ref[:, lo:hi])

    u = proj(1536, 2048)
    pair = 2 if tb % 2 == 0 else 1
    steps = tb // pair
    srows = steps * rb
    if pair == 2:
        u4 = u.reshape(steps, 2, rb, SSM_WIDTH)
        u_parts = [u4[:, i].reshape(srows, SSM_WIDTH) for i in range(2)]
    else:
        u_parts = [u]
    ub_parts = [v.astype(_BF16) for v in u_parts]
    s_re[0:rb, :] = sre_out_ref[...]
    s_im[0:rb, :] = sim_out_ref[...]
    for c in range(SSM_CHUNKS):
        ch = slice(c * CHUNK_CH, (c + 1) * CHUNK_CH)
        if pair == 2:
            inc = _dot(jnp.concatenate([ub_parts[0][:, ch], ub_parts[1][:, ch]], axis=1), b2blk_ref[c])
        else:
            inc = _dot(ub_parts[0][:, ch], b2blk_ref[c, CHUNK_CH:2 * CHUNK_CH, :])
        s_re[rb:rb + srows, c * CHUNK_ST:(c + 1) * CHUNK_ST] = inc[:, 0:CHUNK_ST]
        s_im[rb:rb + srows, c * CHUNK_ST:(c + 1) * CHUNK_ST] = inc[:, CHUNK_ST:2 * CHUNK_ST]

    a_row = 2 * (pair - 1)
    for c in range(SSM_CHUNKS):
        cols = slice(c * CHUNK_ST, (c + 1) * CHUNK_ST)
        ar = jnp.broadcast_to(abar_ref[a_row:a_row + 1, cols], (rb, CHUNK_ST))
        ai = jnp.broadcast_to(abar_ref[a_row + 1:a_row + 2, cols], (rb, CHUNK_ST))

        def step(k, carry, cols=cols, ar=ar, ai=ai):
            sr, si = carry
            r0 = pl.multiple_of(rb + k * rb, rb)
            nsr = ar * sr - ai * si + s_re[pl.ds(r0, rb), cols]
            nsi = ar * si + ai * sr + s_im[pl.ds(r0, rb), cols]
            s_re[pl.ds(r0, rb), cols] = nsr
            s_im[pl.ds(r0, rb), cols] = nsi
            return nsr, nsi

        sr, si = lax.fori_loop(0, steps, step, (s_re[0:rb, cols], s_im[0:rb, cols]), unroll=True)
        sre_out_ref[:, cols] = sr
        sim_out_ref[:, cols] = si

    h = proj(0, 512)
    gate_c = proj(1024, 1536)
    if single_step:
        _conv_state_fetch(conv0_hbm, *conv_io, wait=True)
    vbuf[0:rb, :] = conv_carry[0]
    vbuf[rb:carry_rows, :] = conv_carry[1]
    vbuf[carry_rows:carry_rows + rows, :] = gate_c * h
    conv_y = (cb_ref[layer:layer + 1, :]
              + vbuf[0:rows, :] * cw_ref[0:1, :]
              + vbuf[rb:rb + rows, :] * cw_ref[1:2, :]
              + vbuf[2 * rb:2 * rb + rows, :] * cw_ref[2:3, :])
    conv_carry[0] = vbuf[rows:rows + rb, :]
    conv_carry[1] = vbuf[rows + rb:rows + carry_rows, :]
    if single_step:
        _conv_state_store(conv_carry, conv_out_hbm, layer, conv_sem, single_step, wait=False)
    gate_b = proj(512, 1024)
    z_a = _dot((gate_b * conv_y).astype(_BF16), w_out_a_ref[...])
    merged = _sigmoid(proj(2048, 3072)) * z_a

    y_parts = [[] for _ in range(pair)]
    for c in range(SSM_CHUNKS):
        ch = slice(c * CHUNK_CH, (c + 1) * CHUNK_CH)
        cols = slice(c * CHUNK_ST, (c + 1) * CHUNK_ST)
        if pair == 2:
            y2 = (_dot_nt(s_re[:, cols].astype(_BF16), c2_re_ref[c])
                  + _dot_nt(s_im[:, cols].astype(_BF16), c2_im_ref[c]))
            y_parts[0].append(y2[0:srows, CHUNK_CH:2 * CHUNK_CH] + _dot(ub_parts[0][:, ch], fblk_ref[c]))
            y_parts[1].append(y2[rb:rb + srows, 0:CHUNK_CH])
        else:
            y_parts[0].append(
                _dot_nt(s_re[rb:rb + srows, cols].astype(_BF16), c2_re_ref[c, 0:CHUNK_CH, :])
                + _dot_nt(s_im[rb:rb + srows, cols].astype(_BF16), c2_im_ref[c, 0:CHUNK_CH, :]))
    d_skip = dskip_ref[layer:layer + 1, :]
    ssm_parts = [jnp.concatenate(yp, axis=1) + d_skip * up for yp, up in zip(y_parts, u_parts)]
    if pair == 2:
        ssm_y = jnp.stack([v.reshape(steps, rb, SSM_WIDTH) for v in ssm_parts],
                          axis=1).reshape(rows, SSM_WIDTH)
    else:
        ssm_y = ssm_parts[0]
    sy = _gelu(ssm_y).astype(_BF16)
    z_b = _dot(sy, glu_a_ref[...]) * _sigmoid(_dot(sy, glu_b_ref[...]))
    merged = (merged + _sigmoid(proj(3072, 4096)) * z_b).astype(_BF16)
    out_ref[...] = x + _dot(merged, w_o_ref[...])
    if not single_step:
        _conv_state_store(conv_carry, conv_out_hbm, layer, conv_sem, single_step, wait=False)
    _conv_state_store(conv_carry, conv_out_hbm, layer, conv_sem, single_step, wait=True)


def _ffn_kernel(x_ref, p_ref, *refs, rb, tb, layer, zero_conv, single_step, final_norm, p_native,
                out_native):
    refs = list(refs)
    f0_hbm = None if zero_conv else refs.pop(0)
    if layer > 0:
        refs.pop(0)
    (g2_ref, w_up_ref, fcw_ref, fcb_ref, w_down_ref,
     g3_ref, w_pg_ref, w_ple_ref, gf_ref,
     out_ref, fconv_out_hbm,
     fbuf, fcarry, fsem, *dma) = refs
    rows = rb * tb
    carry_rows = (CONV_WIDTH - 1) * rb
    p_dma = [dma.pop(0), dma.pop(0)] if p_native else None
    out_dma = [dma.pop(0), dma.pop(0)] if out_native else None

    conv_io = (layer, fcarry, fsem, single_step)
    _conv_state_fetch(f0_hbm, *conv_io, wait=False)
    if not single_step:
        _conv_state_fetch(f0_hbm, *conv_io, wait=True)
    if p_native:
        _fetch_time_major(p_ref, (layer,), *p_dma, tb)

    x = x_ref[...]
    xn = _rms(x, g2_ref[layer:layer + 1, :]).astype(_BF16)
    acc = x
    if single_step:
        _conv_state_fetch(f0_hbm, *conv_io, wait=True)
    for lo, hi in FF_CHUNKS:
        wc = hi - lo
        up_a = _dot(xn, w_up_ref[:, lo:hi])
        up_b = _dot(xn, w_up_ref[:, D_FF + lo:D_FF + hi])
        fbuf[0:rb, 0:wc] = fcarry[0, :, lo:hi]
        fbuf[rb:carry_rows, 0:wc] = fcarry[1, :, lo:hi]
        fbuf[carry_rows:carry_rows + rows, 0:wc] = up_a
        conv_a = (fcb_ref[layer:layer + 1, lo:hi]
                  + fbuf[0:rows, 0:wc] * fcw_ref[0:1, lo:hi]
                  + fbuf[rb:rb + rows, 0:wc] * fcw_ref[1:2, lo:hi]
                  + fbuf[2 * rb:2 * rb + rows, 0:wc] * fcw_ref[2:3, lo:hi])
        fcarry[0, :, lo:hi] = fbuf[rows:rows + rb, 0:wc]
        fcarry[1, :, lo:hi] = fbuf[rows + rb:rows + carry_rows, 0:wc]
        hid = (_gelu(conv_a) * up_b).astype(_BF16)
        acc = acc + _dot(hid, w_down_ref[lo:hi, :])
    if single_step:
        _conv_state_store(fcarry, fconv_out_hbm, layer, fsem, single_step, wait=False)
    x2 = acc
    xn3 = _rms(x2, g3_ref[layer:layer + 1, :]).astype(_BF16)
    gate = _sigmoid(_dot(xn3, w_pg_ref[...]))
    p = _read_time_major(p_dma[0]) if p_native else p_ref[...]
    pe = _dot(p.astype(_BF16), w_ple_ref[...])
    x3 = x2 + gate * pe
    if final_norm:
        x3 = _rms(x3, gf_ref[...])
    if out_native:
        _store_batch_major(x3, out_ref, *out_dma, tb)
    else:
        out_ref[...] = x3
    if not single_step:
        _conv_state_store(fcarry, fconv_out_hbm, layer, fsem, single_step, wait=False)
    _conv_state_store(fcarry, fconv_out_hbm, layer, fsem, single_step, wait=True)


def _const_spec(shape, layer=None):
    if layer is None:
        return pl.BlockSpec(shape, lambda j: (0,) * len(shape), pipeline_mode=pl.Buffered(1))
    return pl.BlockSpec((None,) + shape, lambda j: (layer,) + (0,) * len(shape),
                        pipeline_mode=pl.Buffered(1))


def _native_scratch(tb, rb, width):
    return [pltpu.VMEM((2, tb, rb, width), _F32), pltpu.SemaphoreType.DMA((2,))]


def _stack_spec(n_layers, rb, width):
    return pl.BlockSpec((n_layers, rb, width), lambda j: (0, 0, 0))


def _conv_state_scratch(rb, width):
    return [pltpu.VMEM((CONV_WIDTH - 1, rb, width), _F32), pltpu.SemaphoreType.DMA((DEPTH, CONV_WIDTH - 1))]


def _mixer_call(x, conv0, ssm0, prev, wts, layer, ssm_layer, rb, tb, x_native, after=()):
    n = x.shape[0] * x.shape[1] if x_native else x.shape[0]
    rows = rb * tb
    carry_rows = (CONV_WIDTH - 1) * rb
    hbm_spec = pl.BlockSpec(memory_space=pl.ANY)
    row_spec = pl.BlockSpec((rows, D_MODEL), lambda j: (j, 0))
    operands = [x]
    in_specs = [hbm_spec if x_native else row_spec]
    if conv0 is not None:
        operands.append(conv0)
        in_specs.append(hbm_spec)
    operands += list(ssm0)
    in_specs += [_const_spec((rb, SSM_FLAT), ssm_layer)] * 2
    aliases = {}
    if prev:
        aliases[len(operands)] = 1
        operands += list(prev)
        in_specs += [hbm_spec] + [_const_spec((layer, rb, SSM_FLAT))] * 2
    operands += list(after)
    in_specs += [hbm_spec] * len(after)
    in_specs += [
        _const_spec((DEPTH, D_MODEL)),
        _const_spec((D_MODEL, IN_PROJ_WIDTH), layer),
        _const_spec((CONV_WIDTH, CONV_A_WIDTH), layer),
        _const_spec((DEPTH, CONV_A_WIDTH)),
        _const_spec((CONV_A_WIDTH, D_MODEL), layer),
        _const_spec((4, SSM_FLAT), layer),
        _const_spec((SSM_CHUNKS, 2 * CHUNK_CH, 2 * CHUNK_ST), layer),
        _const_spec((SSM_CHUNKS, 2 * CHUNK_CH, CHUNK_ST), layer),
        _const_spec((SSM_CHUNKS, 2 * CHUNK_CH, CHUNK_ST), layer),
        _const_spec((SSM_CHUNKS, CHUNK_CH, CHUNK_CH), layer),
        _const_spec((DEPTH, SSM_WIDTH)),
        _const_spec((SSM_WIDTH, D_MODEL), layer),
        _const_spec((SSM_WIDTH, D_MODEL), layer),
        _const_spec((D_MODEL, D_MODEL), layer),
    ]
    out_specs = [row_spec, hbm_spec] + [_stack_spec(layer + 1, rb, SSM_FLAT)] * 2
    out_shape = ([jax.ShapeDtypeStruct((n, D_MODEL), _F32),
                  jax.ShapeDtypeStruct((DEPTH, rb, CONV_WIDTH - 1, CONV_A_WIDTH), _F32)]
                 + [jax.ShapeDtypeStruct((layer + 1, rb, SSM_FLAT), _F32)] * 2)
    s_rows = rows // 2 if tb % 2 == 0 else rows
    scratch = [
        pltpu.VMEM((carry_rows + rows, CONV_A_WIDTH), _F32),
        pltpu.VMEM((rb + s_rows, SSM_FLAT), _F32),
        pltpu.VMEM((rb + s_rows, SSM_FLAT), _F32),
    ]
    scratch += _conv_state_scratch(rb, CONV_A_WIDTH)
    if x_native:
        scratch += _native_scratch(tb, rb, D_MODEL)
    return pl.pallas_call(
        functools.partial(_mixer_kernel, rb=rb, tb=tb, layer=layer, zero_conv=conv0 is None,
                          single_step=n == rows, n_after=len(after)),
        grid=(n // rows,),
        in_specs=in_specs,
        out_specs=out_specs,
        out_shape=out_shape,
        scratch_shapes=scratch,
        input_output_aliases=aliases,
        compiler_params=pltpu.CompilerParams(
            dimension_semantics=("arbitrary",), vmem_limit_bytes=VMEM_LIMIT_BYTES),
        name=f"mixer_l{layer}_rb{rb}",
    )(*operands,
      wts["norm_mix"], wts["w_in"], wts["conv_a_w"], wts["conv_a_b"], wts["w_out_a"],
      wts["abar"], wts["b2blk"], wts["c2_re"], wts["c2_im"], wts["fblk"], wts["d_skip"],
      wts["w_glu_a"], wts["w_glu_b"], wts["w_o"])


def _ffn_call(x, p, f0, prev, wts, layer, rb, tb, final_norm, p_native, out_native):
    n = x.shape[0]
    rows = rb * tb
    carry_rows = (CONV_WIDTH - 1) * rb
    hbm_spec = pl.BlockSpec(memory_space=pl.ANY)
    row_spec = pl.BlockSpec((rows, D_MODEL), lambda j: (j, 0))
    operands = [x, p]
    in_specs = [
        row_spec,
        hbm_spec if p_native else pl.BlockSpec((None, rows, PLE_DIM), lambda j: (layer, j, 0)),
    ]
    if f0 is not None:
        operands.append(f0)
        in_specs.append(hbm_spec)
    aliases = {}
    if prev is not None:
        aliases[len(operands)] = 1
        operands.append(prev)
        in_specs.append(hbm_spec)
    in_specs += [
        _const_spec((DEPTH, D_MODEL)),
        _const_spec((D_MODEL, 2 * D_FF), layer),
        _const_spec((CONV_WIDTH, D_FF), layer),
        _const_spec((DEPTH, D_FF)),
        _const_spec((D_FF, D_MODEL), layer),
        _const_spec((DEPTH, D_MODEL)),
        _const_spec((D_MODEL, D_MODEL), layer),
        _const_spec((PLE_DIM, D_MODEL), layer),
        _const_spec((1, D_MODEL)),
    ]
    out_specs = [hbm_spec if out_native else row_spec, hbm_spec]
    out_shape = [jax.ShapeDtypeStruct((rb, n // rb, D_MODEL) if out_native else (n, D_MODEL), _F32),
                 jax.ShapeDtypeStruct((DEPTH, rb, CONV_WIDTH - 1, D_FF), _F32)]
    scratch = [pltpu.VMEM((carry_rows + rows, FF_CHUNK_MAX), _F32)]
    scratch += _conv_state_scratch(rb, D_FF)
    if p_native:
        scratch += _native_scratch(tb, rb, PLE_DIM)
    if out_native:
        scratch += _native_scratch(tb, rb, D_MODEL)
    return pl.pallas_call(
        functools.partial(_ffn_kernel, rb=rb, tb=tb, layer=layer, zero_conv=f0 is None,
                          single_step=n == rows,
                          final_norm=final_norm, p_native=p_native, out_native=out_native),
        grid=(n // rows,),
        in_specs=in_specs,
        out_specs=out_specs,
        out_shape=out_shape,
        scratch_shapes=scratch,
        input_output_aliases=aliases,
        compiler_params=pltpu.CompilerParams(
            dimension_semantics=("arbitrary",), vmem_limit_bytes=VMEM_LIMIT_BYTES),
        name=f"ffn_l{layer}_rb{rb}",
    )(*operands,
      wts["norm_ffn"], wts["w_up"], wts["ffn_conv_w"], wts["ffn_conv_b"], wts["w_down"],
      wts["norm_ple"], wts["w_ple_gate"], wts["w_ple"], wts["norm_final"])


def _run_trunk(x, p, conv0, sre0, sim0, f0, wts, rb, tb_mixer, tb_ffn, native, first_after=()):
    mixer_states, ffn_state = [], None
    for layer in range(DEPTH):
        last = layer == DEPTH - 1
        ssm_layer = 0 if conv0 is None else layer
        x, *mixer_states = _mixer_call(x, conv0, (sre0, sim0), mixer_states, wts, layer, ssm_layer,
                                       rb, tb_mixer, x_native=native and layer == 0,
                                       after=first_after if layer == 0 else ())
        x, ffn_state = _ffn_call(x, p, f0, ffn_state, wts, layer, rb, tb_ffn,
                                 final_norm=last, p_native=native, out_native=native and last)
    return (x, *mixer_states, ffn_state)


def kernel(x_prompt, x_sample, p_prompt, p_sample, state_conv_a, state_ssm_re, state_ssm_im, state_ffn_conv, norm_mix, w_in, conv_a_w, conv_a_b, w_out_a, log_dt, lam_re, lam_im, b_re, b_im, c_re, c_im, d_skip, w_glu_a, w_glu_b, w_o, norm_ffn, w_up, ffn_conv_w, ffn_conv_b, w_down, norm_ple, w_ple_gate, w_ple, norm_final):
    abar, b2blk, c2_re, c2_im, fblk = _ssm_prep(log_dt, lam_re, lam_im, b_re, b_im, c_re, c_im)
    wts = {
        "norm_mix": norm_mix,
        "w_in": w_in.astype(_BF16),
        "conv_a_w": conv_a_w,
        "conv_a_b": conv_a_b,
        "w_out_a": w_out_a.astype(_BF16),
        "abar": abar, "b2blk": b2blk, "c2_re": c2_re, "c2_im": c2_im, "fblk": fblk,
        "d_skip": d_skip,
        "w_glu_a": w_glu_a.astype(_BF16),
        "w_glu_b": w_glu_b.astype(_BF16),
        "w_o": w_o.astype(_BF16),
        "norm_ffn": norm_ffn,
        "w_up": w_up.astype(_BF16),
        "ffn_conv_w": ffn_conv_w,
        "ffn_conv_b": ffn_conv_b,
        "w_down": w_down.astype(_BF16),
        "norm_ple": norm_ple,
        "w_ple_gate": w_ple_gate.astype(_BF16),
        "w_ple": w_ple.astype(_BF16),
        "norm_final": norm_final.reshape(1, D_MODEL),
    }

    bp = x_prompt.shape[0]
    zs = jnp.zeros((1, bp, SSM_FLAT), _F32)
    casts = [wts[k] for k in ("w_up", "w_down", "w_ple_gate", "w_ple")]
    y_prompt, conv_p, sre_p, sim_p, ffn_p = _run_trunk(
        x_prompt, p_prompt, None, zs, zs, None, wts, bp, MIXER_TIME_BLOCK, FFN_TIME_BLOCK,
        native=True, first_after=casts)

    bs, sseq, _ = x_sample.shape
    xs = x_sample.transpose(1, 0, 2).reshape(sseq * bs, D_MODEL)
    ps = p_sample.transpose(0, 2, 1, 3).reshape(DEPTH, sseq * bs, PLE_DIM)
    ys, conv_s, sre_s, sim_s, ffn_s = _run_trunk(
        xs, ps, state_conv_a,
        state_ssm_re.reshape(DEPTH, bs, SSM_FLAT), state_ssm_im.reshape(DEPTH, bs, SSM_FLAT),
        state_ffn_conv, wts, bs, sseq, sseq, native=False)
    y_sample = ys.reshape(sseq, bs, D_MODEL).transpose(1, 0, 2)

    def ssm_state(s, b):
        return s.reshape(DEPTH, b, SSM_GROUPS, SSM_STATE)

    return (y_prompt, y_sample,
            conv_p, ssm_state(sre_p, bp), ssm_state(sim_p, bp), ffn_p,
            conv_s, ssm_state(sre_s, bs), ssm_state(sim_s, bs), ffn_s)
```

```python
import functools
import math

import jax
import jax.numpy as jnp
from jax import lax
from jax.experimental import pallas as pl
from jax.experimental.pallas import tpu as pltpu

D_MODEL = 1024
DEPTH = 2
CONV_WIDTH = 3
CONV_A_WIDTH = 512
SSM_WIDTH = 512
SSM_GROUP = 16
SSM_GROUPS = 32
SSM_STATE = 64
SSM_FLAT = SSM_GROUPS * SSM_STATE
D_FF = 2816
PLE_DIM = 256
NORM_EPS = 1e-6
IN_PROJ_WIDTH = 4096

SSM_CHUNKS = 4
CHUNK_CH = SSM_WIDTH // SSM_CHUNKS
CHUNK_ST = SSM_FLAT // SSM_CHUNKS

FF_CHUNKS = ((0, 1024), (1024, 2048), (2048, 2816))
FF_CHUNK_MAX = 1024

MIXER_TIME_BLOCKS = (64, 128)
FFN_TIME_BLOCK = 128
VMEM_LIMIT_BYTES = 56 * 1024 * 1024

_BF16 = jnp.bfloat16
_F32 = jnp.float32


def _dot(a, b):
    return jnp.dot(a, b, preferred_element_type=_F32)


def _dot_nt(a, b):
    return lax.dot_general(a, b, (((1,), (1,)), ((), ())), preferred_element_type=_F32)


def _dot_nt_split(a, b):
    a_hi = a.astype(_BF16)
    b_hi = b.astype(_BF16)
    a_lo = (a - a_hi.astype(_F32)).astype(_BF16)
    b_lo = (b - b_hi.astype(_F32)).astype(_BF16)
    return _dot_nt(a_hi, b_hi) + _dot_nt(a_hi, b_lo) + _dot_nt(a_lo, b_hi)


def _rms(x, g):
    ms = jnp.mean(x * x, axis=-1, keepdims=True)
    return x * lax.rsqrt(ms + NORM_EPS) * g


def _gelu(x):
    c = math.sqrt(2.0 / math.pi)
    t = jnp.tanh(x * (c + (0.044715 * c) * (x * x)))
    return x * (0.5 + 0.5 * t)


def _sigmoid(x):
    return 0.5 * jnp.tanh(0.5 * x) + 0.5


def _seq_copies(hbm_ref, lead, buf, sem, step, slot, tb, to_hbm):
    copies = []
    for b in range(buf.shape[2]):
        hbm_view = hbm_ref.at[lead + (b, pl.ds(step * tb, tb))]
        vmem_view = buf.at[slot, :, b]
        src, dst = (vmem_view, hbm_view) if to_hbm else (hbm_view, vmem_view)
        copies.append(pltpu.make_async_copy(src, dst, sem.at[slot]))
    return copies


def _fetch_time_major(hbm_ref, lead, buf, sem, tb):
    j = pl.program_id(0)
    slot = j % 2

    @pl.when(j == 0)
    def _():
        for c in _seq_copies(hbm_ref, lead, buf, sem, 0, 0, tb, False):
            c.start()

    @pl.when(j + 1 < pl.num_programs(0))
    def _():
        for c in _seq_copies(hbm_ref, lead, buf, sem, j + 1, 1 - slot, tb, False):
            c.start()

    for c in _seq_copies(hbm_ref, lead, buf, sem, j, slot, tb, False):
        c.wait()


def _read_time_major(buf):
    tb, nb, width = buf.shape[1:]
    return buf[pl.program_id(0) % 2].reshape(tb * nb, width)


def _store_batch_major(val, hbm_ref, buf, sem, tb):
    j = pl.program_id(0)
    slot = j % 2
    buf[slot] = val.reshape(tb, buf.shape[2], buf.shape[3])
    for c in _seq_copies(hbm_ref, (), buf, sem, j, slot, tb, True):
        c.start()

    @pl.when(j >= 1)
    def _():
        for c in _seq_copies(hbm_ref, (), buf, sem, j - 1, 1 - slot, tb, True):
            c.wait()

    @pl.when(j == pl.num_programs(0) - 1)
    def _():
        for c in _seq_copies(hbm_ref, (), buf, sem, j, slot, tb, True):
            c.wait()


def _conv_state_copies(hbm_ref, slot, carry, sem, to_hbm):
    copies = []
    for k in range(CONV_WIDTH - 1):
        hbm_view = hbm_ref.at[slot, :, k, :]
        src, dst = (carry.at[k], hbm_view) if to_hbm else (hbm_view, carry.at[k])
        copies.append(pltpu.make_async_copy(src, dst, sem.at[slot, k]))
    return copies


def _at_step(first, single_step, body):
    if single_step:
        body()
    else:
        step = 0 if first else pl.num_programs(0) - 1
        pl.when(pl.program_id(0) == step)(body)


def _conv_state_fetch(state_hbm, layer, carry, sem, single_step, wait):
    def body():
        if state_hbm is None:
            if not wait:
                carry[...] = jnp.zeros(carry.shape, carry.dtype)
        else:
            for c in _conv_state_copies(state_hbm, layer, carry, sem, False):
                c.wait() if wait else c.start()

    _at_step(True, single_step, body)


def _conv_state_store(carry, out_hbm, layer, sem, single_step, wait):
    def body():
        for slot in range(layer, out_hbm.shape[0]):
            for c in _conv_state_copies(out_hbm, slot, carry, sem, True):
                c.wait() if wait else c.start()

    _at_step(False, single_step, body)


def _ssm_prep_kernel(logdt_ref, lre_ref, lim_ref, bre_ref, bim_ref, cre_ref, cim_ref,
                     abar_ref, b2blk_ref, c2_re_ref, c2_im_ref, fblk_ref):
    dt = jnp.exp(logdt_ref[...])
    lre = lre_ref[...]
    lim = lim_ref[...]
    mag = jnp.exp(lre * dt)
    ang = lim * dt
    ar = mag * jnp.cos(ang)
    ai = mag * jnp.sin(ang)
    abar_ref[0:1, :] = ar
    abar_ref[1:2, :] = ai
    abar_ref[2:3, :] = ar * ar - ai * ai
    abar_ref[3:4, :] = 2.0 * (ar * ai)
    den = lre * lre + lim * lim
    nr = ar - 1.0
    coef_re = (nr * lre + ai * lim) / den
    coef_im = (ai * lre - nr * lim) / den
    bre = bre_ref[...]
    bim = bim_ref[...]
    bbar_re = coef_re * bre - coef_im * bim
    bbar_im = coef_re * bim + coef_im * bre
    abb_re = ar * bbar_re - ai * bbar_im
    abb_im = ar * bbar_im + ai * bbar_re
    row = lax.broadcasted_iota(jnp.int32, (CHUNK_CH, CHUNK_ST), 0)
    lane = lax.broadcasted_iota(jnp.int32, (CHUNK_CH, CHUNK_ST), 1)
    same_group = (row // SSM_GROUP) == (lane // SSM_STATE)
    reps = CHUNK_CH // SSM_GROUP
    for c in range(SSM_CHUNKS):
        cols = slice(c * CHUNK_ST, (c + 1) * CHUNK_ST)
        rows = slice(c * CHUNK_CH, (c + 1) * CHUNK_CH)

        def b_block(v, cols=cols):
            return jnp.where(same_group, jnp.concatenate([v[:, cols]] * reps, axis=0), 0.0)

        b_re = b_block(bbar_re)
        b_im = b_block(bbar_im)
        b2blk_ref[c, 0:CHUNK_CH, 0:CHUNK_ST] = b_block(abb_re).astype(_BF16)
        b2blk_ref[c, 0:CHUNK_CH, CHUNK_ST:2 * CHUNK_ST] = b_block(abb_im).astype(_BF16)
        b2blk_ref[c, CHUNK_CH:2 * CHUNK_CH, 0:CHUNK_ST] = b_re.astype(_BF16)
        b2blk_ref[c, CHUNK_CH:2 * CHUNK_CH, CHUNK_ST:2 * CHUNK_ST] = b_im.astype(_BF16)
        c_re = jnp.where(same_group, cre_ref[rows, :], 0.0)
        c_im = jnp.where(same_group, cim_ref[rows, :], 0.0)
        arc = ar[:, cols]
        aic = ai[:, cols]
        c2_re_ref[c, 0:CHUNK_CH, :] = c_re.astype(_BF16)
        c2_re_ref[c, CHUNK_CH:2 * CHUNK_CH, :] = (c_re * arc - c_im * aic).astype(_BF16)
        c2_im_ref[c, 0:CHUNK_CH, :] = (-c_im).astype(_BF16)
        c2_im_ref[c, CHUNK_CH:2 * CHUNK_CH, :] = (-(c_re * aic + c_im * arc)).astype(_BF16)
        feed = _dot_nt_split(b_re, c_re) - _dot_nt_split(b_im, c_im)
        fblk_ref[c] = feed.astype(_BF16)


def _ssm_prep(log_dt, lam_re, lam_im, b_re, b_im, c_re, c_im):
    logdt = jnp.repeat(log_dt, SSM_STATE, axis=1).reshape(DEPTH, 1, SSM_FLAT)
    lre = lam_re.reshape(DEPTH, 1, SSM_FLAT)
    lim = lam_im.reshape(DEPTH, 1, SSM_FLAT)
    bre = b_re.transpose(0, 3, 1, 2).reshape(DEPTH, SSM_GROUP, SSM_FLAT)
    bim = b_im.transpose(0, 3, 1, 2).reshape(DEPTH, SSM_GROUP, SSM_FLAT)
    reps = CHUNK_ST // SSM_STATE
    cre = jnp.tile(c_re.reshape(DEPTH, SSM_WIDTH, SSM_STATE), (1, 1, reps))
    cim = jnp.tile(c_im.reshape(DEPTH, SSM_WIDTH, SSM_STATE), (1, 1, reps))

    def spec(*shape):
        return pl.BlockSpec((None,) + shape, lambda i: (i,) + (0,) * len(shape))

    return pl.pallas_call(
        _ssm_prep_kernel,
        grid=(DEPTH,),
        in_specs=[spec(1, SSM_FLAT), spec(1, SSM_FLAT), spec(1, SSM_FLAT),
                  spec(SSM_GROUP, SSM_FLAT), spec(SSM_GROUP, SSM_FLAT),
                  spec(SSM_WIDTH, CHUNK_ST), spec(SSM_WIDTH, CHUNK_ST)],
        out_specs=[spec(4, SSM_FLAT),
                   spec(SSM_CHUNKS, 2 * CHUNK_CH, 2 * CHUNK_ST),
                   spec(SSM_CHUNKS, 2 * CHUNK_CH, CHUNK_ST),
                   spec(SSM_CHUNKS, 2 * CHUNK_CH, CHUNK_ST),
                   spec(SSM_CHUNKS, CHUNK_CH, CHUNK_CH)],
        out_shape=[jax.ShapeDtypeStruct((DEPTH, 4, SSM_FLAT), _F32),
                   jax.ShapeDtypeStruct((DEPTH, SSM_CHUNKS, 2 * CHUNK_CH, 2 * CHUNK_ST), _BF16),
                   jax.ShapeDtypeStruct((DEPTH, SSM_CHUNKS, 2 * CHUNK_CH, CHUNK_ST), _BF16),
                   jax.ShapeDtypeStruct((DEPTH, SSM_CHUNKS, 2 * CHUNK_CH, CHUNK_ST), _BF16),
                   jax.ShapeDtypeStruct((DEPTH, SSM_CHUNKS, CHUNK_CH, CHUNK_CH), _BF16)],
        name="ssm_prep",
    )(logdt, lre, lim, bre, bim, cre, cim)


def _mixer_kernel(x_ref, *refs, rb, tb, layer, zero_conv, single_step, n_cast):
    refs = list(refs)
    conv0_hbm = None if zero_conv else refs.pop(0)
    sre0_ref, sim0_ref = refs.pop(0), refs.pop(0)
    prev = [refs.pop(0) for _ in range(3)] if layer > 0 else []
    cast_in = [refs.pop(0) for _ in range(n_cast)]
    (g_ref, w_in_ref, cw_ref, cb_ref, w_out_a_ref,
     abar_ref, b2blk_ref, c2_re_ref, c2_im_ref, fblk_ref, dskip_ref,
     glu_a_ref, glu_b_ref, w_o_ref,
     out_ref, conv_out_hbm, sre_stack_ref, sim_stack_ref, *refs) = refs
    cast_out = [refs.pop(0) for _ in range(n_cast)]
    vbuf, s_re, s_im, conv_carry, conv_sem, *x_dma = refs
    for src_ref, dst_ref in zip(cast_in, cast_out):
        dst_ref[...] = src_ref[...].astype(_BF16)
    rows = rb * tb
    carry_rows = (CONV_WIDTH - 1) * rb
    sre_out_ref = sre_stack_ref.at[layer]
    sim_out_ref = sim_stack_ref.at[layer]

    @pl.when(pl.program_id(0) == 0)
    def _():
        for stack_ref, prev_ref in zip((sre_stack_ref, sim_stack_ref), prev[1:]):
            stack_ref[0:layer] = prev_ref[...]
        sre_out_ref[...] = sre0_ref[...]
        sim_out_ref[...] = sim0_ref[...]

    conv_io = (layer, conv_carry, conv_sem, single_step)
    _conv_state_fetch(conv0_hbm, *conv_io, wait=False)
    if not single_step:
        _conv_state_fetch(conv0_hbm, *conv_io, wait=True)
    if x_dma:
        _fetch_time_major(x_ref, (), *x_dma, tb)
        x = _read_time_major(x_dma[0])
    else:
        x = x_ref[...]
    xn = _rms(x, g_ref[layer:layer + 1, :]).astype(_BF16)

    def proj(lo, hi):
        return _dot(xn, w_in_ref[:, lo:hi])

    u = proj(1536, 2048)
    pair = 2 if tb % 2 == 0 else 1
    steps = tb // pair
    srows = steps * rb
    if pair == 2:
        u4 = u.reshape(steps, 2, rb, SSM_WIDTH)
        u_parts = [u4[:, i].reshape(srows, SSM_WIDTH) for i in range(2)]
    else:
        u_parts = [u]
    ub_parts = [v.astype(_BF16) for v in u_parts]
    s_re[0:rb, :] = sre_out_ref[...]
    s_im[0:rb, :] = sim_out_ref[...]
    for c in range(SSM_CHUNKS):
        ch = slice(c * CHUNK_CH, (c + 1) * CHUNK_CH)
        if pair == 2:
            inc = _dot(jnp.concatenate([ub_parts[0][:, ch], ub_parts[1][:, ch]], axis=1), b2blk_ref[c])
        else:
            inc = _dot(ub_parts[0][:, ch], b2blk_ref[c, CHUNK_CH:2 * CHUNK_CH, :])
        s_re[rb:rb + srows, c * CHUNK_ST:(c + 1) * CHUNK_ST] = inc[:, 0:CHUNK_ST]
        s_im[rb:rb + srows, c * CHUNK_ST:(c + 1) * CHUNK_ST] = inc[:, CHUNK_ST:2 * CHUNK_ST]

    a_row = 2 * (pair - 1)
    for c in range(SSM_CHUNKS):
        cols = slice(c * CHUNK_ST, (c + 1) * CHUNK_ST)
        ar = jnp.broadcast_to(abar_ref[a_row:a_row + 1, cols], (rb, CHUNK_ST))
        ai = jnp.broadcast_to(abar_ref[a_row + 1:a_row + 2, cols], (rb, CHUNK_ST))

        def step(k, carry, cols=cols, ar=ar, ai=ai):
            sr, si = carry
            r0 = pl.multiple_of(rb + k * rb, rb)
            nsr = ar * sr - ai * si + s_re[pl.ds(r0, rb), cols]
            nsi = ar * si + ai * sr + s_im[pl.ds(r0, rb), cols]
            s_re[pl.ds(r0, rb), cols] = nsr
            s_im[pl.ds(r0, rb), cols] = nsi
            return nsr, nsi

        sr, si = lax.fori_loop(0, steps, step, (s_re[0:rb, cols], s_im[0:rb, cols]), unroll=True)
        sre_out_ref[:, cols] = sr
        sim_out_ref[:, cols] = si

    h = proj(0, 512)
    gate_c = proj(1024, 1536)
    if single_step:
        _conv_state_fetch(conv0_hbm, *conv_io, wait=True)
    vbuf[0:rb, :] = conv_carry[0]
    vbuf[rb:carry_rows, :] = conv_carry[1]
    vbuf[carry_rows:carry_rows + rows, :] = gate_c * h
    conv_y = (cb_ref[layer:layer + 1, :]
              + vbuf[0:rows, :] * cw_ref[0:1, :]
              + vbuf[rb:rb + rows, :] * cw_ref[1:2, :]
              + vbuf[2 * rb:2 * rb + rows, :] * cw_ref[2:3, :])
    conv_carry[0] = vbuf[rows:rows + rb, :]
    conv_carry[1] = vbuf[rows + rb:rows + carry_rows, :]
    if single_step:
        _conv_state_store(conv_carry, conv_out_hbm, layer, conv_sem, single_step, wait=False)
    gate_b = proj(512, 1024)
    z_a = _dot((gate_b * conv_y).astype(_BF16), w_out_a_ref[...])
    merged = _sigmoid(proj(2048, 3072)) * z_a

    y_parts = [[] for _ in range(pair)]
    for c in range(SSM_CHUNKS):
        ch = slice(c * CHUNK_CH, (c + 1) * CHUNK_CH)
        cols = slice(c * CHUNK_ST, (c + 1) * CHUNK_ST)
        if pair == 2:
            y2 = (_dot_nt(s_re[:, cols].astype(_BF16), c2_re_ref[c])
                  + _dot_nt(s_im[:, cols].astype(_BF16), c2_im_ref[c]))
            y_parts[0].append(y2[0:srows, CHUNK_CH:2 * CHUNK_CH] + _dot(ub_parts[0][:, ch], fblk_ref[c]))
            y_parts[1].append(y2[rb:rb + srows, 0:CHUNK_CH])
        else:
            y_parts[0].append(
                _dot_nt(s_re[rb:rb + srows, cols].astype(_BF16), c2_re_ref[c, 0:CHUNK_CH, :])
                + _dot_nt(s_im[rb:rb + srows, cols].astype(_BF16), c2_im_ref[c, 0:CHUNK_CH, :]))
    d_skip = dskip_ref[layer:layer + 1, :]
    ssm_parts = [jnp.concatenate(yp, axis=1) + d_skip * up for yp, up in zip(y_parts, u_parts)]
    if pair == 2:
        ssm_y = jnp.stack([v.reshape(steps, rb, SSM_WIDTH) for v in ssm_parts],
                          axis=1).reshape(rows, SSM_WIDTH)
    else:
        ssm_y = ssm_parts[0]
    sy = _gelu(ssm_y).astype(_BF16)
    z_b = _dot(sy, glu_a_ref[...]) * _sigmoid(_dot(sy, glu_b_ref[...]))
    merged = (merged + _sigmoid(proj(3072, 4096)) * z_b).astype(_BF16)
    out_ref[...] = x + _dot(merged, w_o_ref[...])
    if not single_step:
        _conv_state_store(conv_carry, conv_out_hbm, layer, conv_sem, single_step, wait=False)
    _conv_state_store(conv_carry, conv_out_hbm, layer, conv_sem, single_step, wait=True)


def _ffn_kernel(x_ref, p_ref, *refs, rb, tb, layer, zero_conv, single_step, final_norm, p_native,
                out_native):
    refs = list(refs)
    f0_hbm = None if zero_conv else refs.pop(0)
    if layer > 0:
        refs.pop(0)
    (g2_ref, w_up_ref, fcw_ref, fcb_ref, w_down_ref,
     g3_ref, w_pg_ref, w_ple_ref, gf_ref,
     out_ref, fconv_out_hbm,
     fbuf, fcarry, fsem, *dma) = refs
    rows = rb * tb
    carry_rows = (CONV_WIDTH - 1) * rb
    p_dma = [dma.pop(0), dma.pop(0)] if p_native else None
    out_dma = [dma.pop(0), dma.pop(0)] if out_native else None

    conv_io = (layer, fcarry, fsem, single_step)
    _conv_state_fetch(f0_hbm, *conv_io, wait=False)
    if not single_step:
        _conv_state_fetch(f0_hbm, *conv_io, wait=True)
    if p_native:
        _fetch_time_major(p_ref, (layer,), *p_dma, tb)

    x = x_ref[...]
    xn = _rms(x, g2_ref[layer:layer + 1, :]).astype(_BF16)
    acc = x
    if single_step:
        _conv_state_fetch(f0_hbm, *conv_io, wait=True)
    for lo, hi in FF_CHUNKS:
        wc = hi - lo
        up_a = _dot(xn, w_up_ref[:, lo:hi])
        up_b = _dot(xn, w_up_ref[:, D_FF + lo:D_FF + hi])
        fbuf[0:rb, 0:wc] = fcarry[0, :, lo:hi]
        fbuf[rb:carry_rows, 0:wc] = fcarry[1, :, lo:hi]
        fbuf[carry_rows:carry_rows + rows, 0:wc] = up_a
        conv_a = (fcb_ref[layer:layer + 1, lo:hi]
                  + fbuf[0:rows, 0:wc] * fcw_ref[0:1, lo:hi]
                  + fbuf[rb:rb + rows, 0:wc] * fcw_ref[1:2, lo:hi]
                  + fbuf[2 * rb:2 * rb + rows, 0:wc] * fcw_ref[2:3, lo:hi])
        fcarry[0, :, lo:hi] = fbuf[rows:rows + rb, 0:wc]
        fcarry[1, :, lo:hi] = fbuf[rows + rb:rows + carry_rows, 0:wc]
        hid = (_gelu(conv_a) * up_b).astype(_BF16)
        acc = acc + _dot(hid, w_down_ref[lo:hi, :])
    if single_step:
        _conv_state_store(fcarry, fconv_out_hbm, layer, fsem, single_step, wait=False)
    x2 = acc
    xn3 = _rms(x2, g3_ref[layer:layer + 1, :]).astype(_BF16)
    gate = _sigmoid(_dot(xn3, w_pg_ref[...]))
    p = _read_time_major(p_dma[0]) if p_native else p_ref[...]
    pe = _dot(p.astype(_BF16), w_ple_ref[...])
    x3 = x2 + gate * pe
    if final_norm:
        x3 = _rms(x3, gf_ref[...])
    if out_native:
        _store_batch_major(x3, out_ref, *out_dma, tb)
    else:
        out_ref[...] = x3
    if not single_step:
        _conv_state_store(fcarry, fconv_out_hbm, layer, fsem, single_step, wait=False)
    _conv_state_store(fcarry, fconv_out_hbm, layer, fsem, single_step, wait=True)


def _const_spec(shape, layer=None):
    if layer is None:
        return pl.BlockSpec(shape, lambda j: (0,) * len(shape), pipeline_mode=pl.Buffered(1))
    return pl.BlockSpec((None,) + shape, lambda j: (layer,) + (0,) * len(shape),
                        pipeline_mode=pl.Buffered(1))


def _native_scratch(tb, rb, width):
    return [pltpu.VMEM((2, tb, rb, width), _F32), pltpu.SemaphoreType.DMA((2,))]


def _stack_spec(n_layers, rb, width):
    return pl.BlockSpec((n_layers, rb, width), lambda j: (0, 0, 0))


def _conv_state_scratch(rb, width):
    return [pltpu.VMEM((CONV_WIDTH - 1, rb, width), _F32), pltpu.SemaphoreType.DMA((DEPTH, CONV_WIDTH - 1))]


def _mixer_call(x, conv0, ssm0, prev, wts, layer, ssm_layer, rb, tb, x_native, cast=()):
    n = x.shape[0] * x.shape[1] if x_native else x.shape[0]
    rows = rb * tb
    carry_rows = (CONV_WIDTH - 1) * rb
    hbm_spec = pl.BlockSpec(memory_space=pl.ANY)
    row_spec = pl.BlockSpec((rows, D_MODEL), lambda j: (j, 0))
    operands = [x]
    in_specs = [hbm_spec if x_native else row_spec]
    if conv0 is not None:
        operands.append(conv0)
        in_specs.append(hbm_spec)
    operands += list(ssm0)
    in_specs += [_const_spec((rb, SSM_FLAT), ssm_layer)] * 2
    aliases = {}
    if prev:
        aliases[len(operands)] = 1
        operands += list(prev)
        in_specs += [hbm_spec] + [_const_spec((layer, rb, SSM_FLAT))] * 2
    steps = n // rows
    per_layer = steps // DEPTH
    cast_specs = []
    for w in cast:
        assert steps % DEPTH == 0 and w.shape[1] % (16 * per_layer) == 0, (steps, w.shape)
        cast_specs.append(pl.BlockSpec((None, w.shape[1] // per_layer, w.shape[2]),
                                       lambda j: (j // per_layer, j % per_layer, 0)))
    operands += list(cast)
    in_specs += cast_specs
    in_specs += [
        _const_spec((DEPTH, D_MODEL)),
        _const_spec((D_MODEL, IN_PROJ_WIDTH), layer),
        _const_spec((CONV_WIDTH, CONV_A_WIDTH), layer),
        _const_spec((DEPTH, CONV_A_WIDTH)),
        _const_spec((CONV_A_WIDTH, D_MODEL), layer),
        _const_spec((4, SSM_FLAT), layer),
        _const_spec((SSM_CHUNKS, 2 * CHUNK_CH, 2 * CHUNK_ST), layer),
        _const_spec((SSM_CHUNKS, 2 * CHUNK_CH, CHUNK_ST), layer),
        _const_spec((SSM_CHUNKS, 2 * CHUNK_CH, CHUNK_ST), layer),
        _const_spec((SSM_CHUNKS, CHUNK_CH, CHUNK_CH), layer),
        _const_spec((DEPTH, SSM_WIDTH)),
        _const_spec((SSM_WIDTH, D_MODEL), layer),
        _const_spec((SSM_WIDTH, D_MODEL), layer),
        _const_spec((D_MODEL, D_MODEL), layer),
    ]
    out_specs = [row_spec, hbm_spec] + [_stack_spec(layer + 1, rb, SSM_FLAT)] * 2 + cast_specs
    out_shape = ([jax.ShapeDtypeStruct((n, D_MODEL), _F32),
                  jax.ShapeDtypeStruct((DEPTH, rb, CONV_WIDTH - 1, CONV_A_WIDTH), _F32)]
                 + [jax.ShapeDtypeStruct((layer + 1, rb, SSM_FLAT), _F32)] * 2
                 + [jax.ShapeDtypeStruct(w.shape, _BF16) for w in cast])
    s_rows = rows // 2 if tb % 2 == 0 else rows
    scratch = [
        pltpu.VMEM((carry_rows + rows, CONV_A_WIDTH), _F32),
        pltpu.VMEM((rb + s_rows, SSM_FLAT), _F32),
        pltpu.VMEM((rb + s_rows, SSM_FLAT), _F32),
    ]
    scratch += _conv_state_scratch(rb, CONV_A_WIDTH)
    if x_native:
        scratch += _native_scratch(tb, rb, D_MODEL)
    return pl.pallas_call(
        functools.partial(_mixer_kernel, rb=rb, tb=tb, layer=layer, zero_conv=conv0 is None,
                          single_step=n == rows, n_cast=len(cast)),
        grid=(n // rows,),
        in_specs=in_specs,
        out_specs=out_specs,
        out_shape=out_shape,
        scratch_shapes=scratch,
        input_output_aliases=aliases,
        compiler_params=pltpu.CompilerParams(
            dimension_semantics=("arbitrary",), vmem_limit_bytes=VMEM_LIMIT_BYTES),
        name=f"mixer_l{layer}_rb{rb}",
    )(*operands,
      wts["norm_mix"], wts["w_in"], wts["conv_a_w"], wts["conv_a_b"], wts["w_out_a"],
      wts["abar"], wts["b2blk"], wts["c2_re"], wts["c2_im"], wts["fblk"], wts["d_skip"],
      wts["w_glu_a"], wts["w_glu_b"], wts["w_o"])


def _ffn_call(x, p, f0, prev, wts, layer, rb, tb, final_norm, p_native, out_native):
    n = x.shape[0]
    rows = rb * tb
    carry_rows = (CONV_WIDTH - 1) * rb
    hbm_spec = pl.BlockSpec(memory_space=pl.ANY)
    row_spec = pl.BlockSpec((rows, D_MODEL), lambda j: (j, 0))
    operands = [x, p]
    in_specs = [
        row_spec,
        hbm_spec if p_native else pl.BlockSpec((None, rows, PLE_DIM), lambda j: (layer, j, 0)),
    ]
    if f0 is not None:
        operands.append(f0)
        in_specs.append(hbm_spec)
    aliases = {}
    if prev is not None:
        aliases[len(operands)] = 1
        operands.append(prev)
        in_specs.append(hbm_spec)
    in_specs += [
        _const_spec((DEPTH, D_MODEL)),
        _const_spec((D_MODEL, 2 * D_FF), layer),
        _const_spec((CONV_WIDTH, D_FF), layer),
        _const_spec((DEPTH, D_FF)),
        _const_spec((D_FF, D_MODEL), layer),
        _const_spec((DEPTH, D_MODEL)),
        _const_spec((D_MODEL, D_MODEL), layer),
        _const_spec((PLE_DIM, D_MODEL), layer),
        _const_spec((1, D_MODEL)),
    ]
    out_specs = [hbm_spec if out_native else row_spec, hbm_spec]
    out_shape = [jax.ShapeDtypeStruct((rb, n // rb, D_MODEL) if out_native else (n, D_MODEL), _F32),
                 jax.ShapeDtypeStruct((DEPTH, rb, CONV_WIDTH - 1, D_FF), _F32)]
    scratch = [pltpu.VMEM((carry_rows + rows, FF_CHUNK_MAX), _F32)]
    scratch += _conv_state_scratch(rb, D_FF)
    if p_native:
        scratch += _native_scratch(tb, rb, PLE_DIM)
    if out_native:
        scratch += _native_scratch(tb, rb, D_MODEL)
    return pl.pallas_call(
        functools.partial(_ffn_kernel, rb=rb, tb=tb, layer=layer, zero_conv=f0 is None,
                          single_step=n == rows,
                          final_norm=final_norm, p_native=p_native, out_native=out_native),
        grid=(n // rows,),
        in_specs=in_specs,
        out_specs=out_specs,
        out_shape=out_shape,
        scratch_shapes=scratch,
        input_output_aliases=aliases,
        compiler_params=pltpu.CompilerParams(
            dimension_semantics=("arbitrary",), vmem_limit_bytes=VMEM_LIMIT_BYTES),
        name=f"ffn_l{layer}_rb{rb}",
    )(*operands,
      wts["norm_ffn"], wts["w_up"], wts["ffn_conv_w"], wts["ffn_conv_b"], wts["w_down"],
      wts["norm_ple"], wts["w_ple_gate"], wts["w_ple"], wts["norm_final"])


def _run_trunk(x, p, conv0, sre0, sim0, f0, wts, rb, tb_mixer, tb_ffn, native, cast_names=()):
    mixer_states, ffn_state = [], None
    for layer in range(DEPTH):
        last = layer == DEPTH - 1
        ssm_layer = 0 if conv0 is None else layer
        cast = [wts[k] for k in cast_names] if layer == 0 else []
        x, *mixer_states = _mixer_call(x, conv0, (sre0, sim0), mixer_states, wts, layer, ssm_layer,
                                       rb, tb_mixer[layer], x_native=native and layer == 0, cast=cast)
        if cast:
            wts.update(zip(cast_names, mixer_states[3:]))
            mixer_states = mixer_states[:3]
        x, ffn_state = _ffn_call(x, p, f0, ffn_state, wts, layer, rb, tb_ffn,
                                 final_norm=last, p_native=native, out_native=native and last)
    return (x, *mixer_states, ffn_state)


def kernel(x_prompt, x_sample, p_prompt, p_sample, state_conv_a, state_ssm_re, state_ssm_im, state_ffn_conv, norm_mix, w_in, conv_a_w, conv_a_b, w_out_a, log_dt, lam_re, lam_im, b_re, b_im, c_re, c_im, d_skip, w_glu_a, w_glu_b, w_o, norm_ffn, w_up, ffn_conv_w, ffn_conv_b, w_down, norm_ple, w_ple_gate, w_ple, norm_final):
    abar, b2blk, c2_re, c2_im, fblk = _ssm_prep(log_dt, lam_re, lam_im, b_re, b_im, c_re, c_im)
    wts = {
        "norm_mix": norm_mix,
        "w_in": w_in.astype(_BF16),
        "conv_a_w": conv_a_w,
        "conv_a_b": conv_a_b,
        "w_out_a": w_out_a.astype(_BF16),
        "abar": abar, "b2blk": b2blk, "c2_re": c2_re, "c2_im": c2_im, "fblk": fblk,
        "d_skip": d_skip,
        "w_glu_a": w_glu_a.astype(_BF16),
        "w_glu_b": w_glu_b.astype(_BF16),
        "w_o": w_o.astype(_BF16),
        "norm_ffn": norm_ffn,
        "w_up": w_up,
        "ffn_conv_w": ffn_conv_w,
        "ffn_conv_b": ffn_conv_b,
        "w_down": w_down,
        "norm_ple": norm_ple,
        "w_ple_gate": w_ple_gate,
        "w_ple": w_ple,
        "norm_final": norm_final.reshape(1, D_MODEL),
    }

    bp = x_prompt.shape[0]
    zs = jnp.zeros((1, bp, SSM_FLAT), _F32)
    y_prompt, conv_p, sre_p, sim_p, ffn_p = _run_trunk(
        x_prompt, p_prompt, None, zs, zs, None, wts, bp, MIXER_TIME_BLOCKS, FFN_TIME_BLOCK,
        native=True, cast_names=("w_up", "w_down", "w_ple_gate", "w_ple"))

    bs, sseq, _ = x_sample.shape
    xs = x_sample.transpose(1, 0, 2).reshape(sseq * bs, D_MODEL)
    ps = p_sample.transpose(0, 2, 1, 3).reshape(DEPTH, sseq * bs, PLE_DIM)
    ys, conv_s, sre_s, sim_s, ffn_s = _run_trunk(
        xs, ps, state_conv_a,
        state_ssm_re.reshape(DEPTH, bs, SSM_FLAT), state_ssm_im.reshape(DEPTH, bs, SSM_FLAT),
        state_ffn_conv, wts, bs, (sseq,) * DEPTH, sseq, native=False)
    y_sample = ys.reshape(sseq, bs, D_MODEL).transpose(1, 0, 2)

    def ssm_state(s, b):
        return s.reshape(DEPTH, b, SSM_GROUPS, SSM_STATE)

    return (y_prompt, y_sample,
            conv_p, ssm_state(sre_p, bp), ssm_state(sim_p, bp), ffn_p,
            conv_s, ssm_state(sre_s, bs), ssm_state(sim_s, bs), ffn_s)
```

```python
import functools
import math

import jax
import jax.numpy as jnp
from jax import lax
from jax.experimental import pallas as pl
from jax.experimental.pallas import tpu as pltpu

D_MODEL = 1024
DEPTH = 2
CONV_WIDTH = 3
CONV_A_WIDTH = 512
SSM_WIDTH = 512
SSM_GROUP = 16
SSM_GROUPS = 32
SSM_STATE = 64
SSM_FLAT = SSM_GROUPS * SSM_STATE
D_FF = 2816
PLE_DIM = 256
NORM_EPS = 1e-6
IN_PROJ_WIDTH = 4096

SSM_CHUNKS = 4
CHUNK_CH = SSM_WIDTH // SSM_CHUNKS
CHUNK_ST = SSM_FLAT // SSM_CHUNKS

FF_CHUNKS = ((0, 1024), (1024, 2048), (2048, 2816))
FF_CHUNK_MAX = 1024

MIXER_TIME_BLOCKS = (64, 128)
FFN_TIME_BLOCK = 128
MIXER_MATMUL_WEIGHTS = ("w_in", "w_out_a", "w_glu_a", "w_glu_b", "w_o")
FFN_MATMUL_WEIGHTS = ("w_up", "w_down", "w_ple_gate", "w_ple")
VMEM_LIMIT_BYTES = 56 * 1024 * 1024

_BF16 = jnp.bfloat16
_F32 = jnp.float32


def _dot(a, b):
    return jnp.dot(a, b, preferred_element_type=_F32)


def _dot_nt(a, b):
    return lax.dot_general(a, b, (((1,), (1,)), ((), ())), preferred_element_type=_F32)


def _dot_nt_split(a, b):
    a_hi = a.astype(_BF16)
    b_hi = b.astype(_BF16)
    a_lo = (a - a_hi.astype(_F32)).astype(_BF16)
    b_lo = (b - b_hi.astype(_F32)).astype(_BF16)
    return _dot_nt(a_hi, b_hi) + _dot_nt(a_hi, b_lo) + _dot_nt(a_lo, b_hi)


def _rms(x, g):
    ms = jnp.mean(x * x, axis=-1, keepdims=True)
    return x * lax.rsqrt(ms + NORM_EPS) * g


def _gelu(x):
    c = math.sqrt(2.0 / math.pi)
    t = jnp.tanh(x * (c + (0.044715 * c) * (x * x)))
    return x * (0.5 + 0.5 * t)


def _sigmoid(x):
    return 0.5 * jnp.tanh(0.5 * x) + 0.5


def _seq_copies(hbm_ref, lead, buf, sem, step, slot, tb, to_hbm):
    copies = []
    for b in range(buf.shape[2]):
        hbm_view = hbm_ref.at[lead + (b, pl.ds(step * tb, tb))]
        vmem_view = buf.at[slot, :, b]
        src, dst = (vmem_view, hbm_view) if to_hbm else (hbm_view, vmem_view)
        copies.append(pltpu.make_async_copy(src, dst, sem.at[slot]))
    return copies


def _fetch_time_major(hbm_ref, lead, buf, sem, tb):
    j = pl.program_id(0)
    slot = j % 2

    @pl.when(j == 0)
    def _():
        for c in _seq_copies(hbm_ref, lead, buf, sem, 0, 0, tb, False):
            c.start()

    @pl.when(j + 1 < pl.num_programs(0))
    def _():
        for c in _seq_copies(hbm_ref, lead, buf, sem, j + 1, 1 - slot, tb, False):
            c.start()

    for c in _seq_copies(hbm_ref, lead, buf, sem, j, slot, tb, False):
        c.wait()


def _read_time_major(buf):
    tb, nb, width = buf.shape[1:]
    return buf[pl.program_id(0) % 2].reshape(tb * nb, width)


def _store_batch_major(val, hbm_ref, buf, sem, tb):
    j = pl.program_id(0)
    slot = j % 2
    buf[slot] = val.reshape(tb, buf.shape[2], buf.shape[3])
    for c in _seq_copies(hbm_ref, (), buf, sem, j, slot, tb, True):
        c.start()

    @pl.when(j >= 1)
    def _():
        for c in _seq_copies(hbm_ref, (), buf, sem, j - 1, 1 - slot, tb, True):
            c.wait()

    @pl.when(j == pl.num_programs(0) - 1)
    def _():
        for c in _seq_copies(hbm_ref, (), buf, sem, j, slot, tb, True):
            c.wait()


def _conv_state_copies(hbm_ref, slot, carry, sem, to_hbm):
    copies = []
    for k in range(CONV_WIDTH - 1):
        hbm_view = hbm_ref.at[slot, :, k, :]
        src, dst = (carry.at[k], hbm_view) if to_hbm else (hbm_view, carry.at[k])
        copies.append(pltpu.make_async_copy(src, dst, sem.at[slot, k]))
    return copies


def _at_step(first, single_step, body):
    if single_step:
        body()
    else:
        step = 0 if first else pl.num_programs(0) - 1
        pl.when(pl.program_id(0) == step)(body)


def _conv_state_fetch(state_hbm, layer, carry, sem, single_step, wait):
    def body():
        if state_hbm is None:
            if not wait:
                carry[...] = jnp.zeros(carry.shape, carry.dtype)
        else:
            for c in _conv_state_copies(state_hbm, layer, carry, sem, False):
                c.wait() if wait else c.start()

    _at_step(True, single_step, body)


def _conv_state_store(carry, out_hbm, layer, sem, single_step, wait):
    def body():
        for slot in range(layer, out_hbm.shape[0]):
            for c in _conv_state_copies(out_hbm, slot, carry, sem, True):
                c.wait() if wait else c.start()

    _at_step(False, single_step, body)


def _ssm_prep_kernel(logdt_ref, lre_ref, lim_ref, bre_ref, bim_ref, cre_ref, cim_ref,
                     abar_ref, b2blk_ref, c2_re_ref, c2_im_ref, fblk_ref):
    dt = jnp.exp(logdt_ref[...])
    lre = lre_ref[...]
    lim = lim_ref[...]
    mag = jnp.exp(lre * dt)
    ang = lim * dt
    ar = mag * jnp.cos(ang)
    ai = mag * jnp.sin(ang)
    abar_ref[0:1, :] = ar
    abar_ref[1:2, :] = ai
    abar_ref[2:3, :] = ar * ar - ai * ai
    abar_ref[3:4, :] = 2.0 * (ar * ai)
    den = lre * lre + lim * lim
    nr = ar - 1.0
    coef_re = (nr * lre + ai * lim) / den
    coef_im = (ai * lre - nr * lim) / den
    bre = bre_ref[...]
    bim = bim_ref[...]
    bbar_re = coef_re * bre - coef_im * bim
    bbar_im = coef_re * bim + coef_im * bre
    abb_re = ar * bbar_re - ai * bbar_im
    abb_im = ar * bbar_im + ai * bbar_re
    row = lax.broadcasted_iota(jnp.int32, (CHUNK_CH, CHUNK_ST), 0)
    lane = lax.broadcasted_iota(jnp.int32, (CHUNK_CH, CHUNK_ST), 1)
    same_group = (row // SSM_GROUP) == (lane // SSM_STATE)
    reps = CHUNK_CH // SSM_GROUP
    for c in range(SSM_CHUNKS):
        cols = slice(c * CHUNK_ST, (c + 1) * CHUNK_ST)
        rows = slice(c * CHUNK_CH, (c + 1) * CHUNK_CH)

        def b_block(v, cols=cols):
            return jnp.where(same_group, jnp.concatenate([v[:, cols]] * reps, axis=0), 0.0)

        b_re = b_block(bbar_re)
        b_im = b_block(bbar_im)
        b2blk_ref[c, 0:CHUNK_CH, 0:CHUNK_ST] = b_block(abb_re).astype(_BF16)
        b2blk_ref[c, 0:CHUNK_CH, CHUNK_ST:2 * CHUNK_ST] = b_block(abb_im).astype(_BF16)
        b2blk_ref[c, CHUNK_CH:2 * CHUNK_CH, 0:CHUNK_ST] = b_re.astype(_BF16)
        b2blk_ref[c, CHUNK_CH:2 * CHUNK_CH, CHUNK_ST:2 * CHUNK_ST] = b_im.astype(_BF16)
        c_re = jnp.where(same_group, cre_ref[rows, :], 0.0)
        c_im = jnp.where(same_group, cim_ref[rows, :], 0.0)
        arc = ar[:, cols]
        aic = ai[:, cols]
        c2_re_ref[c, 0:CHUNK_CH, :] = c_re.astype(_BF16)
        c2_re_ref[c, CHUNK_CH:2 * CHUNK_CH, :] = (c_re * arc - c_im * aic).astype(_BF16)
        c2_im_ref[c, 0:CHUNK_CH, :] = (-c_im).astype(_BF16)
        c2_im_ref[c, CHUNK_CH:2 * CHUNK_CH, :] = (-(c_re * aic + c_im * arc)).astype(_BF16)
        feed = _dot_nt_split(b_re, c_re) - _dot_nt_split(b_im, c_im)
        fblk_ref[c] = feed.astype(_BF16)


def _ssm_prep(log_dt, lam_re, lam_im, b_re, b_im, c_re, c_im):
    logdt = jnp.repeat(log_dt, SSM_STATE, axis=1).reshape(DEPTH, 1, SSM_FLAT)
    lre = lam_re.reshape(DEPTH, 1, SSM_FLAT)
    lim = lam_im.reshape(DEPTH, 1, SSM_FLAT)
    bre = b_re.transpose(0, 3, 1, 2).reshape(DEPTH, SSM_GROUP, SSM_FLAT)
    bim = b_im.transpose(0, 3, 1, 2).reshape(DEPTH, SSM_GROUP, SSM_FLAT)
    reps = CHUNK_ST // SSM_STATE
    cre = jnp.tile(c_re.reshape(DEPTH, SSM_WIDTH, SSM_STATE), (1, 1, reps))
    cim = jnp.tile(c_im.reshape(DEPTH, SSM_WIDTH, SSM_STATE), (1, 1, reps))

    def spec(*shape):
        return pl.BlockSpec((None,) + shape, lambda i: (i,) + (0,) * len(shape))

    return pl.pallas_call(
        _ssm_prep_kernel,
        grid=(DEPTH,),
        in_specs=[spec(1, SSM_FLAT), spec(1, SSM_FLAT), spec(1, SSM_FLAT),
                  spec(SSM_GROUP, SSM_FLAT), spec(SSM_GROUP, SSM_FLAT),
                  spec(SSM_WIDTH, CHUNK_ST), spec(SSM_WIDTH, CHUNK_ST)],
        out_specs=[spec(4, SSM_FLAT),
                   spec(SSM_CHUNKS, 2 * CHUNK_CH, 2 * CHUNK_ST),
                   spec(SSM_CHUNKS, 2 * CHUNK_CH, CHUNK_ST),
                   spec(SSM_CHUNKS, 2 * CHUNK_CH, CHUNK_ST),
                   spec(SSM_CHUNKS, CHUNK_CH, CHUNK_CH)],
        out_shape=[jax.ShapeDtypeStruct((DEPTH, 4, SSM_FLAT), _F32),
                   jax.ShapeDtypeStruct((DEPTH, SSM_CHUNKS, 2 * CHUNK_CH, 2 * CHUNK_ST), _BF16),
                   jax.ShapeDtypeStruct((DEPTH, SSM_CHUNKS, 2 * CHUNK_CH, CHUNK_ST), _BF16),
                   jax.ShapeDtypeStruct((DEPTH, SSM_CHUNKS, 2 * CHUNK_CH, CHUNK_ST), _BF16),
                   jax.ShapeDtypeStruct((DEPTH, SSM_CHUNKS, CHUNK_CH, CHUNK_CH), _BF16)],
        name="ssm_prep",
    )(logdt, lre, lim, bre, bim, cre, cim)


def _mixer_kernel(x_ref, *refs, rb, tb, layer, zero_conv, single_step, n_cast):
    refs = list(refs)
    conv0_hbm = None if zero_conv else refs.pop(0)
    sre0_ref, sim0_ref = refs.pop(0), refs.pop(0)
    prev = [refs.pop(0) for _ in range(3)] if layer > 0 else []
    cast_in = [refs.pop(0) for _ in range(n_cast)]
    (g_ref, w_in_ref, cw_ref, cb_ref, w_out_a_ref,
     abar_ref, b2blk_ref, c2_re_ref, c2_im_ref, fblk_ref, dskip_ref,
     glu_a_ref, glu_b_ref, w_o_ref,
     out_ref, conv_out_hbm, sre_stack_ref, sim_stack_ref, *refs) = refs
    cast_out = [refs.pop(0) for _ in range(n_cast)]
    vbuf, s_re, s_im, conv_carry, conv_sem, *x_dma = refs
    for src_ref, dst_ref in zip(cast_in, cast_out):
        dst_ref[...] = src_ref[...].astype(_BF16)
    rows = rb * tb
    carry_rows = (CONV_WIDTH - 1) * rb
    sre_out_ref = sre_stack_ref.at[layer]
    sim_out_ref = sim_stack_ref.at[layer]

    @pl.when(pl.program_id(0) == 0)
    def _():
        for stack_ref, prev_ref in zip((sre_stack_ref, sim_stack_ref), prev[1:]):
            stack_ref[0:layer] = prev_ref[...]
        sre_out_ref[...] = sre0_ref[...]
        sim_out_ref[...] = sim0_ref[...]

    conv_io = (layer, conv_carry, conv_sem, single_step)
    _conv_state_fetch(conv0_hbm, *conv_io, wait=False)
    if not single_step:
        _conv_state_fetch(conv0_hbm, *conv_io, wait=True)
    if x_dma:
        _fetch_time_major(x_ref, (), *x_dma, tb)
        x = _read_time_major(x_dma[0])
    else:
        x = x_ref[...]
    xn = _rms(x, g_ref[layer:layer + 1, :]).astype(_BF16)

    def proj(lo, hi):
        return _dot(xn, w_in_ref[:, lo:hi])

    u = proj(1536, 2048)
    pair = 2 if tb % 2 == 0 else 1
    steps = tb // pair
    srows = steps * rb
    if pair == 2:
        u4 = u.reshape(steps, 2, rb, SSM_WIDTH)
        u_parts = [u4[:, i].reshape(srows, SSM_WIDTH) for i in range(2)]
    else:
        u_parts = [u]
    ub_parts = [v.astype(_BF16) for v in u_parts]
    s_re[0:rb, :] = sre_out_ref[...]
    s_im[0:rb, :] = sim_out_ref[...]
    for c in range(SSM_CHUNKS):
        ch = slice(c * CHUNK_CH, (c + 1) * CHUNK_CH)
        if pair == 2:
            inc = _dot(jnp.concatenate([ub_parts[0][:, ch], ub_parts[1][:, ch]], axis=1), b2blk_ref[c])
        else:
            inc = _dot(ub_parts[0][:, ch], b2blk_ref[c, CHUNK_CH:2 * CHUNK_CH, :])
        s_re[rb:rb + srows, c * CHUNK_ST:(c + 1) * CHUNK_ST] = inc[:, 0:CHUNK_ST]
        s_im[rb:rb + srows, c * CHUNK_ST:(c + 1) * CHUNK_ST] = inc[:, CHUNK_ST:2 * CHUNK_ST]

    a_row = 2 * (pair - 1)
    for c in range(SSM_CHUNKS):
        cols = slice(c * CHUNK_ST, (c + 1) * CHUNK_ST)
        ar = jnp.broadcast_to(abar_ref[a_row:a_row + 1, cols], (rb, CHUNK_ST))
        ai = jnp.broadcast_to(abar_ref[a_row + 1:a_row + 2, cols], (rb, CHUNK_ST))

        def step(k, carry, cols=cols, ar=ar, ai=ai):
            sr, si = carry
            r0 = pl.multiple_of(rb + k * rb, rb)
            nsr = ar * sr - ai * si + s_re[pl.ds(r0, rb), cols]
            nsi = ar * si + ai * sr + s_im[pl.ds(r0, rb), cols]
            s_re[pl.ds(r0, rb), cols] = nsr
            s_im[pl.ds(r0, rb), cols] = nsi
            return nsr, nsi

        sr, si = lax.fori_loop(0, steps, step, (s_re[0:rb, cols], s_im[0:rb, cols]), unroll=True)
        sre_out_ref[:, cols] = sr
        sim_out_ref[:, cols] = si

    h = proj(0, 512)
    gate_c = proj(1024, 1536)
    if single_step:
        _conv_state_fetch(conv0_hbm, *conv_io, wait=True)
    vbuf[0:rb, :] = conv_carry[0]
    vbuf[rb:carry_rows, :] = conv_carry[1]
    vbuf[carry_rows:carry_rows + rows, :] = gate_c * h
    conv_y = (cb_ref[layer:layer + 1, :]
              + vbuf[0:rows, :] * cw_ref[0:1, :]
              + vbuf[rb:rb + rows, :] * cw_ref[1:2, :]
              + vbuf[2 * rb:2 * rb + rows, :] * cw_ref[2:3, :])
    conv_carry[0] = vbuf[rows:rows + rb, :]
    conv_carry[1] = vbuf[rows + rb:rows + carry_rows, :]
    if single_step:
        _conv_state_store(conv_carry, conv_out_hbm, layer, conv_sem, single_step, wait=False)
    gate_b = proj(512, 1024)
    z_a = _dot((gate_b * conv_y).astype(_BF16), w_out_a_ref[...])
    merged = _sigmoid(proj(2048, 3072)) * z_a

    y_parts = [[] for _ in range(pair)]
    for c in range(SSM_CHUNKS):
        ch = slice(c * CHUNK_CH, (c + 1) * CHUNK_CH)
        cols = slice(c * CHUNK_ST, (c + 1) * CHUNK_ST)
        if pair == 2:
            y2 = (_dot_nt(s_re[:, cols].astype(_BF16), c2_re_ref[c])
                  + _dot_nt(s_im[:, cols].astype(_BF16), c2_im_ref[c]))
            y_parts[0].append(y2[0:srows, CHUNK_CH:2 * CHUNK_CH] + _dot(ub_parts[0][:, ch], fblk_ref[c]))
            y_parts[1].append(y2[rb:rb + srows, 0:CHUNK_CH])
        else:
            y_parts[0].append(
                _dot_nt(s_re[rb:rb + srows, cols].astype(_BF16), c2_re_ref[c, 0:CHUNK_CH, :])
                + _dot_nt(s_im[rb:rb + srows, cols].astype(_BF16), c2_im_ref[c, 0:CHUNK_CH, :]))
    d_skip = dskip_ref[layer:layer + 1, :]
    ssm_parts = [jnp.concatenate(yp, axis=1) + d_skip * up for yp, up in zip(y_parts, u_parts)]
    if pair == 2:
        ssm_y = jnp.stack([v.reshape(steps, rb, SSM_WIDTH) for v in ssm_parts],
                          axis=1).reshape(rows, SSM_WIDTH)
    else:
        ssm_y = ssm_parts[0]
    sy = _gelu(ssm_y).astype(_BF16)
    z_b = _dot(sy, glu_a_ref[...]) * _sigmoid(_dot(sy, glu_b_ref[...]))
    merged = (merged + _sigmoid(proj(3072, 4096)) * z_b).astype(_BF16)
    out_ref[...] = x + _dot(merged, w_o_ref[...])
    if not single_step:
        _conv_state_store(conv_carry, conv_out_hbm, layer, conv_sem, single_step, wait=False)
    _conv_state_store(conv_carry, conv_out_hbm, layer, conv_sem, single_step, wait=True)


def _ffn_kernel(x_ref, p_ref, *refs, rb, tb, layer, zero_conv, single_step, final_norm, p_native,
                out_native):
    refs = list(refs)
    f0_hbm = None if zero_conv else refs.pop(0)
    if layer > 0:
        refs.pop(0)
    (g2_ref, w_up_ref, fcw_ref, fcb_ref, w_down_ref,
     g3_ref, w_pg_ref, w_ple_ref, gf_ref,
     out_ref, fconv_out_hbm,
     fbuf, fcarry, fsem, *dma) = refs
    rows = rb * tb
    carry_rows = (CONV_WIDTH - 1) * rb
    p_dma = [dma.pop(0), dma.pop(0)] if p_native else None
    out_dma = [dma.pop(0), dma.pop(0)] if out_native else None

    conv_io = (layer, fcarry, fsem, single_step)
    _conv_state_fetch(f0_hbm, *conv_io, wait=False)
    if not single_step:
        _conv_state_fetch(f0_hbm, *conv_io, wait=True)
    if p_native:
        _fetch_time_major(p_ref, (layer,), *p_dma, tb)

    x = x_ref[...]
    xn = _rms(x, g2_ref[layer:layer + 1, :]).astype(_BF16)
    acc = x
    if single_step:
        _conv_state_fetch(f0_hbm, *conv_io, wait=True)
    for lo, hi in FF_CHUNKS:
        wc = hi - lo
        up_a = _dot(xn, w_up_ref[:, lo:hi])
        up_b = _dot(xn, w_up_ref[:, D_FF + lo:D_FF + hi])
        fbuf[0:rb, 0:wc] = fcarry[0, :, lo:hi]
        fbuf[rb:carry_rows, 0:wc] = fcarry[1, :, lo:hi]
        fbuf[carry_rows:carry_rows + rows, 0:wc] = up_a
        conv_a = (fcb_ref[layer:layer + 1, lo:hi]
                  + fbuf[0:rows, 0:wc] * fcw_ref[0:1, lo:hi]
                  + fbuf[rb:rb + rows, 0:wc] * fcw_ref[1:2, lo:hi]
                  + fbuf[2 * rb:2 * rb + rows, 0:wc] * fcw_ref[2:3, lo:hi])
        fcarry[0, :, lo:hi] = fbuf[rows:rows + rb, 0:wc]
        fcarry[1, :, lo:hi] = fbuf[rows + rb:rows + carry_rows, 0:wc]
        hid = (_gelu(conv_a) * up_b).astype(_BF16)
        acc = acc + _dot(hid, w_down_ref[lo:hi, :])
    if single_step:
        _conv_state_store(fcarry, fconv_out_hbm, layer, fsem, single_step, wait=False)
    x2 = acc
    xn3 = _rms(x2, g3_ref[layer:layer + 1, :]).astype(_BF16)
    gate = _sigmoid(_dot(xn3, w_pg_ref[...]))
    p = _read_time_major(p_dma[0]) if p_native else p_ref[...]
    pe = _dot(p.astype(_BF16), w_ple_ref[...])
    x3 = x2 + gate * pe
    if final_norm:
        x3 = _rms(x3, gf_ref[...])
    if out_native:
        _store_batch_major(x3, out_ref, *out_dma, tb)
    else:
        out_ref[...] = x3
    if not single_step:
        _conv_state_store(fcarry, fconv_out_hbm, layer, fsem, single_step, wait=False)
    _conv_state_store(fcarry, fconv_out_hbm, layer, fsem, single_step, wait=True)


def _const_spec(shape, layer=None):
    if layer is None:
        return pl.BlockSpec(shape, lambda j: (0,) * len(shape), pipeline_mode=pl.Buffered(1))
    return pl.BlockSpec((None,) + shape, lambda j: (layer,) + (0,) * len(shape),
                        pipeline_mode=pl.Buffered(1))


def _layer_weight(wts, name, layer):
    w = wts[name]
    return w[layer] if isinstance(w, list) else (w, layer)


def _native_scratch(tb, rb, width):
    return [pltpu.VMEM((2, tb, rb, width), _F32), pltpu.SemaphoreType.DMA((2,))]


def _stack_spec(n_layers, rb, width):
    return pl.BlockSpec((n_layers, rb, width), lambda j: (0, 0, 0))


def _conv_state_scratch(rb, width):
    return [pltpu.VMEM((CONV_WIDTH - 1, rb, width), _F32), pltpu.SemaphoreType.DMA((DEPTH, CONV_WIDTH - 1))]


def _mixer_call(x, conv0, ssm0, prev, wts, layer, ssm_layer, rb, tb, x_native, cast=()):
    n = x.shape[0] * x.shape[1] if x_native else x.shape[0]
    rows = rb * tb
    carry_rows = (CONV_WIDTH - 1) * rb
    hbm_spec = pl.BlockSpec(memory_space=pl.ANY)
    row_spec = pl.BlockSpec((rows, D_MODEL), lambda j: (j, 0))
    operands = [x]
    in_specs = [hbm_spec if x_native else row_spec]
    if conv0 is not None:
        operands.append(conv0)
        in_specs.append(hbm_spec)
    operands += list(ssm0)
    in_specs += [_const_spec((rb, SSM_FLAT), ssm_layer)] * 2
    aliases = {}
    if prev:
        aliases[len(operands)] = 1
        operands += list(prev)
        in_specs += [hbm_spec] + [_const_spec((layer, rb, SSM_FLAT))] * 2
    steps = n // rows
    cast_out_specs, cast_shapes = [], []
    for w, first, count in cast:
        per = steps // count
        assert steps % count == 0 and w.shape[1] % (16 * per) == 0, (steps, count, w.shape)
        block = (None, w.shape[1] // per, w.shape[2])
        operands.append(w)
        in_specs.append(pl.BlockSpec(block, lambda j, per=per, first=first: (first + j // per, j % per, 0)))
        cast_out_specs.append(pl.BlockSpec(block, lambda j, per=per: (j // per, j % per, 0)))
        cast_shapes.append(jax.ShapeDtypeStruct((count,) + w.shape[1:], _BF16))
    names = ("w_in", "w_out_a", "w_glu_a", "w_glu_b", "w_o")
    (w_in, i_in), (w_out_a, i_out_a), (w_glu_a, i_glu_a), (w_glu_b, i_glu_b), (w_o, i_o) = [
        _layer_weight(wts, k, layer) for k in names]
    in_specs += [
        _const_spec((DEPTH, D_MODEL)),
        _const_spec((D_MODEL, IN_PROJ_WIDTH), i_in),
        _const_spec((CONV_WIDTH, CONV_A_WIDTH), layer),
        _const_spec((DEPTH, CONV_A_WIDTH)),
        _const_spec((CONV_A_WIDTH, D_MODEL), i_out_a),
        _const_spec((4, SSM_FLAT), layer),
        _const_spec((SSM_CHUNKS, 2 * CHUNK_CH, 2 * CHUNK_ST), layer),
        _const_spec((SSM_CHUNKS, 2 * CHUNK_CH, CHUNK_ST), layer),
        _const_spec((SSM_CHUNKS, 2 * CHUNK_CH, CHUNK_ST), layer),
        _const_spec((SSM_CHUNKS, CHUNK_CH, CHUNK_CH), layer),
        _const_spec((DEPTH, SSM_WIDTH)),
        _const_spec((SSM_WIDTH, D_MODEL), i_glu_a),
        _const_spec((SSM_WIDTH, D_MODEL), i_glu_b),
        _const_spec((D_MODEL, D_MODEL), i_o),
    ]
    out_specs = [row_spec, hbm_spec] + [_stack_spec(layer + 1, rb, SSM_FLAT)] * 2 + cast_out_specs
    out_shape = ([jax.ShapeDtypeStruct((n, D_MODEL), _F32),
                  jax.ShapeDtypeStruct((DEPTH, rb, CONV_WIDTH - 1, CONV_A_WIDTH), _F32)]
                 + [jax.ShapeDtypeStruct((layer + 1, rb, SSM_FLAT), _F32)] * 2 + cast_shapes)
    s_rows = rows // 2 if tb % 2 == 0 else rows
    scratch = [
        pltpu.VMEM((carry_rows + rows, CONV_A_WIDTH), _F32),
        pltpu.VMEM((rb + s_rows, SSM_FLAT), _F32),
        pltpu.VMEM((rb + s_rows, SSM_FLAT), _F32),
    ]
    scratch += _conv_state_scratch(rb, CONV_A_WIDTH)
    if x_native:
        scratch += _native_scratch(tb, rb, D_MODEL)
    return pl.pallas_call(
        functools.partial(_mixer_kernel, rb=rb, tb=tb, layer=layer, zero_conv=conv0 is None,
                          single_step=n == rows, n_cast=len(cast)),
        grid=(n // rows,),
        in_specs=in_specs,
        out_specs=out_specs,
        out_shape=out_shape,
        scratch_shapes=scratch,
        input_output_aliases=aliases,
        compiler_params=pltpu.CompilerParams(
            dimension_semantics=("arbitrary",), vmem_limit_bytes=VMEM_LIMIT_BYTES),
        name=f"mixer_l{layer}_rb{rb}",
    )(*operands,
      wts["norm_mix"], w_in, wts["conv_a_w"], wts["conv_a_b"], w_out_a,
      wts["abar"], wts["b2blk"], wts["c2_re"], wts["c2_im"], wts["fblk"], wts["d_skip"],
      w_glu_a, w_glu_b, w_o)


def _ffn_call(x, p, f0, prev, wts, layer, rb, tb, final_norm, p_native, out_native):
    n = x.shape[0]
    rows = rb * tb
    carry_rows = (CONV_WIDTH - 1) * rb
    hbm_spec = pl.BlockSpec(memory_space=pl.ANY)
    row_spec = pl.BlockSpec((rows, D_MODEL), lambda j: (j, 0))
    operands = [x, p]
    in_specs = [
        row_spec,
        hbm_spec if p_native else pl.BlockSpec((None, rows, PLE_DIM), lambda j: (layer, j, 0)),
    ]
    if f0 is not None:
        operands.append(f0)
        in_specs.append(hbm_spec)
    aliases = {}
    if prev is not None:
        aliases[len(operands)] = 1
        operands.append(prev)
        in_specs.append(hbm_spec)
    in_specs += [
        _const_spec((DEPTH, D_MODEL)),
        _const_spec((D_MODEL, 2 * D_FF), layer),
        _const_spec((CONV_WIDTH, D_FF), layer),
        _const_spec((DEPTH, D_FF)),
        _const_spec((D_FF, D_MODEL), layer),
        _const_spec((DEPTH, D_MODEL)),
        _const_spec((D_MODEL, D_MODEL), layer),
        _const_spec((PLE_DIM, D_MODEL), layer),
        _const_spec((1, D_MODEL)),
    ]
    out_specs = [hbm_spec if out_native else row_spec, hbm_spec]
    out_shape = [jax.ShapeDtypeStruct((rb, n // rb, D_MODEL) if out_native else (n, D_MODEL), _F32),
                 jax.ShapeDtypeStruct((DEPTH, rb, CONV_WIDTH - 1, D_FF), _F32)]
    scratch = [pltpu.VMEM((carry_rows + rows, FF_CHUNK_MAX), _F32)]
    scratch += _conv_state_scratch(rb, D_FF)
    if p_native:
        scratch += _native_scratch(tb, rb, PLE_DIM)
    if out_native:
        scratch += _native_scratch(tb, rb, D_MODEL)
    return pl.pallas_call(
        functools.partial(_ffn_kernel, rb=rb, tb=tb, layer=layer, zero_conv=f0 is None,
                          single_step=n == rows,
                          final_norm=final_norm, p_native=p_native, out_native=out_native),
        grid=(n // rows,),
        in_specs=in_specs,
        out_specs=out_specs,
        out_shape=out_shape,
        scratch_shapes=scratch,
        input_output_aliases=aliases,
        compiler_params=pltpu.CompilerParams(
            dimension_semantics=("arbitrary",), vmem_limit_bytes=VMEM_LIMIT_BYTES),
        name=f"ffn_l{layer}_rb{rb}",
    )(*operands,
      wts["norm_ffn"], wts["w_up"], wts["ffn_conv_w"], wts["ffn_conv_b"], wts["w_down"],
      wts["norm_ple"], wts["w_ple_gate"], wts["w_ple"], wts["norm_final"])


def _run_trunk(x, p, conv0, sre0, sim0, f0, wts, rb, tb_mixer, tb_ffn, native, cast_weights=False):
    mixer_states, ffn_state = [], None
    for layer in range(DEPTH):
        last = layer == DEPTH - 1
        ssm_layer = 0 if conv0 is None else layer
        cast = []
        if cast_weights and layer == 0:
            cast = ([(wts[k], 0, DEPTH) for k in FFN_MATMUL_WEIGHTS]
                    + [(wts[k][1], 1, DEPTH - 1) for k in MIXER_MATMUL_WEIGHTS])
        x, *mixer_states = _mixer_call(x, conv0, (sre0, sim0), mixer_states, wts, layer, ssm_layer,
                                       rb, tb_mixer[layer], x_native=native and layer == 0, cast=cast)
        if cast:
            rounded = mixer_states[3:]
            mixer_states = mixer_states[:3]
            wts.update(zip(FFN_MATMUL_WEIGHTS, rounded))
            for k, w in zip(MIXER_MATMUL_WEIGHTS, rounded[len(FFN_MATMUL_WEIGHTS):]):
                wts[k] = [wts[k][0]] + [(w, i) for i in range(DEPTH - 1)]
        x, ffn_state = _ffn_call(x, p, f0, ffn_state, wts, layer, rb, tb_ffn,
                                 final_norm=last, p_native=native, out_native=native and last)
    return (x, *mixer_states, ffn_state)


def kernel(x_prompt, x_sample, p_prompt, p_sample, state_conv_a, state_ssm_re, state_ssm_im, state_ffn_conv, norm_mix, w_in, conv_a_w, conv_a_b, w_out_a, log_dt, lam_re, lam_im, b_re, b_im, c_re, c_im, d_skip, w_glu_a, w_glu_b, w_o, norm_ffn, w_up, ffn_conv_w, ffn_conv_b, w_down, norm_ple, w_ple_gate, w_ple, norm_final):
    abar, b2blk, c2_re, c2_im, fblk = _ssm_prep(log_dt, lam_re, lam_im, b_re, b_im, c_re, c_im)
    wts = {
        "norm_mix": norm_mix,
        "w_in": [(w_in[:1].astype(_BF16), 0), w_in],
        "conv_a_w": conv_a_w,
        "conv_a_b": conv_a_b,
        "w_out_a": [(w_out_a[:1].astype(_BF16), 0), w_out_a],
        "abar": abar, "b2blk": b2blk, "c2_re": c2_re, "c2_im": c2_im, "fblk": fblk,
        "d_skip": d_skip,
        "w_glu_a": [(w_glu_a[:1].astype(_BF16), 0), w_glu_a],
        "w_glu_b": [(w_glu_b[:1].astype(_BF16), 0), w_glu_b],
        "w_o": [(w_o[:1].astype(_BF16), 0), w_o],
        "norm_ffn": norm_ffn,
        "w_up": w_up,
        "ffn_conv_w": ffn_conv_w,
        "ffn_conv_b": ffn_conv_b,
        "w_down": w_down,
        "norm_ple": norm_ple,
        "w_ple_gate": w_ple_gate,
        "w_ple": w_ple,
        "norm_final": norm_final.reshape(1, D_MODEL),
    }

    bp = x_prompt.shape[0]
    zs = jnp.zeros((1, bp, SSM_FLAT), _F32)
    y_prompt, conv_p, sre_p, sim_p, ffn_p = _run_trunk(
        x_prompt, p_prompt, None, zs, zs, None, wts, bp, MIXER_TIME_BLOCKS, FFN_TIME_BLOCK,
        native=True, cast_weights=True)

    bs, sseq, _ = x_sample.shape
    xs = x_sample.transpose(1, 0, 2).reshape(sseq * bs, D_MODEL)
    ps = p_sample.transpose(0, 2, 1, 3).reshape(DEPTH, sseq * bs, PLE_DIM)
    ys, conv_s, sre_s, sim_s, ffn_s = _run_trunk(
        xs, ps, state_conv_a,
        state_ssm_re.reshape(DEPTH, bs, SSM_FLAT), state_ssm_im.reshape(DEPTH, bs, SSM_FLAT),
        state_ffn_conv, wts, bs, (sseq,) * DEPTH, sseq, native=False)
    y_sample = ys.reshape(sseq, bs, D_MODEL).transpose(1, 0, 2)

    def ssm_state(s, b):
        return s.reshape(DEPTH, b, SSM_GROUPS, SSM_STATE)

    return (y_prompt, y_sample,
            conv_p, ssm_state(sre_p, bp), ssm_state(sim_p, bp), ffn_p,
            conv_s, ssm_state(sre_s, bs), ssm_state(sim_s, bs), ffn_s)
```

```python
import functools
import math

import jax
import jax.numpy as jnp
from jax import lax
from jax.experimental import pallas as pl
from jax.experimental.pallas import tpu as pltpu

D_MODEL = 1024
DEPTH = 2
CONV_WIDTH = 3
CONV_A_WIDTH = 512
SSM_WIDTH = 512
SSM_GROUP = 16
SSM_GROUPS = 32
SSM_STATE = 64
SSM_FLAT = SSM_GROUPS * SSM_STATE
D_FF = 2816
PLE_DIM = 256
NORM_EPS = 1e-6
IN_PROJ_WIDTH = 4096

SSM_CHUNKS = 4
CHUNK_CH = SSM_WIDTH // SSM_CHUNKS
CHUNK_ST = SSM_FLAT // SSM_CHUNKS

FF_CHUNKS = ((0, 1024), (1024, 2048), (2048, 2816))
FF_CHUNK_MAX = 1024

MIXER_TIME_BLOCKS = (64, 128)
FFN_TIME_BLOCK = 128
MIXER_MATMUL_WEIGHTS = ("w_in", "w_out_a", "w_glu_a", "w_glu_b", "w_o")
FFN_MATMUL_WEIGHTS = ("w_up", "w_down", "w_ple_gate", "w_ple")
ROUND_CHUNKS = 8
VMEM_LIMIT_BYTES = 56 * 1024 * 1024

_BF16 = jnp.bfloat16
_F32 = jnp.float32


def _dot(a, b):
    return jnp.dot(a, b, preferred_element_type=_F32)


def _dot_nt(a, b):
    return lax.dot_general(a, b, (((1,), (1,)), ((), ())), preferred_element_type=_F32)


def _dot_nt_split(a, b):
    a_hi = a.astype(_BF16)
    b_hi = b.astype(_BF16)
    a_lo = (a - a_hi.astype(_F32)).astype(_BF16)
    b_lo = (b - b_hi.astype(_F32)).astype(_BF16)
    return _dot_nt(a_hi, b_hi) + _dot_nt(a_hi, b_lo) + _dot_nt(a_lo, b_hi)


def _rms(x, g):
    ms = jnp.mean(x * x, axis=-1, keepdims=True)
    return x * lax.rsqrt(ms + NORM_EPS) * g


def _gelu(x):
    c = math.sqrt(2.0 / math.pi)
    t = jnp.tanh(x * (c + (0.044715 * c) * (x * x)))
    return x * (0.5 + 0.5 * t)


def _sigmoid(x):
    return 0.5 * jnp.tanh(0.5 * x) + 0.5


def _seq_copies(hbm_ref, lead, buf, sem, step, slot, tb, to_hbm):
    copies = []
    for b in range(buf.shape[2]):
        hbm_view = hbm_ref.at[lead + (b, pl.ds(step * tb, tb))]
        vmem_view = buf.at[slot, :, b]
        src, dst = (vmem_view, hbm_view) if to_hbm else (hbm_view, vmem_view)
        copies.append(pltpu.make_async_copy(src, dst, sem.at[slot]))
    return copies


def _fetch_time_major(hbm_ref, lead, buf, sem, tb):
    j = pl.program_id(0)
    slot = j % 2

    @pl.when(j == 0)
    def _():
        for c in _seq_copies(hbm_ref, lead, buf, sem, 0, 0, tb, False):
            c.start()

    @pl.when(j + 1 < pl.num_programs(0))
    def _():
        for c in _seq_copies(hbm_ref, lead, buf, sem, j + 1, 1 - slot, tb, False):
            c.start()

    for c in _seq_copies(hbm_ref, lead, buf, sem, j, slot, tb, False):
        c.wait()


def _read_time_major(buf):
    tb, nb, width = buf.shape[1:]
    return buf[pl.program_id(0) % 2].reshape(tb * nb, width)


def _store_batch_major(val, hbm_ref, buf, sem, tb):
    j = pl.program_id(0)
    slot = j % 2
    buf[slot] = val.reshape(tb, buf.shape[2], buf.shape[3])
    for c in _seq_copies(hbm_ref, (), buf, sem, j, slot, tb, True):
        c.start()

    @pl.when(j >= 1)
    def _():
        for c in _seq_copies(hbm_ref, (), buf, sem, j - 1, 1 - slot, tb, True):
            c.wait()

    @pl.when(j == pl.num_programs(0) - 1)
    def _():
        for c in _seq_copies(hbm_ref, (), buf, sem, j, slot, tb, True):
            c.wait()


def _conv_state_copies(hbm_ref, slot, carry, sem, to_hbm):
    copies = []
    for k in range(CONV_WIDTH - 1):
        hbm_view = hbm_ref.at[slot, :, k, :]
        src, dst = (carry.at[k], hbm_view) if to_hbm else (hbm_view, carry.at[k])
        copies.append(pltpu.make_async_copy(src, dst, sem.at[slot, k]))
    return copies


def _at_step(first, single_step, body):
    if single_step:
        body()
    else:
        step = 0 if first else pl.num_programs(0) - 1
        pl.when(pl.program_id(0) == step)(body)


def _conv_state_fetch(state_hbm, layer, carry, sem, single_step, wait):
    def body():
        if state_hbm is None:
            if not wait:
                carry[...] = jnp.zeros(carry.shape, carry.dtype)
        else:
            for c in _conv_state_copies(state_hbm, layer, carry, sem, False):
                c.wait() if wait else c.start()

    _at_step(True, single_step, body)


def _conv_state_store(carry, out_hbm, layer, sem, single_step, wait):
    def body():
        for slot in range(layer, out_hbm.shape[0]):
            for c in _conv_state_copies(out_hbm, slot, carry, sem, True):
                c.wait() if wait else c.start()

    _at_step(False, single_step, body)


def _ssm_prep_kernel(logdt_ref, lre_ref, lim_ref, bre_ref, bim_ref, cre_ref, cim_ref,
                     abar_ref, b2blk_ref, c2_re_ref, c2_im_ref, fblk_ref):
    dt = jnp.exp(logdt_ref[...])
    lre = lre_ref[...]
    lim = lim_ref[...]
    mag = jnp.exp(lre * dt)
    ang = lim * dt
    ar = mag * jnp.cos(ang)
    ai = mag * jnp.sin(ang)
    abar_ref[0:1, :] = ar
    abar_ref[1:2, :] = ai
    abar_ref[2:3, :] = ar * ar - ai * ai
    abar_ref[3:4, :] = 2.0 * (ar * ai)
    den = lre * lre + lim * lim
    nr = ar - 1.0
    coef_re = (nr * lre + ai * lim) / den
    coef_im = (ai * lre - nr * lim) / den
    bre = bre_ref[...]
    bim = bim_ref[...]
    bbar_re = coef_re * bre - coef_im * bim
    bbar_im = coef_re * bim + coef_im * bre
    abb_re = ar * bbar_re - ai * bbar_im
    abb_im = ar * bbar_im + ai * bbar_re
    row = lax.broadcasted_iota(jnp.int32, (CHUNK_CH, CHUNK_ST), 0)
    lane = lax.broadcasted_iota(jnp.int32, (CHUNK_CH, CHUNK_ST), 1)
    same_group = (row // SSM_GROUP) == (lane // SSM_STATE)
    reps = CHUNK_CH // SSM_GROUP
    for c in range(SSM_CHUNKS):
        cols = slice(c * CHUNK_ST, (c + 1) * CHUNK_ST)
        rows = slice(c * CHUNK_CH, (c + 1) * CHUNK_CH)

        def b_block(v, cols=cols):
            return jnp.where(same_group, jnp.concatenate([v[:, cols]] * reps, axis=0), 0.0)

        b_re = b_block(bbar_re)
        b_im = b_block(bbar_im)
        b2blk_ref[c, 0:CHUNK_CH, 0:CHUNK_ST] = b_block(abb_re).astype(_BF16)
        b2blk_ref[c, 0:CHUNK_CH, CHUNK_ST:2 * CHUNK_ST] = b_block(abb_im).astype(_BF16)
        b2blk_ref[c, CHUNK_CH:2 * CHUNK_CH, 0:CHUNK_ST] = b_re.astype(_BF16)
        b2blk_ref[c, CHUNK_CH:2 * CHUNK_CH, CHUNK_ST:2 * CHUNK_ST] = b_im.astype(_BF16)
        c_re = jnp.where(same_group, cre_ref[rows, :], 0.0)
        c_im = jnp.where(same_group, cim_ref[rows, :], 0.0)
        arc = ar[:, cols]
        aic = ai[:, cols]
        c2_re_ref[c, 0:CHUNK_CH, :] = c_re.astype(_BF16)
        c2_re_ref[c, CHUNK_CH:2 * CHUNK_CH, :] = (c_re * arc - c_im * aic).astype(_BF16)
        c2_im_ref[c, 0:CHUNK_CH, :] = (-c_im).astype(_BF16)
        c2_im_ref[c, CHUNK_CH:2 * CHUNK_CH, :] = (-(c_re * aic + c_im * arc)).astype(_BF16)
        feed = _dot_nt_split(b_re, c_re) - _dot_nt_split(b_im, c_im)
        fblk_ref[c] = feed.astype(_BF16)


def _ssm_prep(log_dt, lam_re, lam_im, b_re, b_im, c_re, c_im):
    logdt = jnp.repeat(log_dt, SSM_STATE, axis=1).reshape(DEPTH, 1, SSM_FLAT)
    lre = lam_re.reshape(DEPTH, 1, SSM_FLAT)
    lim = lam_im.reshape(DEPTH, 1, SSM_FLAT)
    bre = b_re.transpose(0, 3, 1, 2).reshape(DEPTH, SSM_GROUP, SSM_FLAT)
    bim = b_im.transpose(0, 3, 1, 2).reshape(DEPTH, SSM_GROUP, SSM_FLAT)
    reps = CHUNK_ST // SSM_STATE
    cre = jnp.tile(c_re.reshape(DEPTH, SSM_WIDTH, SSM_STATE), (1, 1, reps))
    cim = jnp.tile(c_im.reshape(DEPTH, SSM_WIDTH, SSM_STATE), (1, 1, reps))

    def spec(*shape):
        return pl.BlockSpec((None,) + shape, lambda i: (i,) + (0,) * len(shape))

    return pl.pallas_call(
        _ssm_prep_kernel,
        grid=(DEPTH,),
        in_specs=[spec(1, SSM_FLAT), spec(1, SSM_FLAT), spec(1, SSM_FLAT),
                  spec(SSM_GROUP, SSM_FLAT), spec(SSM_GROUP, SSM_FLAT),
                  spec(SSM_WIDTH, CHUNK_ST), spec(SSM_WIDTH, CHUNK_ST)],
        out_specs=[spec(4, SSM_FLAT),
                   spec(SSM_CHUNKS, 2 * CHUNK_CH, 2 * CHUNK_ST),
                   spec(SSM_CHUNKS, 2 * CHUNK_CH, CHUNK_ST),
                   spec(SSM_CHUNKS, 2 * CHUNK_CH, CHUNK_ST),
                   spec(SSM_CHUNKS, CHUNK_CH, CHUNK_CH)],
        out_shape=[jax.ShapeDtypeStruct((DEPTH, 4, SSM_FLAT), _F32),
                   jax.ShapeDtypeStruct((DEPTH, SSM_CHUNKS, 2 * CHUNK_CH, 2 * CHUNK_ST), _BF16),
                   jax.ShapeDtypeStruct((DEPTH, SSM_CHUNKS, 2 * CHUNK_CH, CHUNK_ST), _BF16),
                   jax.ShapeDtypeStruct((DEPTH, SSM_CHUNKS, 2 * CHUNK_CH, CHUNK_ST), _BF16),
                   jax.ShapeDtypeStruct((DEPTH, SSM_CHUNKS, CHUNK_CH, CHUNK_CH), _BF16)],
        name="ssm_prep",
    )(logdt, lre, lim, bre, bim, cre, cim)


def _round_kernel(*refs):
    half = len(refs) // 2
    for src_ref, dst_ref in zip(refs[:half], refs[half:]):
        dst_ref[...] = src_ref[...].astype(_BF16)


def _round_layer0(ws):
    specs = [pl.BlockSpec((1, w.shape[1] // ROUND_CHUNKS, w.shape[2]), lambda j: (0, j, 0)) for w in ws]
    return pl.pallas_call(
        _round_kernel,
        grid=(ROUND_CHUNKS,),
        in_specs=specs,
        out_specs=specs,
        out_shape=[jax.ShapeDtypeStruct((1,) + w.shape[1:], _BF16) for w in ws],
        name="round_mixer_l0",
    )(*ws)


def _mixer_kernel(x_ref, *refs, rb, tb, layer, zero_conv, single_step, n_cast):
    refs = list(refs)
    conv0_hbm = None if zero_conv else refs.pop(0)
    sre0_ref, sim0_ref = refs.pop(0), refs.pop(0)
    prev = [refs.pop(0) for _ in range(3)] if layer > 0 else []
    cast_in = [refs.pop(0) for _ in range(n_cast)]
    (g_ref, w_in_ref, cw_ref, cb_ref, w_out_a_ref,
     abar_ref, b2blk_ref, c2_re_ref, c2_im_ref, fblk_ref, dskip_ref,
     glu_a_ref, glu_b_ref, w_o_ref,
     out_ref, conv_out_hbm, sre_stack_ref, sim_stack_ref, *refs) = refs
    cast_out = [refs.pop(0) for _ in range(n_cast)]
    vbuf, s_re, s_im, conv_carry, conv_sem, *x_dma = refs
    for src_ref, dst_ref in zip(cast_in, cast_out):
        dst_ref[...] = src_ref[...].astype(_BF16)
    rows = rb * tb
    carry_rows = (CONV_WIDTH - 1) * rb
    sre_out_ref = sre_stack_ref.at[layer]
    sim_out_ref = sim_stack_ref.at[layer]

    @pl.when(pl.program_id(0) == 0)
    def _():
        for stack_ref, prev_ref in zip((sre_stack_ref, sim_stack_ref), prev[1:]):
            stack_ref[0:layer] = prev_ref[...]
        sre_out_ref[...] = sre0_ref[...]
        sim_out_ref[...] = sim0_ref[...]

    conv_io = (layer, conv_carry, conv_sem, single_step)
    _conv_state_fetch(conv0_hbm, *conv_io, wait=False)
    if not single_step:
        _conv_state_fetch(conv0_hbm, *conv_io, wait=True)
    if x_dma:
        _fetch_time_major(x_ref, (), *x_dma, tb)
        x = _read_time_major(x_dma[0])
    else:
        x = x_ref[...]
    xn = _rms(x, g_ref[layer:layer + 1, :]).astype(_BF16)

    def proj(lo, hi):
        return _dot(xn, w_in_ref[:, lo:hi])

    u = proj(1536, 2048)
    pair = 2 if tb % 2 == 0 else 1
    steps = tb // pair
    srows = steps * rb
    if pair == 2:
        u4 = u.reshape(steps, 2, rb, SSM_WIDTH)
        u_parts = [u4[:, i].reshape(srows, SSM_WIDTH) for i in range(2)]
    else:
        u_parts = [u]
    ub_parts = [v.astype(_BF16) for v in u_parts]
    s_re[0:rb, :] = sre_out_ref[...]
    s_im[0:rb, :] = sim_out_ref[...]
    for c in range(SSM_CHUNKS):
        ch = slice(c * CHUNK_CH, (c + 1) * CHUNK_CH)
        if pair == 2:
            inc = _dot(jnp.concatenate([ub_parts[0][:, ch], ub_parts[1][:, ch]], axis=1), b2blk_ref[c])
        else:
            inc = _dot(ub_parts[0][:, ch], b2blk_ref[c, CHUNK_CH:2 * CHUNK_CH, :])
        s_re[rb:rb + srows, c * CHUNK_ST:(c + 1) * CHUNK_ST] = inc[:, 0:CHUNK_ST]
        s_im[rb:rb + srows, c * CHUNK_ST:(c + 1) * CHUNK_ST] = inc[:, CHUNK_ST:2 * CHUNK_ST]

    a_row = 2 * (pair - 1)
    for c in range(SSM_CHUNKS):
        cols = slice(c * CHUNK_ST, (c + 1) * CHUNK_ST)
        ar = jnp.broadcast_to(abar_ref[a_row:a_row + 1, cols], (rb, CHUNK_ST))
        ai = jnp.broadcast_to(abar_ref[a_row + 1:a_row + 2, cols], (rb, CHUNK_ST))

        def step(k, carry, cols=cols, ar=ar, ai=ai):
            sr, si = carry
            r0 = pl.multiple_of(rb + k * rb, rb)
            nsr = ar * sr - ai * si + s_re[pl.ds(r0, rb), cols]
            nsi = ar * si + ai * sr + s_im[pl.ds(r0, rb), cols]
            s_re[pl.ds(r0, rb), cols] = nsr
            s_im[pl.ds(r0, rb), cols] = nsi
            return nsr, nsi

        sr, si = lax.fori_loop(0, steps, step, (s_re[0:rb, cols], s_im[0:rb, cols]), unroll=True)
        sre_out_ref[:, cols] = sr
        sim_out_ref[:, cols] = si

    h = proj(0, 512)
    gate_c = proj(1024, 1536)
    if single_step:
        _conv_state_fetch(conv0_hbm, *conv_io, wait=True)
    vbuf[0:rb, :] = conv_carry[0]
    vbuf[rb:carry_rows, :] = conv_carry[1]
    vbuf[carry_rows:carry_rows + rows, :] = gate_c * h
    conv_y = (cb_ref[layer:layer + 1, :]
              + vbuf[0:rows, :] * cw_ref[0:1, :]
              + vbuf[rb:rb + rows, :] * cw_ref[1:2, :]
              + vbuf[2 * rb:2 * rb + rows, :] * cw_ref[2:3, :])
    conv_carry[0] = vbuf[rows:rows + rb, :]
    conv_carry[1] = vbuf[rows + rb:rows + carry_rows, :]
    if single_step:
        _conv_state_store(conv_carry, conv_out_hbm, layer, conv_sem, single_step, wait=False)
    gate_b = proj(512, 1024)
    z_a = _dot((gate_b * conv_y).astype(_BF16), w_out_a_ref[...])
    merged = _sigmoid(proj(2048, 3072)) * z_a

    y_parts = [[] for _ in range(pair)]
    for c in range(SSM_CHUNKS):
        ch = slice(c * CHUNK_CH, (c + 1) * CHUNK_CH)
        cols = slice(c * CHUNK_ST, (c + 1) * CHUNK_ST)
        if pair == 2:
            y2 = (_dot_nt(s_re[:, cols].astype(_BF16), c2_re_ref[c])
                  + _dot_nt(s_im[:, cols].astype(_BF16), c2_im_ref[c]))
            y_parts[0].append(y2[0:srows, CHUNK_CH:2 * CHUNK_CH] + _dot(ub_parts[0][:, ch], fblk_ref[c]))
            y_parts[1].append(y2[rb:rb + srows, 0:CHUNK_CH])
        else:
            y_parts[0].append(
                _dot_nt(s_re[rb:rb + srows, cols].astype(_BF16), c2_re_ref[c, 0:CHUNK_CH, :])
                + _dot_nt(s_im[rb:rb + srows, cols].astype(_BF16), c2_im_ref[c, 0:CHUNK_CH, :]))
    d_skip = dskip_ref[layer:layer + 1, :]
    ssm_parts = [jnp.concatenate(yp, axis=1) + d_skip * up for yp, up in zip(y_parts, u_parts)]
    if pair == 2:
        ssm_y = jnp.stack([v.reshape(steps, rb, SSM_WIDTH) for v in ssm_parts],
                          axis=1).reshape(rows, SSM_WIDTH)
    else:
        ssm_y = ssm_parts[0]
    sy = _gelu(ssm_y).astype(_BF16)
    z_b = _dot(sy, glu_a_ref[...]) * _sigmoid(_dot(sy, glu_b_ref[...]))
    merged = (merged + _sigmoid(proj(3072, 4096)) * z_b).astype(_BF16)
    out_ref[...] = x + _dot(merged, w_o_ref[...])
    if not single_step:
        _conv_state_store(conv_carry, conv_out_hbm, layer, conv_sem, single_step, wait=False)
    _conv_state_store(conv_carry, conv_out_hbm, layer, conv_sem, single_step, wait=True)


def _ffn_kernel(x_ref, p_ref, *refs, rb, tb, layer, zero_conv, single_step, final_norm, p_native,
                out_native):
    refs = list(refs)
    f0_hbm = None if zero_conv else refs.pop(0)
    if layer > 0:
        refs.pop(0)
    (g2_ref, w_up_ref, fcw_ref, fcb_ref, w_down_ref,
     g3_ref, w_pg_ref, w_ple_ref, gf_ref,
     out_ref, fconv_out_hbm,
     fbuf, fcarry, fsem, *dma) = refs
    rows = rb * tb
    carry_rows = (CONV_WIDTH - 1) * rb
    p_dma = [dma.pop(0), dma.pop(0)] if p_native else None
    out_dma = [dma.pop(0), dma.pop(0)] if out_native else None

    conv_io = (layer, fcarry, fsem, single_step)
    _conv_state_fetch(f0_hbm, *conv_io, wait=False)
    if not single_step:
        _conv_state_fetch(f0_hbm, *conv_io, wait=True)
    if p_native:
        _fetch_time_major(p_ref, (layer,), *p_dma, tb)

    x = x_ref[...]
    xn = _rms(x, g2_ref[layer:layer + 1, :]).astype(_BF16)
    acc = x
    if single_step:
        _conv_state_fetch(f0_hbm, *conv_io, wait=True)
    for lo, hi in FF_CHUNKS:
        wc = hi - lo
        up_a = _dot(xn, w_up_ref[:, lo:hi])
        up_b = _dot(xn, w_up_ref[:, D_FF + lo:D_FF + hi])
        fbuf[0:rb, 0:wc] = fcarry[0, :, lo:hi]
        fbuf[rb:carry_rows, 0:wc] = fcarry[1, :, lo:hi]
        fbuf[carry_rows:carry_rows + rows, 0:wc] = up_a
        conv_a = (fcb_ref[layer:layer + 1, lo:hi]
                  + fbuf[0:rows, 0:wc] * fcw_ref[0:1, lo:hi]
                  + fbuf[rb:rb + rows, 0:wc] * fcw_ref[1:2, lo:hi]
                  + fbuf[2 * rb:2 * rb + rows, 0:wc] * fcw_ref[2:3, lo:hi])
        fcarry[0, :, lo:hi] = fbuf[rows:rows + rb, 0:wc]
        fcarry[1, :, lo:hi] = fbuf[rows + rb:rows + carry_rows, 0:wc]
        hid = (_gelu(conv_a) * up_b).astype(_BF16)
        acc = acc + _dot(hid, w_down_ref[lo:hi, :])
    if single_step:
        _conv_state_store(fcarry, fconv_out_hbm, layer, fsem, single_step, wait=False)
    x2 = acc
    xn3 = _rms(x2, g3_ref[layer:layer + 1, :]).astype(_BF16)
    gate = _sigmoid(_dot(xn3, w_pg_ref[...]))
    p = _read_time_major(p_dma[0]) if p_native else p_ref[...]
    pe = _dot(p.astype(_BF16), w_ple_ref[...])
    x3 = x2 + gate * pe
    if final_norm:
        x3 = _rms(x3, gf_ref[...])
    if out_native:
        _store_batch_major(x3, out_ref, *out_dma, tb)
    else:
        out_ref[...] = x3
    if not single_step:
        _conv_state_store(fcarry, fconv_out_hbm, layer, fsem, single_step, wait=False)
    _conv_state_store(fcarry, fconv_out_hbm, layer, fsem, single_step, wait=True)


def _const_spec(shape, layer=None):
    if layer is None:
        return pl.BlockSpec(shape, lambda j: (0,) * len(shape), pipeline_mode=pl.Buffered(1))
    return pl.BlockSpec((None,) + shape, lambda j: (layer,) + (0,) * len(shape),
                        pipeline_mode=pl.Buffered(1))


def _layer_weight(wts, name, layer):
    w = wts[name]
    return w[layer] if isinstance(w, list) else (w, layer)


def _native_scratch(tb, rb, width):
    return [pltpu.VMEM((2, tb, rb, width), _F32), pltpu.SemaphoreType.DMA((2,))]


def _stack_spec(n_layers, rb, width):
    return pl.BlockSpec((n_layers, rb, width), lambda j: (0, 0, 0))


def _conv_state_scratch(rb, width):
    return [pltpu.VMEM((CONV_WIDTH - 1, rb, width), _F32), pltpu.SemaphoreType.DMA((DEPTH, CONV_WIDTH - 1))]


def _mixer_call(x, conv0, ssm0, prev, wts, layer, ssm_layer, rb, tb, x_native, cast=()):
    n = x.shape[0] * x.shape[1] if x_native else x.shape[0]
    rows = rb * tb
    carry_rows = (CONV_WIDTH - 1) * rb
    hbm_spec = pl.BlockSpec(memory_space=pl.ANY)
    row_spec = pl.BlockSpec((rows, D_MODEL), lambda j: (j, 0))
    operands = [x]
    in_specs = [hbm_spec if x_native else row_spec]
    if conv0 is not None:
        operands.append(conv0)
        in_specs.append(hbm_spec)
    operands += list(ssm0)
    in_specs += [_const_spec((rb, SSM_FLAT), ssm_layer)] * 2
    aliases = {}
    if prev:
        aliases[len(operands)] = 1
        operands += list(prev)
        in_specs += [hbm_spec] + [_const_spec((layer, rb, SSM_FLAT))] * 2
    steps = n // rows
    cast_out_specs, cast_shapes = [], []
    for w, first, count in cast:
        per = steps // count
        assert steps % count == 0 and w.shape[1] % (16 * per) == 0, (steps, count, w.shape)
        block = (None, w.shape[1] // per, w.shape[2])
        operands.append(w)
        in_specs.append(pl.BlockSpec(block, lambda j, per=per, first=first: (first + j // per, j % per, 0)))
        cast_out_specs.append(pl.BlockSpec(block, lambda j, per=per: (j // per, j % per, 0)))
        cast_shapes.append(jax.ShapeDtypeStruct((count,) + w.shape[1:], _BF16))
    names = ("w_in", "w_out_a", "w_glu_a", "w_glu_b", "w_o")
    (w_in, i_in), (w_out_a, i_out_a), (w_glu_a, i_glu_a), (w_glu_b, i_glu_b), (w_o, i_o) = [
        _layer_weight(wts, k, layer) for k in names]
    in_specs += [
        _const_spec((DEPTH, D_MODEL)),
        _const_spec((D_MODEL, IN_PROJ_WIDTH), i_in),
        _const_spec((CONV_WIDTH, CONV_A_WIDTH), layer),
        _const_spec((DEPTH, CONV_A_WIDTH)),
        _const_spec((CONV_A_WIDTH, D_MODEL), i_out_a),
        _const_spec((4, SSM_FLAT), layer),
        _const_spec((SSM_CHUNKS, 2 * CHUNK_CH, 2 * CHUNK_ST), layer),
        _const_spec((SSM_CHUNKS, 2 * CHUNK_CH, CHUNK_ST), layer),
        _const_spec((SSM_CHUNKS, 2 * CHUNK_CH, CHUNK_ST), layer),
        _const_spec((SSM_CHUNKS, CHUNK_CH, CHUNK_CH), layer),
        _const_spec((DEPTH, SSM_WIDTH)),
        _const_spec((SSM_WIDTH, D_MODEL), i_glu_a),
        _const_spec((SSM_WIDTH, D_MODEL), i_glu_b),
        _const_spec((D_MODEL, D_MODEL), i_o),
    ]
    out_specs = [row_spec, hbm_spec] + [_stack_spec(layer + 1, rb, SSM_FLAT)] * 2 + cast_out_specs
    out_shape = ([jax.ShapeDtypeStruct((n, D_MODEL), _F32),
                  jax.ShapeDtypeStruct((DEPTH, rb, CONV_WIDTH - 1, CONV_A_WIDTH), _F32)]
                 + [jax.ShapeDtypeStruct((layer + 1, rb, SSM_FLAT), _F32)] * 2 + cast_shapes)
    s_rows = rows // 2 if tb % 2 == 0 else rows
    scratch = [
        pltpu.VMEM((carry_rows + rows, CONV_A_WIDTH), _F32),
        pltpu.VMEM((rb + s_rows, SSM_FLAT), _F32),
        pltpu.VMEM((rb + s_rows, SSM_FLAT), _F32),
    ]
    scratch += _conv_state_scratch(rb, CONV_A_WIDTH)
    if x_native:
        scratch += _native_scratch(tb, rb, D_MODEL)
    return pl.pallas_call(
        functools.partial(_mixer_kernel, rb=rb, tb=tb, layer=layer, zero_conv=conv0 is None,
                          single_step=n == rows, n_cast=len(cast)),
        grid=(n // rows,),
        in_specs=in_specs,
        out_specs=out_specs,
        out_shape=out_shape,
        scratch_shapes=scratch,
        input_output_aliases=aliases,
        compiler_params=pltpu.CompilerParams(
            dimension_semantics=("arbitrary",), vmem_limit_bytes=VMEM_LIMIT_BYTES),
        name=f"mixer_l{layer}_rb{rb}",
    )(*operands,
      wts["norm_mix"], w_in, wts["conv_a_w"], wts["conv_a_b"], w_out_a,
      wts["abar"], wts["b2blk"], wts["c2_re"], wts["c2_im"], wts["fblk"], wts["d_skip"],
      w_glu_a, w_glu_b, w_o)


def _ffn_call(x, p, f0, prev, wts, layer, rb, tb, final_norm, p_native, out_native):
    n = x.shape[0]
    rows = rb * tb
    carry_rows = (CONV_WIDTH - 1) * rb
    hbm_spec = pl.BlockSpec(memory_space=pl.ANY)
    row_spec = pl.BlockSpec((rows, D_MODEL), lambda j: (j, 0))
    operands = [x, p]
    in_specs = [
        row_spec,
        hbm_spec if p_native else pl.BlockSpec((None, rows, PLE_DIM), lambda j: (layer, j, 0)),
    ]
    if f0 is not None:
        operands.append(f0)
        in_specs.append(hbm_spec)
    aliases = {}
    if prev is not None:
        aliases[len(operands)] = 1
        operands.append(prev)
        in_specs.append(hbm_spec)
    in_specs += [
        _const_spec((DEPTH, D_MODEL)),
        _const_spec((D_MODEL, 2 * D_FF), layer),
        _const_spec((CONV_WIDTH, D_FF), layer),
        _const_spec((DEPTH, D_FF)),
        _const_spec((D_FF, D_MODEL), layer),
        _const_spec((DEPTH, D_MODEL)),
        _const_spec((D_MODEL, D_MODEL), layer),
        _const_spec((PLE_DIM, D_MODEL), layer),
        _const_spec((1, D_MODEL)),
    ]
    out_specs = [hbm_spec if out_native else row_spec, hbm_spec]
    out_shape = [jax.ShapeDtypeStruct((rb, n // rb, D_MODEL) if out_native else (n, D_MODEL), _F32),
                 jax.ShapeDtypeStruct((DEPTH, rb, CONV_WIDTH - 1, D_FF), _F32)]
    scratch = [pltpu.VMEM((carry_rows + rows, FF_CHUNK_MAX), _F32)]
    scratch += _conv_state_scratch(rb, D_FF)
    if p_native:
        scratch += _native_scratch(tb, rb, PLE_DIM)
    if out_native:
        scratch += _native_scratch(tb, rb, D_MODEL)
    return pl.pallas_call(
        functools.partial(_ffn_kernel, rb=rb, tb=tb, layer=layer, zero_conv=f0 is None,
                          single_step=n == rows,
                          final_norm=final_norm, p_native=p_native, out_native=out_native),
        grid=(n // rows,),
        in_specs=in_specs,
        out_specs=out_specs,
        out_shape=out_shape,
        scratch_shapes=scratch,
        input_output_aliases=aliases,
        compiler_params=pltpu.CompilerParams(
            dimension_semantics=("arbitrary",), vmem_limit_bytes=VMEM_LIMIT_BYTES),
        name=f"ffn_l{layer}_rb{rb}",
    )(*operands,
      wts["norm_ffn"], wts["w_up"], wts["ffn_conv_w"], wts["ffn_conv_b"], wts["w_down"],
      wts["norm_ple"], wts["w_ple_gate"], wts["w_ple"], wts["norm_final"])


def _run_trunk(x, p, conv0, sre0, sim0, f0, wts, rb, tb_mixer, tb_ffn, native, cast_weights=False):
    mixer_states, ffn_state = [], None
    for layer in range(DEPTH):
        last = layer == DEPTH - 1
        ssm_layer = 0 if conv0 is None else layer
        cast = []
        if cast_weights and layer == 0:
            cast = ([(wts[k], 0, DEPTH) for k in FFN_MATMUL_WEIGHTS]
                    + [(wts[k][1], 1, DEPTH - 1) for k in MIXER_MATMUL_WEIGHTS])
        x, *mixer_states = _mixer_call(x, conv0, (sre0, sim0), mixer_states, wts, layer, ssm_layer,
                                       rb, tb_mixer[layer], x_native=native and layer == 0, cast=cast)
        if cast:
            rounded = mixer_states[3:]
            mixer_states = mixer_states[:3]
            wts.update(zip(FFN_MATMUL_WEIGHTS, rounded))
            for k, w in zip(MIXER_MATMUL_WEIGHTS, rounded[len(FFN_MATMUL_WEIGHTS):]):
                wts[k] = [wts[k][0]] + [(w, i) for i in range(DEPTH - 1)]
        x, ffn_state = _ffn_call(x, p, f0, ffn_state, wts, layer, rb, tb_ffn,
                                 final_norm=last, p_native=native, out_native=native and last)
    return (x, *mixer_states, ffn_state)


def kernel(x_prompt, x_sample, p_prompt, p_sample, state_conv_a, state_ssm_re, state_ssm_im, state_ffn_conv, norm_mix, w_in, conv_a_w, conv_a_b, w_out_a, log_dt, lam_re, lam_im, b_re, b_im, c_re, c_im, d_skip, w_glu_a, w_glu_b, w_o, norm_ffn, w_up, ffn_conv_w, ffn_conv_b, w_down, norm_ple, w_ple_gate, w_ple, norm_final):
    abar, b2blk, c2_re, c2_im, fblk = _ssm_prep(log_dt, lam_re, lam_im, b_re, b_im, c_re, c_im)
    mixer_f32 = (w_in, w_out_a, w_glu_a, w_glu_b, w_o)
    w_in_0, w_out_a_0, w_glu_a_0, w_glu_b_0, w_o_0 = _round_layer0(mixer_f32)
    wts = {
        "norm_mix": norm_mix,
        "w_in": [(w_in_0, 0), w_in],
        "conv_a_w": conv_a_w,
        "conv_a_b": conv_a_b,
        "w_out_a": [(w_out_a_0, 0), w_out_a],
        "abar": abar, "b2blk": b2blk, "c2_re": c2_re, "c2_im": c2_im, "fblk": fblk,
        "d_skip": d_skip,
        "w_glu_a": [(w_glu_a_0, 0), w_glu_a],
        "w_glu_b": [(w_glu_b_0, 0), w_glu_b],
        "w_o": [(w_o_0, 0), w_o],
        "norm_ffn": norm_ffn,
        "w_up": w_up,
        "ffn_conv_w": ffn_conv_w,
        "ffn_conv_b": ffn_conv_b,
        "w_down": w_down,
        "norm_ple": norm_ple,
        "w_ple_gate": w_ple_gate,
        "w_ple": w_ple,
        "norm_final": norm_final.reshape(1, D_MODEL),
    }

    bp = x_prompt.shape[0]
    zs = jnp.zeros((1, bp, SSM_FLAT), _F32)
    y_prompt, conv_p, sre_p, sim_p, ffn_p = _run_trunk(
        x_prompt, p_prompt, None, zs, zs, None, wts, bp, MIXER_TIME_BLOCKS, FFN_TIME_BLOCK,
        native=True, cast_weights=True)

    bs, sseq, _ = x_sample.shape
    xs = x_sample.transpose(1, 0, 2).reshape(sseq * bs, D_MODEL)
    ps = p_sample.transpose(0, 2, 1, 3).reshape(DEPTH, sseq * bs, PLE_DIM)
    ys, conv_s, sre_s, sim_s, ffn_s = _run_trunk(
        xs, ps, state_conv_a,
        state_ssm_re.reshape(DEPTH, bs, SSM_FLAT), state_ssm_im.reshape(DEPTH, bs, SSM_FLAT),
        state_ffn_conv, wts, bs, (sseq,) * DEPTH, sseq, native=False)
    y_sample = ys.reshape(sseq, bs, D_MODEL).transpose(1, 0, 2)

    def ssm_state(s, b):
        return s.reshape(DEPTH, b, SSM_GROUPS, SSM_STATE)

    return (y_prompt, y_sample,
            conv_p, ssm_state(sre_p, bp), ssm_state(sim_p, bp), ffn_p,
            conv_s, ssm_state(sre_s, bs), ssm_state(sim_s, bs), ffn_s)
```

```python
import functools
import math

import jax
import jax.numpy as jnp
from jax import lax
from jax.experimental import pallas as pl
from jax.experimental.pallas import tpu as pltpu

D_MODEL = 1024
DEPTH = 2
CONV_WIDTH = 3
CONV_A_WIDTH = 512
SSM_WIDTH = 512
SSM_GROUP = 16
SSM_GROUPS = 32
SSM_STATE = 64
SSM_FLAT = SSM_GROUPS * SSM_STATE
D_FF = 2816
PLE_DIM = 256
NORM_EPS = 1e-6
IN_PROJ_WIDTH = 4096

SSM_CHUNKS = 4
CHUNK_CH = SSM_WIDTH // SSM_CHUNKS
CHUNK_ST = SSM_FLAT // SSM_CHUNKS

FF_CHUNKS = ((0, 1024), (1024, 2048), (2048, 2816))
FF_CHUNK_MAX = 1024

MIXER_TIME_BLOCKS = (64, 128)
FFN_TIME_BLOCK = 128
MIXER_MATMUL_WEIGHTS = ("w_in", "w_out_a", "w_glu_a", "w_glu_b", "w_o")
FFN_MATMUL_WEIGHTS = ("w_up", "w_down", "w_ple_gate", "w_ple")
ROUND_CHUNKS = 8
VMEM_LIMIT_BYTES = 56 * 1024 * 1024

_BF16 = jnp.bfloat16
_F32 = jnp.float32


def _dot(a, b):
    return jnp.dot(a, b, preferred_element_type=_F32)


def _dot_nt(a, b):
    return lax.dot_general(a, b, (((1,), (1,)), ((), ())), preferred_element_type=_F32)


def _dot_nt_split(a, b):
    a_hi = a.astype(_BF16)
    b_hi = b.astype(_BF16)
    a_lo = (a - a_hi.astype(_F32)).astype(_BF16)
    b_lo = (b - b_hi.astype(_F32)).astype(_BF16)
    return _dot_nt(a_hi, b_hi) + _dot_nt(a_hi, b_lo) + _dot_nt(a_lo, b_hi)


def _rms(x, g):
    ms = jnp.mean(x * x, axis=-1, keepdims=True)
    return x * lax.rsqrt(ms + NORM_EPS) * g


def _gelu(x):
    c = math.sqrt(2.0 / math.pi)
    t = jnp.tanh(x * (c + (0.044715 * c) * (x * x)))
    return x * (0.5 + 0.5 * t)


def _sigmoid(x):
    return 0.5 * jnp.tanh(0.5 * x) + 0.5


def _seq_copies(hbm_ref, lead, buf, sem, step, slot, tb, to_hbm):
    copies = []
    for b in range(buf.shape[2]):
        hbm_view = hbm_ref.at[lead + (b, pl.ds(step * tb, tb))]
        vmem_view = buf.at[slot, :, b]
        src, dst = (vmem_view, hbm_view) if to_hbm else (hbm_view, vmem_view)
        copies.append(pltpu.make_async_copy(src, dst, sem.at[slot]))
    return copies


def _fetch_time_major(hbm_ref, lead, buf, sem, tb):
    j = pl.program_id(0)
    slot = j % 2

    @pl.when(j == 0)
    def _():
        for c in _seq_copies(hbm_ref, lead, buf, sem, 0, 0, tb, False):
            c.start()

    @pl.when(j + 1 < pl.num_programs(0))
    def _():
        for c in _seq_copies(hbm_ref, lead, buf, sem, j + 1, 1 - slot, tb, False):
            c.start()

    for c in _seq_copies(hbm_ref, lead, buf, sem, j, slot, tb, False):
        c.wait()


def _read_time_major(buf):
    tb, nb, width = buf.shape[1:]
    return buf[pl.program_id(0) % 2].reshape(tb * nb, width)


def _store_batch_major(val, hbm_ref, buf, sem, tb):
    j = pl.program_id(0)
    slot = j % 2
    buf[slot] = val.reshape(tb, buf.shape[2], buf.shape[3])
    for c in _seq_copies(hbm_ref, (), buf, sem, j, slot, tb, True):
        c.start()

    @pl.when(j >= 1)
    def _():
        for c in _seq_copies(hbm_ref, (), buf, sem, j - 1, 1 - slot, tb, True):
            c.wait()

    @pl.when(j == pl.num_programs(0) - 1)
    def _():
        for c in _seq_copies(hbm_ref, (), buf, sem, j, slot, tb, True):
            c.wait()


def _conv_state_copies(hbm_ref, slot, carry, sem, to_hbm):
    copies = []
    for k in range(CONV_WIDTH - 1):
        hbm_view = hbm_ref.at[slot, :, k, :]
        src, dst = (carry.at[k], hbm_view) if to_hbm else (hbm_view, carry.at[k])
        copies.append(pltpu.make_async_copy(src, dst, sem.at[slot, k]))
    return copies


def _at_step(first, single_step, body):
    if single_step:
        body()
    else:
        step = 0 if first else pl.num_programs(0) - 1
        pl.when(pl.program_id(0) == step)(body)


def _conv_state_fetch(state_hbm, layer, carry, sem, single_step, wait):
    def body():
        if state_hbm is None:
            if not wait:
                carry[...] = jnp.zeros(carry.shape, carry.dtype)
        else:
            for c in _conv_state_copies(state_hbm, layer, carry, sem, False):
                c.wait() if wait else c.start()

    _at_step(True, single_step, body)


def _conv_state_store(carry, out_hbm, layer, sem, single_step, wait):
    def body():
        for slot in range(layer, out_hbm.shape[0]):
            for c in _conv_state_copies(out_hbm, slot, carry, sem, True):
                c.wait() if wait else c.start()

    _at_step(False, single_step, body)


def _ssm_prep_kernel(logdt_ref, lre_ref, lim_ref, bre_ref, bim_ref, cre_ref, cim_ref,
                     abar_ref, b2blk_ref, c2_re_ref, c2_im_ref, fblk_ref):
    dt = jnp.exp(logdt_ref[...])
    lre = lre_ref[...]
    lim = lim_ref[...]
    mag = jnp.exp(lre * dt)
    ang = lim * dt
    ar = mag * jnp.cos(ang)
    ai = mag * jnp.sin(ang)
    abar_ref[0:1, :] = ar
    abar_ref[1:2, :] = ai
    abar_ref[2:3, :] = ar * ar - ai * ai
    abar_ref[3:4, :] = 2.0 * (ar * ai)
    den = lre * lre + lim * lim
    nr = ar - 1.0
    coef_re = (nr * lre + ai * lim) / den
    coef_im = (ai * lre - nr * lim) / den
    bre = bre_ref[...]
    bim = bim_ref[...]
    bbar_re = coef_re * bre - coef_im * bim
    bbar_im = coef_re * bim + coef_im * bre
    abb_re = ar * bbar_re - ai * bbar_im
    abb_im = ar * bbar_im + ai * bbar_re
    row = lax.broadcasted_iota(jnp.int32, (CHUNK_CH, CHUNK_ST), 0)
    lane = lax.broadcasted_iota(jnp.int32, (CHUNK_CH, CHUNK_ST), 1)
    same_group = (row // SSM_GROUP) == (lane // SSM_STATE)
    reps = CHUNK_CH // SSM_GROUP
    for c in range(SSM_CHUNKS):
        cols = slice(c * CHUNK_ST, (c + 1) * CHUNK_ST)
        rows = slice(c * CHUNK_CH, (c + 1) * CHUNK_CH)

        def b_block(v, cols=cols):
            return jnp.where(same_group, jnp.concatenate([v[:, cols]] * reps, axis=0), 0.0)

        b_re = b_block(bbar_re)
        b_im = b_block(bbar_im)
        b2blk_ref[c, 0:CHUNK_CH, 0:CHUNK_ST] = b_block(abb_re).astype(_BF16)
        b2blk_ref[c, 0:CHUNK_CH, CHUNK_ST:2 * CHUNK_ST] = b_block(abb_im).astype(_BF16)
        b2blk_ref[c, CHUNK_CH:2 * CHUNK_CH, 0:CHUNK_ST] = b_re.astype(_BF16)
        b2blk_ref[c, CHUNK_CH:2 * CHUNK_CH, CHUNK_ST:2 * CHUNK_ST] = b_im.astype(_BF16)
        c_re = jnp.where(same_group, cre_ref[rows, :], 0.0)
        c_im = jnp.where(same_group, cim_ref[rows, :], 0.0)
        arc = ar[:, cols]
        aic = ai[:, cols]
        c2_re_ref[c, 0:CHUNK_CH, :] = c_re.astype(_BF16)
        c2_re_ref[c, CHUNK_CH:2 * CHUNK_CH, :] = (c_re * arc - c_im * aic).astype(_BF16)
        c2_im_ref[c, 0:CHUNK_CH, :] = (-c_im).astype(_BF16)
        c2_im_ref[c, CHUNK_CH:2 * CHUNK_CH, :] = (-(c_re * aic + c_im * arc)).astype(_BF16)
        feed = _dot_nt_split(b_re, c_re) - _dot_nt_split(b_im, c_im)
        fblk_ref[c] = feed.astype(_BF16)


def _ssm_prep(log_dt, lam_re, lam_im, b_re, b_im, c_re, c_im):
    logdt = jnp.repeat(log_dt, SSM_STATE, axis=1).reshape(DEPTH, 1, SSM_FLAT)
    lre = lam_re.reshape(DEPTH, 1, SSM_FLAT)
    lim = lam_im.reshape(DEPTH, 1, SSM_FLAT)
    bre = b_re.transpose(0, 3, 1, 2).reshape(DEPTH, SSM_GROUP, SSM_FLAT)
    bim = b_im.transpose(0, 3, 1, 2).reshape(DEPTH, SSM_GROUP, SSM_FLAT)
    reps = CHUNK_ST // SSM_STATE
    cre = jnp.tile(c_re.reshape(DEPTH, SSM_WIDTH, SSM_STATE), (1, 1, reps))
    cim = jnp.tile(c_im.reshape(DEPTH, SSM_WIDTH, SSM_STATE), (1, 1, reps))

    def spec(*shape):
        return pl.BlockSpec((None,) + shape, lambda i: (i,) + (0,) * len(shape))

    return pl.pallas_call(
        _ssm_prep_kernel,
        grid=(DEPTH,),
        in_specs=[spec(1, SSM_FLAT), spec(1, SSM_FLAT), spec(1, SSM_FLAT),
                  spec(SSM_GROUP, SSM_FLAT), spec(SSM_GROUP, SSM_FLAT),
                  spec(SSM_WIDTH, CHUNK_ST), spec(SSM_WIDTH, CHUNK_ST)],
        out_specs=[spec(4, SSM_FLAT),
                   spec(SSM_CHUNKS, 2 * CHUNK_CH, 2 * CHUNK_ST),
                   spec(SSM_CHUNKS, 2 * CHUNK_CH, CHUNK_ST),
                   spec(SSM_CHUNKS, 2 * CHUNK_CH, CHUNK_ST),
                   spec(SSM_CHUNKS, CHUNK_CH, CHUNK_CH)],
        out_shape=[jax.ShapeDtypeStruct((DEPTH, 4, SSM_FLAT), _F32),
                   jax.ShapeDtypeStruct((DEPTH, SSM_CHUNKS, 2 * CHUNK_CH, 2 * CHUNK_ST), _BF16),
                   jax.ShapeDtypeStruct((DEPTH, SSM_CHUNKS, 2 * CHUNK_CH, CHUNK_ST), _BF16),
                   jax.ShapeDtypeStruct((DEPTH, SSM_CHUNKS, 2 * CHUNK_CH, CHUNK_ST), _BF16),
                   jax.ShapeDtypeStruct((DEPTH, SSM_CHUNKS, CHUNK_CH, CHUNK_CH), _BF16)],
        name="ssm_prep",
    )(logdt, lre, lim, bre, bim, cre, cim)


def _round_kernel(*refs):
    half = len(refs) // 2
    for src_ref, dst_ref in zip(refs[:half], refs[half:]):
        dst_ref[...] = src_ref[...].astype(_BF16)


def _round_layer0(ws):
    specs = [pl.BlockSpec((1, w.shape[1] // ROUND_CHUNKS, w.shape[2]), lambda j: (0, j, 0)) for w in ws]
    return pl.pallas_call(
        _round_kernel,
        grid=(ROUND_CHUNKS,),
        in_specs=specs,
        out_specs=specs,
        out_shape=[jax.ShapeDtypeStruct((1,) + w.shape[1:], _BF16) for w in ws],
        name="round_mixer_l0",
    )(*ws)


def _mixer_kernel(x_ref, *refs, rb, tb, layer, zero_conv, single_step, n_cast):
    refs = list(refs)
    conv0_hbm = None if zero_conv else refs.pop(0)
    sre0_ref, sim0_ref = refs.pop(0), refs.pop(0)
    prev = [refs.pop(0) for _ in range(3)] if layer > 0 else []
    cast_in = [refs.pop(0) for _ in range(n_cast)]
    (g_ref, w_in_ref, cw_ref, cb_ref, w_out_a_ref,
     abar_ref, b2blk_ref, c2_re_ref, c2_im_ref, fblk_ref, dskip_ref,
     glu_a_ref, glu_b_ref, w_o_ref,
     out_ref, conv_out_hbm, sre_stack_ref, sim_stack_ref, *refs) = refs
    cast_out = [refs.pop(0) for _ in range(n_cast)]
    vbuf, s_re, s_im, conv_carry, conv_sem, *x_dma = refs
    for src_ref, dst_ref in zip(cast_in, cast_out):
        dst_ref[...] = src_ref[...].astype(_BF16)
    rows = rb * tb
    carry_rows = (CONV_WIDTH - 1) * rb
    sre_out_ref = sre_stack_ref.at[layer]
    sim_out_ref = sim_stack_ref.at[layer]

    @pl.when(pl.program_id(0) == 0)
    def _():
        for stack_ref, prev_ref in zip((sre_stack_ref, sim_stack_ref), prev[1:]):
            stack_ref[0:layer] = prev_ref[...]
        sre_out_ref[...] = sre0_ref[...]
        sim_out_ref[...] = sim0_ref[...]

    conv_io = (layer, conv_carry, conv_sem, single_step)
    _conv_state_fetch(conv0_hbm, *conv_io, wait=False)
    if not single_step:
        _conv_state_fetch(conv0_hbm, *conv_io, wait=True)
    if x_dma:
        _fetch_time_major(x_ref, (), *x_dma, tb)
        x = _read_time_major(x_dma[0])
    else:
        x = x_ref[...]
    xn = _rms(x, g_ref[layer:layer + 1, :]).astype(_BF16)

    def proj(lo, hi):
        return _dot(xn, w_in_ref[:, lo:hi])

    u = proj(1536, 2048)
    pair = 2 if tb % 2 == 0 else 1
    steps = tb // pair
    srows = steps * rb
    if pair == 2:
        u4 = u.reshape(steps, 2, rb, SSM_WIDTH)
        u_parts = [u4[:, i].reshape(srows, SSM_WIDTH) for i in range(2)]
    else:
        u_parts = [u]
    ub_parts = [v.astype(_BF16) for v in u_parts]
    s_re[0:rb, :] = sre_out_ref[...]
    s_im[0:rb, :] = sim_out_ref[...]
    for c in range(SSM_CHUNKS):
        ch = slice(c * CHUNK_CH, (c + 1) * CHUNK_CH)
        if pair == 2:
            inc = _dot(jnp.concatenate([ub_parts[0][:, ch], ub_parts[1][:, ch]], axis=1), b2blk_ref[c])
        else:
            inc = _dot(ub_parts[0][:, ch], b2blk_ref[c, CHUNK_CH:2 * CHUNK_CH, :])
        s_re[rb:rb + srows, c * CHUNK_ST:(c + 1) * CHUNK_ST] = inc[:, 0:CHUNK_ST]
        s_im[rb:rb + srows, c * CHUNK_ST:(c + 1) * CHUNK_ST] = inc[:, CHUNK_ST:2 * CHUNK_ST]

    a_row = 2 * (pair - 1)
    for c in range(SSM_CHUNKS):
        cols = slice(c * CHUNK_ST, (c + 1) * CHUNK_ST)
        ar = jnp.broadcast_to(abar_ref[a_row:a_row + 1, cols], (rb, CHUNK_ST))
        ai = jnp.broadcast_to(abar_ref[a_row + 1:a_row + 2, cols], (rb, CHUNK_ST))

        def step(k, carry, cols=cols, ar=ar, ai=ai):
            sr, si = carry
            r0 = pl.multiple_of(rb + k * rb, rb)
            nsr = ar * sr - ai * si + s_re[pl.ds(r0, rb), cols]
            nsi = ar * si + ai * sr + s_im[pl.ds(r0, rb), cols]
            s_re[pl.ds(r0, rb), cols] = nsr
            s_im[pl.ds(r0, rb), cols] = nsi
            return nsr, nsi

        sr, si = lax.fori_loop(0, steps, step, (s_re[0:rb, cols], s_im[0:rb, cols]), unroll=True)
        sre_out_ref[:, cols] = sr
        sim_out_ref[:, cols] = si

    h = proj(0, 512)
    gate_c = proj(1024, 1536)
    if single_step:
        _conv_state_fetch(conv0_hbm, *conv_io, wait=True)
    vbuf[0:rb, :] = conv_carry[0]
    vbuf[rb:carry_rows, :] = conv_carry[1]
    vbuf[carry_rows:carry_rows + rows, :] = gate_c * h
    conv_y = (cb_ref[layer:layer + 1, :]
              + vbuf[0:rows, :] * cw_ref[0:1, :]
              + vbuf[rb:rb + rows, :] * cw_ref[1:2, :]
              + vbuf[2 * rb:2 * rb + rows, :] * cw_ref[2:3, :])
    conv_carry[0] = vbuf[rows:rows + rb, :]
    conv_carry[1] = vbuf[rows + rb:rows + carry_rows, :]
    if single_step:
        _conv_state_store(conv_carry, conv_out_hbm, layer, conv_sem, single_step, wait=False)
    gate_b = proj(512, 1024)
    z_a = _dot((gate_b * conv_y).astype(_BF16), w_out_a_ref[...])
    merged = _sigmoid(proj(2048, 3072)) * z_a

    y_parts = [[] for _ in range(pair)]
    for c in range(SSM_CHUNKS):
        ch = slice(c * CHUNK_CH, (c + 1) * CHUNK_CH)
        cols = slice(c * CHUNK_ST, (c + 1) * CHUNK_ST)
        if pair == 2:
            y2 = (_dot_nt(s_re[:, cols].astype(_BF16), c2_re_ref[c])
                  + _dot_nt(s_im[:, cols].astype(_BF16), c2_im_ref[c]))
            y_parts[0].append(y2[0:srows, CHUNK_CH:2 * CHUNK_CH] + _dot(ub_parts[0][:, ch], fblk_ref[c]))
            y_parts[1].append(y2[rb:rb + srows, 0:CHUNK_CH])
        else:
            y_parts[0].append(
                _dot_nt(s_re[rb:rb + srows, cols].astype(_BF16), c2_re_ref[c, 0:CHUNK_CH, :])
                + _dot_nt(s_im[rb:rb + srows, cols].astype(_BF16), c2_im_ref[c, 0:CHUNK_CH, :]))
    d_skip = dskip_ref[layer:layer + 1, :]
    ssm_parts = [jnp.concatenate(yp, axis=1) + d_skip * up for yp, up in zip(y_parts, u_parts)]
    if pair == 2:
        ssm_y = jnp.stack([v.reshape(steps, rb, SSM_WIDTH) for v in ssm_parts],
                          axis=1).reshape(rows, SSM_WIDTH)
    else:
        ssm_y = ssm_parts[0]
    sy = _gelu(ssm_y).astype(_BF16)
    z_b = _dot(sy, glu_a_ref[...]) * _sigmoid(_dot(sy, glu_b_ref[...]))
    merged = (merged + _sigmoid(proj(3072, 4096)) * z_b).astype(_BF16)
    out_ref[...] = x + _dot(merged, w_o_ref[...])
    if not single_step:
        _conv_state_store(conv_carry, conv_out_hbm, layer, conv_sem, single_step, wait=False)
    _conv_state_store(conv_carry, conv_out_hbm, layer, conv_sem, single_step, wait=True)


def _ffn_kernel(x_ref, p_ref, *refs, rb, tb, layer, zero_conv, single_step, final_norm, p_native,
                out_native):
    refs = list(refs)
    f0_hbm = None if zero_conv else refs.pop(0)
    if layer > 0:
        refs.pop(0)
    (g2_ref, w_up_ref, fcw_ref, fcb_ref, w_down_ref,
     g3_ref, w_pg_ref, w_ple_ref, gf_ref,
     out_ref, fconv_out_hbm,
     fbuf, fcarry, fsem, *dma) = refs
    rows = rb * tb
    carry_rows = (CONV_WIDTH - 1) * rb
    p_dma = [dma.pop(0), dma.pop(0)] if p_native else None
    out_dma = [dma.pop(0), dma.pop(0)] if out_native else None

    conv_io = (layer, fcarry, fsem, single_step)
    _conv_state_fetch(f0_hbm, *conv_io, wait=False)
    if not single_step:
        _conv_state_fetch(f0_hbm, *conv_io, wait=True)
    if p_native:
        _fetch_time_major(p_ref, (layer,), *p_dma, tb)

    x = x_ref[...]
    xn = _rms(x, g2_ref[layer:layer + 1, :]).astype(_BF16)
    acc = x
    if single_step:
        _conv_state_fetch(f0_hbm, *conv_io, wait=True)
    pending = None
    for lo, hi in FF_CHUNKS:
        wc = hi - lo
        up_a = _dot(xn, w_up_ref[:, lo:hi])
        up_b = _dot(xn, w_up_ref[:, D_FF + lo:D_FF + hi])
        if pending is not None:
            acc = acc + _dot(pending[0], w_down_ref[pending[1]:pending[2], :])
        fbuf[0:rb, 0:wc] = fcarry[0, :, lo:hi]
        fbuf[rb:carry_rows, 0:wc] = fcarry[1, :, lo:hi]
        fbuf[carry_rows:carry_rows + rows, 0:wc] = up_a
        conv_a = (fcb_ref[layer:layer + 1, lo:hi]
                  + fbuf[0:rows, 0:wc] * fcw_ref[0:1, lo:hi]
                  + fbuf[rb:rb + rows, 0:wc] * fcw_ref[1:2, lo:hi]
                  + fbuf[2 * rb:2 * rb + rows, 0:wc] * fcw_ref[2:3, lo:hi])
        fcarry[0, :, lo:hi] = fbuf[rows:rows + rb, 0:wc]
        fcarry[1, :, lo:hi] = fbuf[rows + rb:rows + carry_rows, 0:wc]
        pending = ((_gelu(conv_a) * up_b).astype(_BF16), lo, hi)
    acc = acc + _dot(pending[0], w_down_ref[pending[1]:pending[2], :])
    if single_step:
        _conv_state_store(fcarry, fconv_out_hbm, layer, fsem, single_step, wait=False)
    x2 = acc
    xn3 = _rms(x2, g3_ref[layer:layer + 1, :]).astype(_BF16)
    gate = _sigmoid(_dot(xn3, w_pg_ref[...]))
    p = _read_time_major(p_dma[0]) if p_native else p_ref[...]
    pe = _dot(p.astype(_BF16), w_ple_ref[...])
    x3 = x2 + gate * pe
    if final_norm:
        x3 = _rms(x3, gf_ref[...])
    if out_native:
        _store_batch_major(x3, out_ref, *out_dma, tb)
    else:
        out_ref[...] = x3
    if not single_step:
        _conv_state_store(fcarry, fconv_out_hbm, layer, fsem, single_step, wait=False)
    _conv_state_store(fcarry, fconv_out_hbm, layer, fsem, single_step, wait=True)


def _const_spec(shape, layer=None):
    if layer is None:
        return pl.BlockSpec(shape, lambda j: (0,) * len(shape), pipeline_mode=pl.Buffered(1))
    return pl.BlockSpec((None,) + shape, lambda j: (layer,) + (0,) * len(shape),
                        pipeline_mode=pl.Buffered(1))


def _layer_weight(wts, name, layer):
    w = wts[name]
    return w[layer] if isinstance(w, list) else (w, layer)


def _native_scratch(tb, rb, width):
    return [pltpu.VMEM((2, tb, rb, width), _F32), pltpu.SemaphoreType.DMA((2,))]


def _stack_spec(n_layers, rb, width):
    return pl.BlockSpec((n_layers, rb, width), lambda j: (0, 0, 0))


def _conv_state_scratch(rb, width):
    return [pltpu.VMEM((CONV_WIDTH - 1, rb, width), _F32), pltpu.SemaphoreType.DMA((DEPTH, CONV_WIDTH - 1))]


def _mixer_call(x, conv0, ssm0, prev, wts, layer, ssm_layer, rb, tb, x_native, cast=()):
    n = x.shape[0] * x.shape[1] if x_native else x.shape[0]
    rows = rb * tb
    carry_rows = (CONV_WIDTH - 1) * rb
    hbm_spec = pl.BlockSpec(memory_space=pl.ANY)
    row_spec = pl.BlockSpec((rows, D_MODEL), lambda j: (j, 0))
    operands = [x]
    in_specs = [hbm_spec if x_native else row_spec]
    if conv0 is not None:
        operands.append(conv0)
        in_specs.append(hbm_spec)
    operands += list(ssm0)
    in_specs += [_const_spec((rb, SSM_FLAT), ssm_layer)] * 2
    aliases = {}
    if prev:
        aliases[len(operands)] = 1
        operands += list(prev)
        in_specs += [hbm_spec] + [_const_spec((layer, rb, SSM_FLAT))] * 2
    steps = n // rows
    cast_out_specs, cast_shapes = [], []
    for w, first, count in cast:
        per = steps // count
        assert steps % count == 0 and w.shape[1] % (16 * per) == 0, (steps, count, w.shape)
        block = (None, w.shape[1] // per, w.shape[2])
        operands.append(w)
        in_specs.append(pl.BlockSpec(block, lambda j, per=per, first=first: (first + j // per, j % per, 0)))
        cast_out_specs.append(pl.BlockSpec(block, lambda j, per=per: (j // per, j % per, 0)))
        cast_shapes.append(jax.ShapeDtypeStruct((count,) + w.shape[1:], _BF16))
    names = ("w_in", "w_out_a", "w_glu_a", "w_glu_b", "w_o")
    (w_in, i_in), (w_out_a, i_out_a), (w_glu_a, i_glu_a), (w_glu_b, i_glu_b), (w_o, i_o) = [
        _layer_weight(wts, k, layer) for k in names]
    in_specs += [
        _const_spec((DEPTH, D_MODEL)),
        _const_spec((D_MODEL, IN_PROJ_WIDTH), i_in),
        _const_spec((CONV_WIDTH, CONV_A_WIDTH), layer),
        _const_spec((DEPTH, CONV_A_WIDTH)),
        _const_spec((CONV_A_WIDTH, D_MODEL), i_out_a),
        _const_spec((4, SSM_FLAT), layer),
        _const_spec((SSM_CHUNKS, 2 * CHUNK_CH, 2 * CHUNK_ST), layer),
        _const_spec((SSM_CHUNKS, 2 * CHUNK_CH, CHUNK_ST), layer),
        _const_spec((SSM_CHUNKS, 2 * CHUNK_CH, CHUNK_ST), layer),
        _const_spec((SSM_CHUNKS, CHUNK_CH, CHUNK_CH), layer),
        _const_spec((DEPTH, SSM_WIDTH)),
        _const_spec((SSM_WIDTH, D_MODEL), i_glu_a),
        _const_spec((SSM_WIDTH, D_MODEL), i_glu_b),
        _const_spec((D_MODEL, D_MODEL), i_o),
    ]
    out_specs = [row_spec, hbm_spec] + [_stack_spec(layer + 1, rb, SSM_FLAT)] * 2 + cast_out_specs
    out_shape = ([jax.ShapeDtypeStruct((n, D_MODEL), _F32),
                  jax.ShapeDtypeStruct((DEPTH, rb, CONV_WIDTH - 1, CONV_A_WIDTH), _F32)]
                 + [jax.ShapeDtypeStruct((layer + 1, rb, SSM_FLAT), _F32)] * 2 + cast_shapes)
    s_rows = rows // 2 if tb % 2 == 0 else rows
    scratch = [
        pltpu.VMEM((carry_rows + rows, CONV_A_WIDTH), _F32),
        pltpu.VMEM((rb + s_rows, SSM_FLAT), _F32),
        pltpu.VMEM((rb + s_rows, SSM_FLAT), _F32),
    ]
    scratch += _conv_state_scratch(rb, CONV_A_WIDTH)
    if x_native:
        scratch += _native_scratch(tb, rb, D_MODEL)
    return pl.pallas_call(
        functools.partial(_mixer_kernel, rb=rb, tb=tb, layer=layer, zero_conv=conv0 is None,
                          single_step=n == rows, n_cast=len(cast)),
        grid=(n // rows,),
        in_specs=in_specs,
        out_specs=out_specs,
        out_shape=out_shape,
        scratch_shapes=scratch,
        input_output_aliases=aliases,
        compiler_params=pltpu.CompilerParams(
            dimension_semantics=("arbitrary",), vmem_limit_bytes=VMEM_LIMIT_BYTES),
        name=f"mixer_l{layer}_rb{rb}",
    )(*operands,
      wts["norm_mix"], w_in, wts["conv_a_w"], wts["conv_a_b"], w_out_a,
      wts["abar"], wts["b2blk"], wts["c2_re"], wts["c2_im"], wts["fblk"], wts["d_skip"],
      w_glu_a, w_glu_b, w_o)


def _ffn_call(x, p, f0, prev, wts, layer, rb, tb, final_norm, p_native, out_native):
    n = x.shape[0]
    rows = rb * tb
    carry_rows = (CONV_WIDTH - 1) * rb
    hbm_spec = pl.BlockSpec(memory_space=pl.ANY)
    row_spec = pl.BlockSpec((rows, D_MODEL), lambda j: (j, 0))
    operands = [x, p]
    in_specs = [
        row_spec,
        hbm_spec if p_native else pl.BlockSpec((None, rows, PLE_DIM), lambda j: (layer, j, 0)),
    ]
    if f0 is not None:
        operands.append(f0)
        in_specs.append(hbm_spec)
    aliases = {}
    if prev is not None:
        aliases[len(operands)] = 1
        operands.append(prev)
        in_specs.append(hbm_spec)
    in_specs += [
        _const_spec((DEPTH, D_MODEL)),
        _const_spec((D_MODEL, 2 * D_FF), layer),
        _const_spec((CONV_WIDTH, D_FF), layer),
        _const_spec((DEPTH, D_FF)),
        _const_spec((D_FF, D_MODEL), layer),
        _const_spec((DEPTH, D_MODEL)),
        _const_spec((D_MODEL, D_MODEL), layer),
        _const_spec((PLE_DIM, D_MODEL), layer),
        _const_spec((1, D_MODEL)),
    ]
    out_specs = [hbm_spec if out_native else row_spec, hbm_spec]
    out_shape = [jax.ShapeDtypeStruct((rb, n // rb, D_MODEL) if out_native else (n, D_MODEL), _F32),
                 jax.ShapeDtypeStruct((DEPTH, rb, CONV_WIDTH - 1, D_FF), _F32)]
    scratch = [pltpu.VMEM((carry_rows + rows, FF_CHUNK_MAX), _F32)]
    scratch += _conv_state_scratch(rb, D_FF)
    if p_native:
        scratch += _native_scratch(tb, rb, PLE_DIM)
    if out_native:
        scratch += _native_scratch(tb, rb, D_MODEL)
    return pl.pallas_call(
        functools.partial(_ffn_kernel, rb=rb, tb=tb, layer=layer, zero_conv=f0 is None,
                          single_step=n == rows,
                          final_norm=final_norm, p_native=p_native, out_native=out_native),
        grid=(n // rows,),
        in_specs=in_specs,
        out_specs=out_specs,
        out_shape=out_shape,
        scratch_shapes=scratch,
        input_output_aliases=aliases,
        compiler_params=pltpu.CompilerParams(
            dimension_semantics=("arbitrary",), vmem_limit_bytes=VMEM_LIMIT_BYTES),
        name=f"ffn_l{layer}_rb{rb}",
    )(*operands,
      wts["norm_ffn"], wts["w_up"], wts["ffn_conv_w"], wts["ffn_conv_b"], wts["w_down"],
      wts["norm_ple"], wts["w_ple_gate"], wts["w_ple"], wts["norm_final"])


def _run_trunk(x, p, conv0, sre0, sim0, f0, wts, rb, tb_mixer, tb_ffn, native, cast_weights=False):
    mixer_states, ffn_state = [], None
    for layer in range(DEPTH):
        last = layer == DEPTH - 1
        ssm_layer = 0 if conv0 is None else layer
        cast = []
        if cast_weights and layer == 0:
            cast = ([(wts[k], 0, DEPTH) for k in FFN_MATMUL_WEIGHTS]
                    + [(wts[k][1], 1, DEPTH - 1) for k in MIXER_MATMUL_WEIGHTS])
        x, *mixer_states = _mixer_call(x, conv0, (sre0, sim0), mixer_states, wts, layer, ssm_layer,
                                       rb, tb_mixer[layer], x_native=native and layer == 0, cast=cast)
        if cast:
            rounded = mixer_states[3:]
            mixer_states = mixer_states[:3]
            wts.update(zip(FFN_MATMUL_WEIGHTS, rounded))
            for k, w in zip(MIXER_MATMUL_WEIGHTS, rounded[len(FFN_MATMUL_WEIGHTS):]):
                wts[k] = [wts[k][0]] + [(w, i) for i in range(DEPTH - 1)]
        x, ffn_state = _ffn_call(x, p, f0, ffn_state, wts, layer, rb, tb_ffn,
                                 final_norm=last, p_native=native, out_native=native and last)
    return (x, *mixer_states, ffn_state)


def kernel(x_prompt, x_sample, p_prompt, p_sample, state_conv_a, state_ssm_re, state_ssm_im, state_ffn_conv, norm_mix, w_in, conv_a_w, conv_a_b, w_out_a, log_dt, lam_re, lam_im, b_re, b_im, c_re, c_im, d_skip, w_glu_a, w_glu_b, w_o, norm_ffn, w_up, ffn_conv_w, ffn_conv_b, w_down, norm_ple, w_ple_gate, w_ple, norm_final):
    abar, b2blk, c2_re, c2_im, fblk = _ssm_prep(log_dt, lam_re, lam_im, b_re, b_im, c_re, c_im)
    mixer_f32 = (w_in, w_out_a, w_glu_a, w_glu_b, w_o)
    w_in_0, w_out_a_0, w_glu_a_0, w_glu_b_0, w_o_0 = _round_layer0(mixer_f32)
    wts = {
        "norm_mix": norm_mix,
        "w_in": [(w_in_0, 0), w_in],
        "conv_a_w": conv_a_w,
        "conv_a_b": conv_a_b,
        "w_out_a": [(w_out_a_0, 0), w_out_a],
        "abar": abar, "b2blk": b2blk, "c2_re": c2_re, "c2_im": c2_im, "fblk": fblk,
        "d_skip": d_skip,
        "w_glu_a": [(w_glu_a_0, 0), w_glu_a],
        "w_glu_b": [(w_glu_b_0, 0), w_glu_b],
        "w_o": [(w_o_0, 0), w_o],
        "norm_ffn": norm_ffn,
        "w_up": w_up,
        "ffn_conv_w": ffn_conv_w,
        "ffn_conv_b": ffn_conv_b,
        "w_down": w_down,
        "norm_ple": norm_ple,
        "w_ple_gate": w_ple_gate,
        "w_ple": w_ple,
        "norm_final": norm_final.reshape(1, D_MODEL),
    }

    bp = x_prompt.shape[0]
    zs = jnp.zeros((1, bp, SSM_FLAT), _F32)
    y_prompt, conv_p, sre_p, sim_p, ffn_p = _run_trunk(
        x_prompt, p_prompt, None, zs, zs, None, wts, bp, MIXER_TIME_BLOCKS, FFN_TIME_BLOCK,
        native=True, cast_weights=True)

    bs, sseq, _ = x_sample.shape
    xs = x_sample.transpose(1, 0, 2).reshape(sseq * bs, D_MODEL)
    ps = p_sample.transpose(0, 2, 1, 3).reshape(DEPTH, sseq * bs, PLE_DIM)
    ys, conv_s, sre_s, sim_s, ffn_s = _run_trunk(
        xs, ps, state_conv_a,
        state_ssm_re.reshape(DEPTH, bs, SSM_FLAT), state_ssm_im.reshape(DEPTH, bs, SSM_FLAT),
        state_ffn_conv, wts, bs, (sseq,) * DEPTH, sseq, native=False)
    y_sample = ys.reshape(sseq, bs, D_MODEL).transpose(1, 0, 2)

    def ssm_state(s, b):
        return s.reshape(DEPTH, b, SSM_GROUPS, SSM_STATE)

    return (y_prompt, y_sample,
            conv_p, ssm_state(sre_p, bp), ssm_state(sim_p, bp), ffn_p,
            conv_s, ssm_state(sre_s, bs), ssm_state(sim_s, bs), ffn_s)
```

```python
import functools
import math

import jax
import jax.numpy as jnp
from jax import lax
from jax.experimental import pallas as pl
from jax.experimental.pallas import tpu as pltpu

D_MODEL = 1024
DEPTH = 2
CONV_WIDTH = 3
CONV_A_WIDTH = 512
SSM_WIDTH = 512
SSM_GROUP = 16
SSM_GROUPS = 32
SSM_STATE = 64
SSM_FLAT = SSM_GROUPS * SSM_STATE
D_FF = 2816
PLE_DIM = 256
NORM_EPS = 1e-6
IN_PROJ_WIDTH = 4096

SSM_CHUNKS = 4
S5_FOLD = 4
CHUNK_CH = SSM_WIDTH // SSM_CHUNKS
CHUNK_ST = SSM_FLAT // SSM_CHUNKS

FF_CHUNKS = ((0, 1024), (1024, 2048), (2048, 2816))
FF_CHUNK_MAX = 1024

MIXER_TIME_BLOCKS = (64, 128)
FFN_TIME_BLOCK = 128
MIXER_MATMUL_WEIGHTS = ("w_in", "w_out_a", "w_glu_a", "w_glu_b", "w_o")
FFN_MATMUL_WEIGHTS = ("w_up", "w_down", "w_ple_gate", "w_ple")
ROUND_CHUNKS = 8
VMEM_LIMIT_BYTES = 56 * 1024 * 1024

_BF16 = jnp.bfloat16
_F32 = jnp.float32


def _dot(a, b):
    return jnp.dot(a, b, preferred_element_type=_F32)


def _dot_nt(a, b):
    return lax.dot_general(a, b, (((1,), (1,)), ((), ())), preferred_element_type=_F32)


def _dot_nt_split(a, b):
    a_hi = a.astype(_BF16)
    b_hi = b.astype(_BF16)
    a_lo = (a - a_hi.astype(_F32)).astype(_BF16)
    b_lo = (b - b_hi.astype(_F32)).astype(_BF16)
    return _dot_nt(a_hi, b_hi) + _dot_nt(a_hi, b_lo) + _dot_nt(a_lo, b_hi)


def _rms(x, g):
    ms = jnp.mean(x * x, axis=-1, keepdims=True)
    return x * lax.rsqrt(ms + NORM_EPS) * g


def _gelu(x):
    c = math.sqrt(2.0 / math.pi)
    t = jnp.tanh(x * (c + (0.044715 * c) * (x * x)))
    return x * (0.5 + 0.5 * t)


def _sigmoid(x):
    return 0.5 * jnp.tanh(0.5 * x) + 0.5


def _seq_copies(hbm_ref, lead, buf, sem, step, slot, tb, to_hbm):
    copies = []
    for b in range(buf.shape[2]):
        hbm_view = hbm_ref.at[lead + (b, pl.ds(step * tb, tb))]
        vmem_view = buf.at[slot, :, b]
        src, dst = (vmem_view, hbm_view) if to_hbm else (hbm_view, vmem_view)
        copies.append(pltpu.make_async_copy(src, dst, sem.at[slot]))
    return copies


def _fetch_time_major(hbm_ref, lead, buf, sem, tb):
    j = pl.program_id(0)
    slot = j % 2

    @pl.when(j == 0)
    def _():
        for c in _seq_copies(hbm_ref, lead, buf, sem, 0, 0, tb, False):
            c.start()

    @pl.when(j + 1 < pl.num_programs(0))
    def _():
        for c in _seq_copies(hbm_ref, lead, buf, sem, j + 1, 1 - slot, tb, False):
            c.start()

    for c in _seq_copies(hbm_ref, lead, buf, sem, j, slot, tb, False):
        c.wait()


def _read_time_major(buf):
    tb, nb, width = buf.shape[1:]
    return buf[pl.program_id(0) % 2].reshape(tb * nb, width)


def _store_batch_major(val, hbm_ref, buf, sem, tb):
    j = pl.program_id(0)
    slot = j % 2
    buf[slot] = val.reshape(tb, buf.shape[2], buf.shape[3])
    for c in _seq_copies(hbm_ref, (), buf, sem, j, slot, tb, True):
        c.start()

    @pl.when(j >= 1)
    def _():
        for c in _seq_copies(hbm_ref, (), buf, sem, j - 1, 1 - slot, tb, True):
            c.wait()

    @pl.when(j == pl.num_programs(0) - 1)
    def _():
        for c in _seq_copies(hbm_ref, (), buf, sem, j, slot, tb, True):
            c.wait()


def _conv_state_copies(hbm_ref, slot, carry, sem, to_hbm):
    copies = []
    for k in range(CONV_WIDTH - 1):
        hbm_view = hbm_ref.at[slot, :, k, :]
        src, dst = (carry.at[k], hbm_view) if to_hbm else (hbm_view, carry.at[k])
        copies.append(pltpu.make_async_copy(src, dst, sem.at[slot, k]))
    return copies


def _at_step(first, single_step, body):
    if single_step:
        body()
    else:
        step = 0 if first else pl.num_programs(0) - 1
        pl.when(pl.program_id(0) == step)(body)


def _conv_state_fetch(state_hbm, layer, carry, sem, single_step, wait):
    def body():
        if state_hbm is None:
            if not wait:
                carry[...] = jnp.zeros(carry.shape, carry.dtype)
        else:
            for c in _conv_state_copies(state_hbm, layer, carry, sem, False):
                c.wait() if wait else c.start()

    _at_step(True, single_step, body)


def _conv_state_store(carry, out_hbm, layer, sem, single_step, wait):
    def body():
        for slot in range(layer, out_hbm.shape[0]):
            for c in _conv_state_copies(out_hbm, slot, carry, sem, True):
                c.wait() if wait else c.start()

    _at_step(False, single_step, body)


def _ssm_prep_kernel(logdt_ref, lre_ref, lim_ref, bre_ref, bim_ref, cre_ref, cim_ref,
                     abar_ref, bblk_ref, c_re_ref, c_im_ref, fblk_ref):
    fold = S5_FOLD
    dt = jnp.exp(logdt_ref[...])
    lre = lre_ref[...]
    lim = lim_ref[...]
    mag = jnp.exp(lre * dt)
    ang = lim * dt
    ar = mag * jnp.cos(ang)
    ai = mag * jnp.sin(ang)
    a_pow = [(jnp.ones_like(ar), jnp.zeros_like(ai))]
    for _ in range(fold):
        pr, pi = a_pow[-1]
        a_pow.append((pr * ar - pi * ai, pr * ai + pi * ar))
    abar_ref[0:1, :] = ar
    abar_ref[1:2, :] = ai
    abar_ref[2:3, :] = a_pow[fold][0]
    abar_ref[3:4, :] = a_pow[fold][1]
    den = lre * lre + lim * lim
    nr = ar - 1.0
    coef_re = (nr * lre + ai * lim) / den
    coef_im = (ai * lre - nr * lim) / den
    bre = bre_ref[...]
    bim = bim_ref[...]
    bbar_re = coef_re * bre - coef_im * bim
    bbar_im = coef_re * bim + coef_im * bre
    row = lax.broadcasted_iota(jnp.int32, (CHUNK_CH, CHUNK_ST), 0)
    lane = lax.broadcasted_iota(jnp.int32, (CHUNK_CH, CHUNK_ST), 1)
    same_group = (row // SSM_GROUP) == (lane // SSM_STATE)
    reps = CHUNK_CH // SSM_GROUP
    for c in range(SSM_CHUNKS):
        cols = slice(c * CHUNK_ST, (c + 1) * CHUNK_ST)
        rows = slice(c * CHUNK_CH, (c + 1) * CHUNK_CH)

        def b_block(v, cols=cols):
            return jnp.where(same_group, jnp.concatenate([v[:, cols]] * reps, axis=0), 0.0)

        amb = [(b_block(pr * bbar_re - pi * bbar_im), b_block(pr * bbar_im + pi * bbar_re))
               for pr, pi in a_pow[:fold]]
        for i in range(fold):
            blk = slice(i * CHUNK_CH, (i + 1) * CHUNK_CH)
            bblk_ref[c, blk, 0:CHUNK_ST] = amb[fold - 1 - i][0].astype(_BF16)
            bblk_ref[c, blk, CHUNK_ST:2 * CHUNK_ST] = amb[fold - 1 - i][1].astype(_BF16)
        c_re = jnp.where(same_group, cre_ref[rows, :], 0.0)
        c_im = jnp.where(same_group, cim_ref[rows, :], 0.0)
        for m in range(fold):
            blk = slice(m * CHUNK_CH, (m + 1) * CHUNK_CH)
            pr, pi = a_pow[m][0][:, cols], a_pow[m][1][:, cols]
            c_re_ref[c, blk, :] = (c_re * pr - c_im * pi).astype(_BF16)
            c_im_ref[c, blk, :] = (-(c_re * pi + c_im * pr)).astype(_BF16)
        feeds = [(_dot_nt_split(amb[m][0], c_re) - _dot_nt_split(amb[m][1], c_im)).astype(_BF16)
                 for m in range(fold - 1)]
        zero = jnp.zeros((CHUNK_CH, CHUNK_CH), _BF16)
        for i in range(fold - 1):
            for j in range(fold - 1):
                fblk_ref[c, i * CHUNK_CH:(i + 1) * CHUNK_CH, j * CHUNK_CH:(j + 1) * CHUNK_CH] = (
                    feeds[j - i] if j >= i else zero)


def _ssm_prep(log_dt, lam_re, lam_im, b_re, b_im, c_re, c_im):
    logdt = jnp.repeat(log_dt, SSM_STATE, axis=1).reshape(DEPTH, 1, SSM_FLAT)
    lre = lam_re.reshape(DEPTH, 1, SSM_FLAT)
    lim = lam_im.reshape(DEPTH, 1, SSM_FLAT)
    bre = b_re.transpose(0, 3, 1, 2).reshape(DEPTH, SSM_GROUP, SSM_FLAT)
    bim = b_im.transpose(0, 3, 1, 2).reshape(DEPTH, SSM_GROUP, SSM_FLAT)
    reps = CHUNK_ST // SSM_STATE
    cre = jnp.tile(c_re.reshape(DEPTH, SSM_WIDTH, SSM_STATE), (1, 1, reps))
    cim = jnp.tile(c_im.reshape(DEPTH, SSM_WIDTH, SSM_STATE), (1, 1, reps))

    def spec(*shape):
        return pl.BlockSpec((None,) + shape, lambda i: (i,) + (0,) * len(shape))

    return pl.pallas_call(
        _ssm_prep_kernel,
        grid=(DEPTH,),
        in_specs=[spec(1, SSM_FLAT), spec(1, SSM_FLAT), spec(1, SSM_FLAT),
                  spec(SSM_GROUP, SSM_FLAT), spec(SSM_GROUP, SSM_FLAT),
                  spec(SSM_WIDTH, CHUNK_ST), spec(SSM_WIDTH, CHUNK_ST)],
        out_specs=[spec(4, SSM_FLAT),
                   spec(SSM_CHUNKS, S5_FOLD * CHUNK_CH, 2 * CHUNK_ST),
                   spec(SSM_CHUNKS, S5_FOLD * CHUNK_CH, CHUNK_ST),
                   spec(SSM_CHUNKS, S5_FOLD * CHUNK_CH, CHUNK_ST),
                   spec(SSM_CHUNKS, (S5_FOLD - 1) * CHUNK_CH, (S5_FOLD - 1) * CHUNK_CH)],
        out_shape=[jax.ShapeDtypeStruct((DEPTH, 4, SSM_FLAT), _F32),
                   jax.ShapeDtypeStruct((DEPTH, SSM_CHUNKS, S5_FOLD * CHUNK_CH, 2 * CHUNK_ST), _BF16),
                   jax.ShapeDtypeStruct((DEPTH, SSM_CHUNKS, S5_FOLD * CHUNK_CH, CHUNK_ST), _BF16),
                   jax.ShapeDtypeStruct((DEPTH, SSM_CHUNKS, S5_FOLD * CHUNK_CH, CHUNK_ST), _BF16),
                   jax.ShapeDtypeStruct((DEPTH, SSM_CHUNKS, (S5_FOLD - 1) * CHUNK_CH,
                                         (S5_FOLD - 1) * CHUNK_CH), _BF16)],
        name="ssm_prep",
    )(logdt, lre, lim, bre, bim, cre, cim)


def _round_kernel(*refs):
    half = len(refs) // 2
    for src_ref, dst_ref in zip(refs[:half], refs[half:]):
        dst_ref[...] = src_ref[...].astype(_BF16)


def _round_layer0(ws):
    specs = [pl.BlockSpec((1, w.shape[1] // ROUND_CHUNKS, w.shape[2]), lambda j: (0, j, 0)) for w in ws]
    return pl.pallas_call(
        _round_kernel,
        grid=(ROUND_CHUNKS,),
        in_specs=specs,
        out_specs=specs,
        out_shape=[jax.ShapeDtypeStruct((1,) + w.shape[1:], _BF16) for w in ws],
        name="round_mixer_l0",
    )(*ws)


def _mixer_kernel(x_ref, *refs, rb, tb, layer, zero_conv, single_step, n_cast):
    refs = list(refs)
    conv0_hbm = None if zero_conv else refs.pop(0)
    sre0_ref, sim0_ref = refs.pop(0), refs.pop(0)
    prev = [refs.pop(0) for _ in range(3)] if layer > 0 else []
    cast_in = [refs.pop(0) for _ in range(n_cast)]
    (g_ref, w_in_ref, cw_ref, cb_ref, w_out_a_ref,
     abar_ref, b2blk_ref, c2_re_ref, c2_im_ref, fblk_ref, dskip_ref,
     glu_a_ref, glu_b_ref, w_o_ref,
     out_ref, conv_out_hbm, sre_stack_ref, sim_stack_ref, *refs) = refs
    cast_out = [refs.pop(0) for _ in range(n_cast)]
    vbuf, s_re, s_im, conv_carry, conv_sem, *x_dma = refs
    for src_ref, dst_ref in zip(cast_in, cast_out):
        dst_ref[...] = src_ref[...].astype(_BF16)
    rows = rb * tb
    carry_rows = (CONV_WIDTH - 1) * rb
    sre_out_ref = sre_stack_ref.at[layer]
    sim_out_ref = sim_stack_ref.at[layer]

    @pl.when(pl.program_id(0) == 0)
    def _():
        for stack_ref, prev_ref in zip((sre_stack_ref, sim_stack_ref), prev[1:]):
            stack_ref[0:layer] = prev_ref[...]
        sre_out_ref[...] = sre0_ref[...]
        sim_out_ref[...] = sim0_ref[...]

    conv_io = (layer, conv_carry, conv_sem, single_step)
    _conv_state_fetch(conv0_hbm, *conv_io, wait=False)
    if not single_step:
        _conv_state_fetch(conv0_hbm, *conv_io, wait=True)
    if x_dma:
        _fetch_time_major(x_ref, (), *x_dma, tb)
        x = _read_time_major(x_dma[0])
    else:
        x = x_ref[...]
    xn = _rms(x, g_ref[layer:layer + 1, :]).astype(_BF16)

    def proj(lo, hi):
        return _dot(xn, w_in_ref[:, lo:hi])

    u = proj(1536, 2048)
    fold = S5_FOLD if tb % S5_FOLD == 0 else 1
    steps = tb // fold
    srows = steps * rb
    if fold > 1:
        u4 = u.reshape(steps, fold, rb, SSM_WIDTH)
        u_parts = [u4[:, i].reshape(srows, SSM_WIDTH) for i in range(fold)]
    else:
        u_parts = [u]
    ub_parts = [v.astype(_BF16) for v in u_parts]
    s_re[0:rb, :] = sre_out_ref[...]
    s_im[0:rb, :] = sim_out_ref[...]
    last_blk = slice((S5_FOLD - 1) * CHUNK_CH, S5_FOLD * CHUNK_CH)
    for c in range(SSM_CHUNKS):
        ch = slice(c * CHUNK_CH, (c + 1) * CHUNK_CH)
        if fold > 1:
            inc = _dot(jnp.concatenate([ub[:, ch] for ub in ub_parts], axis=1), b2blk_ref[c])
        else:
            inc = _dot(ub_parts[0][:, ch], b2blk_ref[c, last_blk, :])
        s_re[rb:rb + srows, c * CHUNK_ST:(c + 1) * CHUNK_ST] = inc[:, 0:CHUNK_ST]
        s_im[rb:rb + srows, c * CHUNK_ST:(c + 1) * CHUNK_ST] = inc[:, CHUNK_ST:2 * CHUNK_ST]

    a_row = 2 if fold > 1 else 0
    for c in range(SSM_CHUNKS):
        cols = slice(c * CHUNK_ST, (c + 1) * CHUNK_ST)
        ar = jnp.broadcast_to(abar_ref[a_row:a_row + 1, cols], (rb, CHUNK_ST))
        ai = jnp.broadcast_to(abar_ref[a_row + 1:a_row + 2, cols], (rb, CHUNK_ST))

        def step(k, carry, cols=cols, ar=ar, ai=ai):
            sr, si = carry
            r0 = pl.multiple_of(rb + k * rb, rb)
            nsr = ar * sr - ai * si + s_re[pl.ds(r0, rb), cols]
            nsi = ar * si + ai * sr + s_im[pl.ds(r0, rb), cols]
            s_re[pl.ds(r0, rb), cols] = nsr
            s_im[pl.ds(r0, rb), cols] = nsi
            return nsr, nsi

        sr, si = lax.fori_loop(0, steps, step, (s_re[0:rb, cols], s_im[0:rb, cols]), unroll=True)
        sre_out_ref[:, cols] = sr
        sim_out_ref[:, cols] = si

    h = proj(0, 512)
    gate_c = proj(1024, 1536)
    if single_step:
        _conv_state_fetch(conv0_hbm, *conv_io, wait=True)
    vbuf[0:rb, :] = conv_carry[0]
    vbuf[rb:carry_rows, :] = conv_carry[1]
    vbuf[carry_rows:carry_rows + rows, :] = gate_c * h
    conv_y = (cb_ref[layer:layer + 1, :]
              + vbuf[0:rows, :] * cw_ref[0:1, :]
              + vbuf[rb:rb + rows, :] * cw_ref[1:2, :]
              + vbuf[2 * rb:2 * rb + rows, :] * cw_ref[2:3, :])
    conv_carry[0] = vbuf[rows:rows + rb, :]
    conv_carry[1] = vbuf[rows + rb:rows + carry_rows, :]
    if single_step:
        _conv_state_store(conv_carry, conv_out_hbm, layer, conv_sem, single_step, wait=False)
    gate_b = proj(512, 1024)
    z_a = _dot((gate_b * conv_y).astype(_BF16), w_out_a_ref[...])
    merged = _sigmoid(proj(2048, 3072)) * z_a

    y_parts = [[] for _ in range(fold)]
    for c in range(SSM_CHUNKS):
        ch = slice(c * CHUNK_CH, (c + 1) * CHUNK_CH)
        cols = slice(c * CHUNK_ST, (c + 1) * CHUNK_ST)
        if fold > 1:
            y2 = (_dot_nt(s_re[:, cols].astype(_BF16), c2_re_ref[c])
                  + _dot_nt(s_im[:, cols].astype(_BF16), c2_im_ref[c]))
            feed = _dot(jnp.concatenate([ub[:, ch] for ub in ub_parts[:fold - 1]], axis=1), fblk_ref[c])
            for j in range(fold - 1):
                y_parts[j].append(y2[0:srows, (j + 1) * CHUNK_CH:(j + 2) * CHUNK_CH]
                                  + feed[:, j * CHUNK_CH:(j + 1) * CHUNK_CH])
            y_parts[fold - 1].append(y2[rb:rb + srows, 0:CHUNK_CH])
        else:
            y_parts[0].append(
                _dot_nt(s_re[rb:rb + srows, cols].astype(_BF16), c2_re_ref[c, 0:CHUNK_CH, :])
                + _dot_nt(s_im[rb:rb + srows, cols].astype(_BF16), c2_im_ref[c, 0:CHUNK_CH, :]))
    d_skip = dskip_ref[layer:layer + 1, :]
    ssm_parts = [jnp.concatenate(yp, axis=1) + d_skip * up for yp, up in zip(y_parts, u_parts)]
    if fold > 1:
        ssm_y = jnp.stack([v.reshape(steps, rb, SSM_WIDTH) for v in ssm_parts],
                          axis=1).reshape(rows, SSM_WIDTH)
    else:
        ssm_y = ssm_parts[0]
    sy = _gelu(ssm_y).astype(_BF16)
    z_b = _dot(sy, glu_a_ref[...]) * _sigmoid(_dot(sy, glu_b_ref[...]))
    merged = (merged + _sigmoid(proj(3072, 4096)) * z_b).astype(_BF16)
    out_ref[...] = x + _dot(merged, w_o_ref[...])
    if not single_step:
        _conv_state_store(conv_carry, conv_out_hbm, layer, conv_sem, single_step, wait=False)
    _conv_state_store(conv_carry, conv_out_hbm, layer, conv_sem, single_step, wait=True)


def _ffn_kernel(x_ref, p_ref, *refs, rb, tb, layer, zero_conv, single_step, final_norm, p_native,
                out_native):
    refs = list(refs)
    f0_hbm = None if zero_conv else refs.pop(0)
    if layer > 0:
        refs.pop(0)
    (g2_ref, w_up_ref, fcw_ref, fcb_ref, w_down_ref,
     g3_ref, w_pg_ref, w_ple_ref, gf_ref,
     out_ref, fconv_out_hbm,
     fbuf, fcarry, fsem, *dma) = refs
    rows = rb * tb
    carry_rows = (CONV_WIDTH - 1) * rb
    p_dma = [dma.pop(0), dma.pop(0)] if p_native else None
    out_dma = [dma.pop(0), dma.pop(0)] if out_native else None

    conv_io = (layer, fcarry, fsem, single_step)
    _conv_state_fetch(f0_hbm, *conv_io, wait=False)
    if not single_step:
        _conv_state_fetch(f0_hbm, *conv_io, wait=True)
    if p_native:
        _fetch_time_major(p_ref, (layer,), *p_dma, tb)

    x = x_ref[...]
    xn = _rms(x, g2_ref[layer:layer + 1, :]).astype(_BF16)
    acc = x
    if single_step:
        _conv_state_fetch(f0_hbm, *conv_io, wait=True)
    pending = None
    for lo, hi in FF_CHUNKS:
        wc = hi - lo
        up_a = _dot(xn, w_up_ref[:, lo:hi])
        up_b = _dot(xn, w_up_ref[:, D_FF + lo:D_FF + hi])
        if pending is not None:
            acc = acc + _dot(pending[0], w_down_ref[pending[1]:pending[2], :])
        fbuf[0:rb, 0:wc] = fcarry[0, :, lo:hi]
        fbuf[rb:carry_rows, 0:wc] = fcarry[1, :, lo:hi]
        fbuf[carry_rows:carry_rows + rows, 0:wc] = up_a
        conv_a = (fcb_ref[layer:layer + 1, lo:hi]
                  + fbuf[0:rows, 0:wc] * fcw_ref[0:1, lo:hi]
                  + fbuf[rb:rb + rows, 0:wc] * fcw_ref[1:2, lo:hi]
                  + fbuf[2 * rb:2 * rb + rows, 0:wc] * fcw_ref[2:3, lo:hi])
        fcarry[0, :, lo:hi] = fbuf[rows:rows + rb, 0:wc]
        fcarry[1, :, lo:hi] = fbuf[rows + rb:rows + carry_rows, 0:wc]
        pending = ((_gelu(conv_a) * up_b).astype(_BF16), lo, hi)
    acc = acc + _dot(pending[0], w_down_ref[pending[1]:pending[2], :])
    if single_step:
        _conv_state_store(fcarry, fconv_out_hbm, layer, fsem, single_step, wait=False)
    x2 = acc
    xn3 = _rms(x2, g3_ref[layer:layer + 1, :]).astype(_BF16)
    gate = _sigmoid(_dot(xn3, w_pg_ref[...]))
    p = _read_time_major(p_dma[0]) if p_native else p_ref[...]
    pe = _dot(p.astype(_BF16), w_ple_ref[...])
    x3 = x2 + gate * pe
    if final_norm:
        x3 = _rms(x3, gf_ref[...])
    if out_native:
        _store_batch_major(x3, out_ref, *out_dma, tb)
    else:
        out_ref[...] = x3
    if not single_step:
        _conv_state_store(fcarry, fconv_out_hbm, layer, fsem, single_step, wait=False)
    _conv_state_store(fcarry, fconv_out_hbm, layer, fsem, single_step, wait=True)


def _const_spec(shape, layer=None):
    if layer is None:
        return pl.BlockSpec(shape, lambda j: (0,) * len(shape), pipeline_mode=pl.Buffered(1))
    return pl.BlockSpec((None,) + shape, lambda j: (layer,) + (0,) * len(shape),
                        pipeline_mode=pl.Buffered(1))


def _layer_weight(wts, name, layer):
    w = wts[name]
    return w[layer] if isinstance(w, list) else (w, layer)


def _native_scratch(tb, rb, width):
    return [pltpu.VMEM((2, tb, rb, width), _F32), pltpu.SemaphoreType.DMA((2,))]


def _stack_spec(n_layers, rb, width):
    return pl.BlockSpec((n_layers, rb, width), lambda j: (0, 0, 0))


def _conv_state_scratch(rb, width):
    return [pltpu.VMEM((CONV_WIDTH - 1, rb, width), _F32), pltpu.SemaphoreType.DMA((DEPTH, CONV_WIDTH - 1))]


def _mixer_call(x, conv0, ssm0, prev, wts, layer, ssm_layer, rb, tb, x_native, cast=()):
    n = x.shape[0] * x.shape[1] if x_native else x.shape[0]
    rows = rb * tb
    carry_rows = (CONV_WIDTH - 1) * rb
    hbm_spec = pl.BlockSpec(memory_space=pl.ANY)
    row_spec = pl.BlockSpec((rows, D_MODEL), lambda j: (j, 0))
    operands = [x]
    in_specs = [hbm_spec if x_native else row_spec]
    if conv0 is not None:
        operands.append(conv0)
        in_specs.append(hbm_spec)
    operands += list(ssm0)
    in_specs += [_const_spec((rb, SSM_FLAT), ssm_layer)] * 2
    aliases = {}
    if prev:
        aliases[len(operands)] = 1
        operands += list(prev)
        in_specs += [hbm_spec] + [_const_spec((layer, rb, SSM_FLAT))] * 2
    steps = n // rows
    cast_out_specs, cast_shapes = [], []
    for w, first, count in cast:
        per = steps // count
        assert steps % count == 0 and w.shape[1] % (16 * per) == 0, (steps, count, w.shape)
        block = (None, w.shape[1] // per, w.shape[2])
        operands.append(w)
        in_specs.append(pl.BlockSpec(block, lambda j, per=per, first=first: (first + j // per, j % per, 0)))
        cast_out_specs.append(pl.BlockSpec(block, lambda j, per=per: (j // per, j % per, 0)))
        cast_shapes.append(jax.ShapeDtypeStruct((count,) + w.shape[1:], _BF16))
    names = ("w_in", "w_out_a", "w_glu_a", "w_glu_b", "w_o")
    (w_in, i_in), (w_out_a, i_out_a), (w_glu_a, i_glu_a), (w_glu_b, i_glu_b), (w_o, i_o) = [
        _layer_weight(wts, k, layer) for k in names]
    in_specs += [
        _const_spec((DEPTH, D_MODEL)),
        _const_spec((D_MODEL, IN_PROJ_WIDTH), i_in),
        _const_spec((CONV_WIDTH, CONV_A_WIDTH), layer),
        _const_spec((DEPTH, CONV_A_WIDTH)),
        _const_spec((CONV_A_WIDTH, D_MODEL), i_out_a),
        _const_spec((4, SSM_FLAT), layer),
        _const_spec((SSM_CHUNKS, S5_FOLD * CHUNK_CH, 2 * CHUNK_ST), layer),
        _const_spec((SSM_CHUNKS, S5_FOLD * CHUNK_CH, CHUNK_ST), layer),
        _const_spec((SSM_CHUNKS, S5_FOLD * CHUNK_CH, CHUNK_ST), layer),
        _const_spec((SSM_CHUNKS, (S5_FOLD - 1) * CHUNK_CH, (S5_FOLD - 1) * CHUNK_CH), layer),
        _const_spec((DEPTH, SSM_WIDTH)),
        _const_spec((SSM_WIDTH, D_MODEL), i_glu_a),
        _const_spec((SSM_WIDTH, D_MODEL), i_glu_b),
        _const_spec((D_MODEL, D_MODEL), i_o),
    ]
    out_specs = [row_spec, hbm_spec] + [_stack_spec(layer + 1, rb, SSM_FLAT)] * 2 + cast_out_specs
    out_shape = ([jax.ShapeDtypeStruct((n, D_MODEL), _F32),
                  jax.ShapeDtypeStruct((DEPTH, rb, CONV_WIDTH - 1, CONV_A_WIDTH), _F32)]
                 + [jax.ShapeDtypeStruct((layer + 1, rb, SSM_FLAT), _F32)] * 2 + cast_shapes)
    s_rows = rows // S5_FOLD if tb % S5_FOLD == 0 else rows
    scratch = [
        pltpu.VMEM((carry_rows + rows, CONV_A_WIDTH), _F32),
        pltpu.VMEM((rb + s_rows, SSM_FLAT), _F32),
        pltpu.VMEM((rb + s_rows, SSM_FLAT), _F32),
    ]
    scratch += _conv_state_scratch(rb, CONV_A_WIDTH)
    if x_native:
        scratch += _native_scratch(tb, rb, D_MODEL)
    return pl.pallas_call(
        functools.partial(_mixer_kernel, rb=rb, tb=tb, layer=layer, zero_conv=conv0 is None,
                          single_step=n == rows, n_cast=len(cast)),
        grid=(n // rows,),
        in_specs=in_specs,
        out_specs=out_specs,
        out_shape=out_shape,
        scratch_shapes=scratch,
        input_output_aliases=aliases,
        compiler_params=pltpu.CompilerParams(
            dimension_semantics=("arbitrary",), vmem_limit_bytes=VMEM_LIMIT_BYTES),
        name=f"mixer_l{layer}_rb{rb}",
    )(*operands,
      wts["norm_mix"], w_in, wts["conv_a_w"], wts["conv_a_b"], w_out_a,
      wts["abar"], wts["b2blk"], wts["c2_re"], wts["c2_im"], wts["fblk"], wts["d_skip"],
      w_glu_a, w_glu_b, w_o)


def _ffn_call(x, p, f0, prev, wts, layer, rb, tb, final_norm, p_native, out_native):
    n = x.shape[0]
    rows = rb * tb
    carry_rows = (CONV_WIDTH - 1) * rb
    hbm_spec = pl.BlockSpec(memory_space=pl.ANY)
    row_spec = pl.BlockSpec((rows, D_MODEL), lambda j: (j, 0))
    operands = [x, p]
    in_specs = [
        row_spec,
        hbm_spec if p_native else pl.BlockSpec((None, rows, PLE_DIM), lambda j: (layer, j, 0)),
    ]
    if f0 is not None:
        operands.append(f0)
        in_specs.append(hbm_spec)
    aliases = {}
    if prev is not None:
        aliases[len(operands)] = 1
        operands.append(prev)
        in_specs.append(hbm_spec)
    in_specs += [
        _const_spec((DEPTH, D_MODEL)),
        _const_spec((D_MODEL, 2 * D_FF), layer),
        _const_spec((CONV_WIDTH, D_FF), layer),
        _const_spec((DEPTH, D_FF)),
        _const_spec((D_FF, D_MODEL), layer),
        _const_spec((DEPTH, D_MODEL)),
        _const_spec((D_MODEL, D_MODEL), layer),
        _const_spec((PLE_DIM, D_MODEL), layer),
        _const_spec((1, D_MODEL)),
    ]
    out_specs = [hbm_spec if out_native else row_spec, hbm_spec]
    out_shape = [jax.ShapeDtypeStruct((rb, n // rb, D_MODEL) if out_native else (n, D_MODEL), _F32),
                 jax.ShapeDtypeStruct((DEPTH, rb, CONV_WIDTH - 1, D_FF), _F32)]
    scratch = [pltpu.VMEM((carry_rows + rows, FF_CHUNK_MAX), _F32)]
    scratch += _conv_state_scratch(rb, D_FF)
    if p_native:
        scratch += _native_scratch(tb, rb, PLE_DIM)
    if out_native:
        scratch += _native_scratch(tb, rb, D_MODEL)
    return pl.pallas_call(
        functools.partial(_ffn_kernel, rb=rb, tb=tb, layer=layer, zero_conv=f0 is None,
                          single_step=n == rows,
                          final_norm=final_norm, p_native=p_native, out_native=out_native),
        grid=(n // rows,),
        in_specs=in_specs,
        out_specs=out_specs,
        out_shape=out_shape,
        scratch_shapes=scratch,
        input_output_aliases=aliases,
        compiler_params=pltpu.CompilerParams(
            dimension_semantics=("arbitrary",), vmem_limit_bytes=VMEM_LIMIT_BYTES),
        name=f"ffn_l{layer}_rb{rb}",
    )(*operands,
      wts["norm_ffn"], wts["w_up"], wts["ffn_conv_w"], wts["ffn_conv_b"], wts["w_down"],
      wts["norm_ple"], wts["w_ple_gate"], wts["w_ple"], wts["norm_final"])


def _run_trunk(x, p, conv0, sre0, sim0, f0, wts, rb, tb_mixer, tb_ffn, native, cast_weights=False):
    mixer_states, ffn_state = [], None
    for layer in range(DEPTH):
        last = layer == DEPTH - 1
        ssm_layer = 0 if conv0 is None else layer
        cast = []
        if cast_weights and layer == 0:
            cast = ([(wts[k], 0, DEPTH) for k in FFN_MATMUL_WEIGHTS]
                    + [(wts[k][1], 1, DEPTH - 1) for k in MIXER_MATMUL_WEIGHTS])
        x, *mixer_states = _mixer_call(x, conv0, (sre0, sim0), mixer_states, wts, layer, ssm_layer,
                                       rb, tb_mixer[layer], x_native=native and layer == 0, cast=cast)
        if cast:
            rounded = mixer_states[3:]
            mixer_states = mixer_states[:3]
            wts.update(zip(FFN_MATMUL_WEIGHTS, rounded))
            for k, w in zip(MIXER_MATMUL_WEIGHTS, rounded[len(FFN_MATMUL_WEIGHTS):]):
                wts[k] = [wts[k][0]] + [(w, i) for i in range(DEPTH - 1)]
        x, ffn_state = _ffn_call(x, p, f0, ffn_state, wts, layer, rb, tb_ffn,
                                 final_norm=last, p_native=native, out_native=native and last)
    return (x, *mixer_states, ffn_state)


def kernel(x_prompt, x_sample, p_prompt, p_sample, state_conv_a, state_ssm_re, state_ssm_im, state_ffn_conv, norm_mix, w_in, conv_a_w, conv_a_b, w_out_a, log_dt, lam_re, lam_im, b_re, b_im, c_re, c_im, d_skip, w_glu_a, w_glu_b, w_o, norm_ffn, w_up, ffn_conv_w, ffn_conv_b, w_down, norm_ple, w_ple_gate, w_ple, norm_final):
    abar, b2blk, c2_re, c2_im, fblk = _ssm_prep(log_dt, lam_re, lam_im, b_re, b_im, c_re, c_im)
    mixer_f32 = (w_in, w_out_a, w_glu_a, w_glu_b, w_o)
    w_in_0, w_out_a_0, w_glu_a_0, w_glu_b_0, w_o_0 = _round_layer0(mixer_f32)
    wts = {
        "norm_mix": norm_mix,
        "w_in": [(w_in_0, 0), w_in],
        "conv_a_w": conv_a_w,
        "conv_a_b": conv_a_b,
        "w_out_a": [(w_out_a_0, 0), w_out_a],
        "abar": abar, "b2blk": b2blk, "c2_re": c2_re, "c2_im": c2_im, "fblk": fblk,
        "d_skip": d_skip,
        "w_glu_a": [(w_glu_a_0, 0), w_glu_a],
        "w_glu_b": [(w_glu_b_0, 0), w_glu_b],
        "w_o": [(w_o_0, 0), w_o],
        "norm_ffn": norm_ffn,
        "w_up": w_up,
        "ffn_conv_w": ffn_conv_w,
        "ffn_conv_b": ffn_conv_b,
        "w_down": w_down,
        "norm_ple": norm_ple,
        "w_ple_gate": w_ple_gate,
        "w_ple": w_ple,
        "norm_final": norm_final.reshape(1, D_MODEL),
    }

    bp = x_prompt.shape[0]
    zs = jnp.zeros((1, bp, SSM_FLAT), _F32)
    y_prompt, conv_p, sre_p, sim_p, ffn_p = _run_trunk(
        x_prompt, p_prompt, None, zs, zs, None, wts, bp, MIXER_TIME_BLOCKS, FFN_TIME_BLOCK,
        native=True, cast_weights=True)

    bs, sseq, _ = x_sample.shape
    xs = x_sample.transpose(1, 0, 2).reshape(sseq * bs, D_MODEL)
    ps = p_sample.transpose(0, 2, 1, 3).reshape(DEPTH, sseq * bs, PLE_DIM)
    ys, conv_s, sre_s, sim_s, ffn_s = _run_trunk(
        xs, ps, state_conv_a,
        state_ssm_re.reshape(DEPTH, bs, SSM_FLAT), state_ssm_im.reshape(DEPTH, bs, SSM_FLAT),
        state_ffn_conv, wts, bs, (sseq,) * DEPTH, sseq, native=False)
    y_sample = ys.reshape(sseq, bs, D_MODEL).transpose(1, 0, 2)

    def ssm_state(s, b):
        return s.reshape(DEPTH, b, SSM_GROUPS, SSM_STATE)

    return (y_prompt, y_sample,
            conv_p, ssm_state(sre_p, bp), ssm_state(sim_p, bp), ffn_p,
            conv_s, ssm_state(sre_s, bs), ssm_state(sim_s, bs), ffn_s)
```

```python
import functools
import math

import jax
import jax.numpy as jnp
from jax import lax
from jax.experimental import pallas as pl
from jax.experimental.pallas import tpu as pltpu

D_MODEL = 1024
DEPTH = 2
CONV_WIDTH = 3
CONV_A_WIDTH = 512
SSM_WIDTH = 512
SSM_GROUP = 16
SSM_GROUPS = 32
SSM_STATE = 64
SSM_FLAT = SSM_GROUPS * SSM_STATE
D_FF = 2816
PLE_DIM = 256
NORM_EPS = 1e-6
IN_PROJ_WIDTH = 4096

SSM_CHUNKS = 4
CHUNK_CH = SSM_WIDTH // SSM_CHUNKS
CHUNK_ST = SSM_FLAT // SSM_CHUNKS

FF_CHUNKS = ((0, 1024), (1024, 2048), (2048, 2816))
FF_CHUNK_MAX = 1024
FF_STREAM_CHUNK = 256

MIXER_TIME_BLOCKS = (64, 128)
FFN_TIME_BLOCK = 128
MIXER_MATMUL_WEIGHTS = ("w_in", "w_out_a", "w_glu_a", "w_glu_b", "w_o")
FFN_MATMUL_WEIGHTS = ("w_up", "w_down", "w_ple_gate", "w_ple")
ROUND_CHUNKS = 8
VMEM_LIMIT_BYTES = 56 * 1024 * 1024

_BF16 = jnp.bfloat16
_F32 = jnp.float32


def _dot(a, b):
    return jnp.dot(a, b, preferred_element_type=_F32)


def _dot_nt(a, b):
    return lax.dot_general(a, b, (((1,), (1,)), ((), ())), preferred_element_type=_F32)


def _dot_nt_split(a, b):
    a_hi = a.astype(_BF16)
    b_hi = b.astype(_BF16)
    a_lo = (a - a_hi.astype(_F32)).astype(_BF16)
    b_lo = (b - b_hi.astype(_F32)).astype(_BF16)
    return _dot_nt(a_hi, b_hi) + _dot_nt(a_hi, b_lo) + _dot_nt(a_lo, b_hi)


def _rms(x, g):
    ms = jnp.mean(x * x, axis=-1, keepdims=True)
    return x * lax.rsqrt(ms + NORM_EPS) * g


def _gelu(x):
    c = math.sqrt(2.0 / math.pi)
    t = jnp.tanh(x * (c + (0.044715 * c) * (x * x)))
    return x * (0.5 + 0.5 * t)


def _sigmoid(x):
    return 0.5 * jnp.tanh(0.5 * x) + 0.5


def _seq_copies(hbm_ref, lead, buf, sem, step, slot, tb, to_hbm):
    copies = []
    for b in range(buf.shape[2]):
        hbm_view = hbm_ref.at[lead + (b, pl.ds(step * tb, tb))]
        vmem_view = buf.at[slot, :, b]
        src, dst = (vmem_view, hbm_view) if to_hbm else (hbm_view, vmem_view)
        copies.append(pltpu.make_async_copy(src, dst, sem.at[slot]))
    return copies


def _fetch_time_major(hbm_ref, lead, buf, sem, tb):
    j = pl.program_id(0)
    slot = j % 2

    @pl.when(j == 0)
    def _():
        for c in _seq_copies(hbm_ref, lead, buf, sem, 0, 0, tb, False):
            c.start()

    @pl.when(j + 1 < pl.num_programs(0))
    def _():
        for c in _seq_copies(hbm_ref, lead, buf, sem, j + 1, 1 - slot, tb, False):
            c.start()

    for c in _seq_copies(hbm_ref, lead, buf, sem, j, slot, tb, False):
        c.wait()


def _read_time_major(buf):
    tb, nb, width = buf.shape[1:]
    return buf[pl.program_id(0) % 2].reshape(tb * nb, width)


def _store_batch_major(val, hbm_ref, buf, sem, tb):
    j = pl.program_id(0)
    slot = j % 2
    buf[slot] = val.reshape(tb, buf.shape[2], buf.shape[3])
    for c in _seq_copies(hbm_ref, (), buf, sem, j, slot, tb, True):
        c.start()

    @pl.when(j >= 1)
    def _():
        for c in _seq_copies(hbm_ref, (), buf, sem, j - 1, 1 - slot, tb, True):
            c.wait()

    @pl.when(j == pl.num_programs(0) - 1)
    def _():
        for c in _seq_copies(hbm_ref, (), buf, sem, j, slot, tb, True):
            c.wait()


def _conv_state_copies(hbm_ref, slot, carry, sem, to_hbm):
    copies = []
    for k in range(CONV_WIDTH - 1):
        hbm_view = hbm_ref.at[slot, :, k, :]
        src, dst = (carry.at[k], hbm_view) if to_hbm else (hbm_view, carry.at[k])
        copies.append(pltpu.make_async_copy(src, dst, sem.at[slot, k]))
    return copies


def _at_step(first, single_step, body):
    if single_step:
        body()
    else:
        step = 0 if first else pl.num_programs(0) - 1
        pl.when(pl.program_id(0) == step)(body)


def _conv_state_fetch(state_hbm, layer, carry, sem, single_step, wait):
    def body():
        if state_hbm is None:
            if not wait:
                carry[...] = jnp.zeros(carry.shape, carry.dtype)
        else:
            for c in _conv_state_copies(state_hbm, layer, carry, sem, False):
                c.wait() if wait else c.start()

    _at_step(True, single_step, body)


def _conv_state_store(carry, out_hbm, layer, sem, single_step, wait):
    def body():
        for slot in range(layer, out_hbm.shape[0]):
            for c in _conv_state_copies(out_hbm, slot, carry, sem, True):
                c.wait() if wait else c.start()

    _at_step(False, single_step, body)


def _ssm_prep_kernel(logdt_ref, lre_ref, lim_ref, bre_ref, bim_ref, cre_ref, cim_ref,
                     abar_ref, b2blk_ref, c2_re_ref, c2_im_ref, fblk_ref):
    dt = jnp.exp(logdt_ref[...])
    lre = lre_ref[...]
    lim = lim_ref[...]
    mag = jnp.exp(lre * dt)
    ang = lim * dt
    ar = mag * jnp.cos(ang)
    ai = mag * jnp.sin(ang)
    abar_ref[0:1, :] = ar
    abar_ref[1:2, :] = ai
    abar_ref[2:3, :] = ar * ar - ai * ai
    abar_ref[3:4, :] = 2.0 * (ar * ai)
    den = lre * lre + lim * lim
    nr = ar - 1.0
    coef_re = (nr * lre + ai * lim) / den
    coef_im = (ai * lre - nr * lim) / den
    bre = bre_ref[...]
    bim = bim_ref[...]
    bbar_re = coef_re * bre - coef_im * bim
    bbar_im = coef_re * bim + coef_im * bre
    abb_re = ar * bbar_re - ai * bbar_im
    abb_im = ar * bbar_im + ai * bbar_re
    row = lax.broadcasted_iota(jnp.int32, (CHUNK_CH, CHUNK_ST), 0)
    lane = lax.broadcasted_iota(jnp.int32, (CHUNK_CH, CHUNK_ST), 1)
    same_group = (row // SSM_GROUP) == (lane // SSM_STATE)
    reps = CHUNK_CH // SSM_GROUP
    for c in range(SSM_CHUNKS):
        cols = slice(c * CHUNK_ST, (c + 1) * CHUNK_ST)
        rows = slice(c * CHUNK_CH, (c + 1) * CHUNK_CH)

        def b_block(v, cols=cols):
            return jnp.where(same_group, jnp.concatenate([v[:, cols]] * reps, axis=0), 0.0)

        b_re = b_block(bbar_re)
        b_im = b_block(bbar_im)
        b2blk_ref[c, 0:CHUNK_CH, 0:CHUNK_ST] = b_block(abb_re).astype(_BF16)
        b2blk_ref[c, 0:CHUNK_CH, CHUNK_ST:2 * CHUNK_ST] = b_block(abb_im).astype(_BF16)
        b2blk_ref[c, CHUNK_CH:2 * CHUNK_CH, 0:CHUNK_ST] = b_re.astype(_BF16)
        b2blk_ref[c, CHUNK_CH:2 * CHUNK_CH, CHUNK_ST:2 * CHUNK_ST] = b_im.astype(_BF16)
        c_re = jnp.where(same_group, cre_ref[rows, :], 0.0)
        c_im = jnp.where(same_group, cim_ref[rows, :], 0.0)
        arc = ar[:, cols]
        aic = ai[:, cols]
        c2_re_ref[c, 0:CHUNK_CH, :] = c_re.astype(_BF16)
        c2_re_ref[c, CHUNK_CH:2 * CHUNK_CH, :] = (c_re * arc - c_im * aic).astype(_BF16)
        c2_im_ref[c, 0:CHUNK_CH, :] = (-c_im).astype(_BF16)
        c2_im_ref[c, CHUNK_CH:2 * CHUNK_CH, :] = (-(c_re * aic + c_im * arc)).astype(_BF16)
        feed = _dot_nt_split(b_re, c_re) - _dot_nt_split(b_im, c_im)
        fblk_ref[c] = feed.astype(_BF16)


def _ssm_prep(log_dt, lam_re, lam_im, b_re, b_im, c_re, c_im):
    logdt = jnp.repeat(log_dt, SSM_STATE, axis=1).reshape(DEPTH, 1, SSM_FLAT)
    lre = lam_re.reshape(DEPTH, 1, SSM_FLAT)
    lim = lam_im.reshape(DEPTH, 1, SSM_FLAT)
    bre = b_re.transpose(0, 3, 1, 2).reshape(DEPTH, SSM_GROUP, SSM_FLAT)
    bim = b_im.transpose(0, 3, 1, 2).reshape(DEPTH, SSM_GROUP, SSM_FLAT)
    reps = CHUNK_ST // SSM_STATE
    cre = jnp.tile(c_re.reshape(DEPTH, SSM_WIDTH, SSM_STATE), (1, 1, reps))
    cim = jnp.tile(c_im.reshape(DEPTH, SSM_WIDTH, SSM_STATE), (1, 1, reps))

    def spec(*shape):
        return pl.BlockSpec((None,) + shape, lambda i: (i,) + (0,) * len(shape))

    return pl.pallas_call(
        _ssm_prep_kernel,
        grid=(DEPTH,),
        in_specs=[spec(1, SSM_FLAT), spec(1, SSM_FLAT), spec(1, SSM_FLAT),
                  spec(SSM_GROUP, SSM_FLAT), spec(SSM_GROUP, SSM_FLAT),
                  spec(SSM_WIDTH, CHUNK_ST), spec(SSM_WIDTH, CHUNK_ST)],
        out_specs=[spec(4, SSM_FLAT),
                   spec(SSM_CHUNKS, 2 * CHUNK_CH, 2 * CHUNK_ST),
                   spec(SSM_CHUNKS, 2 * CHUNK_CH, CHUNK_ST),
                   spec(SSM_CHUNKS, 2 * CHUNK_CH, CHUNK_ST),
                   spec(SSM_CHUNKS, CHUNK_CH, CHUNK_CH)],
        out_shape=[jax.ShapeDtypeStruct((DEPTH, 4, SSM_FLAT), _F32),
                   jax.ShapeDtypeStruct((DEPTH, SSM_CHUNKS, 2 * CHUNK_CH, 2 * CHUNK_ST), _BF16),
                   jax.ShapeDtypeStruct((DEPTH, SSM_CHUNKS, 2 * CHUNK_CH, CHUNK_ST), _BF16),
                   jax.ShapeDtypeStruct((DEPTH, SSM_CHUNKS, 2 * CHUNK_CH, CHUNK_ST), _BF16),
                   jax.ShapeDtypeStruct((DEPTH, SSM_CHUNKS, CHUNK_CH, CHUNK_CH), _BF16)],
        name="ssm_prep",
    )(logdt, lre, lim, bre, bim, cre, cim)


def _round_kernel(*refs):
    half = len(refs) // 2
    for src_ref, dst_ref in zip(refs[:half], refs[half:]):
        dst_ref[...] = src_ref[...].astype(_BF16)


def _round_layer0(ws):
    specs = [pl.BlockSpec((1, w.shape[1] // ROUND_CHUNKS, w.shape[2]), lambda j: (0, j, 0)) for w in ws]
    return pl.pallas_call(
        _round_kernel,
        grid=(ROUND_CHUNKS,),
        in_specs=specs,
        out_specs=specs,
        out_shape=[jax.ShapeDtypeStruct((1,) + w.shape[1:], _BF16) for w in ws],
        name="round_mixer_l0",
    )(*ws)


def _mixer_kernel(x_ref, *refs, rb, tb, layer, zero_conv, single_step, n_cast):
    refs = list(refs)
    conv0_hbm = None if zero_conv else refs.pop(0)
    sre0_ref, sim0_ref = refs.pop(0), refs.pop(0)
    prev = [refs.pop(0) for _ in range(3)] if layer > 0 else []
    cast_in = [refs.pop(0) for _ in range(n_cast)]
    (g_ref, w_in_ref, cw_ref, cb_ref, w_out_a_ref,
     abar_ref, b2blk_ref, c2_re_ref, c2_im_ref, fblk_ref, dskip_ref,
     glu_a_ref, glu_b_ref, w_o_ref,
     out_ref, conv_out_hbm, sre_stack_ref, sim_stack_ref, *refs) = refs
    cast_out = [refs.pop(0) for _ in range(n_cast)]
    vbuf, s_re, s_im, conv_carry, conv_sem, *x_dma = refs
    for src_ref, dst_ref in zip(cast_in, cast_out):
        dst_ref[...] = src_ref[...].astype(_BF16)
    rows = rb * tb
    carry_rows = (CONV_WIDTH - 1) * rb
    sre_out_ref = sre_stack_ref.at[layer]
    sim_out_ref = sim_stack_ref.at[layer]

    @pl.when(pl.program_id(0) == 0)
    def _():
        for stack_ref, prev_ref in zip((sre_stack_ref, sim_stack_ref), prev[1:]):
            stack_ref[0:layer] = prev_ref[...]
        sre_out_ref[...] = sre0_ref[...]
        sim_out_ref[...] = sim0_ref[...]

    conv_io = (layer, conv_carry, conv_sem, single_step)
    _conv_state_fetch(conv0_hbm, *conv_io, wait=False)
    if not single_step:
        _conv_state_fetch(conv0_hbm, *conv_io, wait=True)
    if x_dma:
        _fetch_time_major(x_ref, (), *x_dma, tb)
        x = _read_time_major(x_dma[0])
    else:
        x = x_ref[...]
    xn = _rms(x, g_ref[layer:layer + 1, :]).astype(_BF16)

    def proj(lo, hi):
        return _dot(xn, w_in_ref[:, lo:hi])

    u = proj(1536, 2048)
    pair = 2 if tb % 2 == 0 else 1
    steps = tb // pair
    srows = steps * rb
    if pair == 2:
        u4 = u.reshape(steps, 2, rb, SSM_WIDTH)
        u_parts = [u4[:, i].reshape(srows, SSM_WIDTH) for i in range(2)]
    else:
        u_parts = [u]
    ub_parts = [v.astype(_BF16) for v in u_parts]
    s_re[0:rb, :] = sre_out_ref[...]
    s_im[0:rb, :] = sim_out_ref[...]
    for c in range(SSM_CHUNKS):
        ch = slice(c * CHUNK_CH, (c + 1) * CHUNK_CH)
        if pair == 2:
            inc = _dot(jnp.concatenate([ub_parts[0][:, ch], ub_parts[1][:, ch]], axis=1), b2blk_ref[c])
        else:
            inc = _dot(ub_parts[0][:, ch], b2blk_ref[c, CHUNK_CH:2 * CHUNK_CH, :])
        s_re[rb:rb + srows, c * CHUNK_ST:(c + 1) * CHUNK_ST] = inc[:, 0:CHUNK_ST]
        s_im[rb:rb + srows, c * CHUNK_ST:(c + 1) * CHUNK_ST] = inc[:, CHUNK_ST:2 * CHUNK_ST]

    a_row = 2 * (pair - 1)
    for c in range(SSM_CHUNKS):
        cols = slice(c * CHUNK_ST, (c + 1) * CHUNK_ST)
        ar = jnp.broadcast_to(abar_ref[a_row:a_row + 1, cols], (rb, CHUNK_ST))
        ai = jnp.broadcast_to(abar_ref[a_row + 1:a_row + 2, cols], (rb, CHUNK_ST))

        def step(k, carry, cols=cols, ar=ar, ai=ai):
            sr, si = carry
            r0 = pl.multiple_of(rb + k * rb, rb)
            nsr = ar * sr - ai * si + s_re[pl.ds(r0, rb), cols]
            nsi = ar * si + ai * sr + s_im[pl.ds(r0, rb), cols]
            s_re[pl.ds(r0, rb), cols] = nsr
            s_im[pl.ds(r0, rb), cols] = nsi
            return nsr, nsi

        sr, si = lax.fori_loop(0, steps, step, (s_re[0:rb, cols], s_im[0:rb, cols]), unroll=True)
        sre_out_ref[:, cols] = sr
        sim_out_ref[:, cols] = si

    h = proj(0, 512)
    gate_c = proj(1024, 1536)
    if single_step:
        _conv_state_fetch(conv0_hbm, *conv_io, wait=True)
    vbuf[0:rb, :] = conv_carry[0]
    vbuf[rb:carry_rows, :] = conv_carry[1]
    vbuf[carry_rows:carry_rows + rows, :] = gate_c * h
    conv_y = (cb_ref[layer:layer + 1, :]
              + vbuf[0:rows, :] * cw_ref[0:1, :]
              + vbuf[rb:rb + rows, :] * cw_ref[1:2, :]
              + vbuf[2 * rb:2 * rb + rows, :] * cw_ref[2:3, :])
    conv_carry[0] = vbuf[rows:rows + rb, :]
    conv_carry[1] = vbuf[rows + rb:rows + carry_rows, :]
    if single_step:
        _conv_state_store(conv_carry, conv_out_hbm, layer, conv_sem, single_step, wait=False)
    gate_b = proj(512, 1024)
    z_a = _dot((gate_b * conv_y).astype(_BF16), w_out_a_ref[...])
    merged = _sigmoid(proj(2048, 3072)) * z_a

    y_parts = [[] for _ in range(pair)]
    for c in range(SSM_CHUNKS):
        ch = slice(c * CHUNK_CH, (c + 1) * CHUNK_CH)
        cols = slice(c * CHUNK_ST, (c + 1) * CHUNK_ST)
        if pair == 2:
            y2 = (_dot_nt(s_re[:, cols].astype(_BF16), c2_re_ref[c])
                  + _dot_nt(s_im[:, cols].astype(_BF16), c2_im_ref[c]))
            y_parts[0].append(y2[0:srows, CHUNK_CH:2 * CHUNK_CH] + _dot(ub_parts[0][:, ch], fblk_ref[c]))
            y_parts[1].append(y2[rb:rb + srows, 0:CHUNK_CH])
        else:
            y_parts[0].append(
                _dot_nt(s_re[rb:rb + srows, cols].astype(_BF16), c2_re_ref[c, 0:CHUNK_CH, :])
                + _dot_nt(s_im[rb:rb + srows, cols].astype(_BF16), c2_im_ref[c, 0:CHUNK_CH, :]))
    d_skip = dskip_ref[layer:layer + 1, :]
    ssm_parts = [jnp.concatenate(yp, axis=1) + d_skip * up for yp, up in zip(y_parts, u_parts)]
    if pair == 2:
        ssm_y = jnp.stack([v.reshape(steps, rb, SSM_WIDTH) for v in ssm_parts],
                          axis=1).reshape(rows, SSM_WIDTH)
    else:
        ssm_y = ssm_parts[0]
    sy = _gelu(ssm_y).astype(_BF16)
    z_b = _dot(sy, glu_a_ref[...]) * _sigmoid(_dot(sy, glu_b_ref[...]))
    merged = (merged + _sigmoid(proj(3072, 4096)) * z_b).astype(_BF16)
    out_ref[...] = x + _dot(merged, w_o_ref[...])
    if not single_step:
        _conv_state_store(conv_carry, conv_out_hbm, layer, conv_sem, single_step, wait=False)
    _conv_state_store(conv_carry, conv_out_hbm, layer, conv_sem, single_step, wait=True)


def _ffn_kernel(x_ref, p_ref, *refs, rb, tb, layer, zero_conv, single_step, final_norm, p_native,
                out_native):
    refs = list(refs)
    f0_hbm = None if zero_conv else refs.pop(0)
    if layer > 0:
        refs.pop(0)
    (g2_ref, w_up_ref, fcw_ref, fcb_ref, w_down_ref,
     g3_ref, w_pg_ref, w_ple_ref, gf_ref,
     out_ref, fconv_out_hbm,
     fbuf, fcarry, fsem, *dma) = refs
    rows = rb * tb
    carry_rows = (CONV_WIDTH - 1) * rb
    p_dma = [dma.pop(0), dma.pop(0)] if p_native else None
    out_dma = [dma.pop(0), dma.pop(0)] if out_native else None

    conv_io = (layer, fcarry, fsem, single_step)
    _conv_state_fetch(f0_hbm, *conv_io, wait=False)
    if not single_step:
        _conv_state_fetch(f0_hbm, *conv_io, wait=True)
    if p_native:
        _fetch_time_major(p_ref, (layer,), *p_dma, tb)

    x = x_ref[...]
    xn = _rms(x, g2_ref[layer:layer + 1, :]).astype(_BF16)
    acc = x
    if single_step:
        _conv_state_fetch(f0_hbm, *conv_io, wait=True)
    pending = None
    for lo, hi in FF_CHUNKS:
        wc = hi - lo
        up_a = _dot(xn, w_up_ref[:, lo:hi])
        up_b = _dot(xn, w_up_ref[:, D_FF + lo:D_FF + hi])
        if pending is not None:
            acc = acc + _dot(pending[0], w_down_ref[pending[1]:pending[2], :])
        fbuf[0:rb, 0:wc] = fcarry[0, :, lo:hi]
        fbuf[rb:carry_rows, 0:wc] = fcarry[1, :, lo:hi]
        fbuf[carry_rows:carry_rows + rows, 0:wc] = up_a
        conv_a = (fcb_ref[layer:layer + 1, lo:hi]
                  + fbuf[0:rows, 0:wc] * fcw_ref[0:1, lo:hi]
                  + fbuf[rb:rb + rows, 0:wc] * fcw_ref[1:2, lo:hi]
                  + fbuf[2 * rb:2 * rb + rows, 0:wc] * fcw_ref[2:3, lo:hi])
        fcarry[0, :, lo:hi] = fbuf[rows:rows + rb, 0:wc]
        fcarry[1, :, lo:hi] = fbuf[rows + rb:rows + carry_rows, 0:wc]
        pending = ((_gelu(conv_a) * up_b).astype(_BF16), lo, hi)
    acc = acc + _dot(pending[0], w_down_ref[pending[1]:pending[2], :])
    if single_step:
        _conv_state_store(fcarry, fconv_out_hbm, layer, fsem, single_step, wait=False)
    x2 = acc
    xn3 = _rms(x2, g3_ref[layer:layer + 1, :]).astype(_BF16)
    gate = _sigmoid(_dot(xn3, w_pg_ref[...]))
    p = _read_time_major(p_dma[0]) if p_native else p_ref[...]
    pe = _dot(p.astype(_BF16), w_ple_ref[...])
    x3 = x2 + gate * pe
    if final_norm:
        x3 = _rms(x3, gf_ref[...])
    if out_native:
        _store_batch_major(x3, out_ref, *out_dma, tb)
    else:
        out_ref[...] = x3
    if not single_step:
        _conv_state_store(fcarry, fconv_out_hbm, layer, fsem, single_step, wait=False)
    _conv_state_store(fcarry, fconv_out_hbm, layer, fsem, single_step, wait=True)


def _ffn_stream_kernel(x_ref, p_ref, f0_hbm, *refs, rb, layer, final_norm):
    refs = list(refs)
    if layer > 0:
        refs.pop(0)
    (g2_ref, wa_ref, wb_ref, fcw_ref, fcb_ref, wd_ref, g3_ref, w_pg_ref, w_ple_ref, gf_ref,
     out_ref, fconv_out_hbm, xn_s, acc_s, fbuf, fcarry, fsem) = refs
    c = pl.program_id(0)
    conv_io = (layer, fcarry, fsem, False)
    _conv_state_fetch(f0_hbm, *conv_io, wait=False)
    _conv_state_fetch(f0_hbm, *conv_io, wait=True)

    @pl.when(c == 0)
    def _():
        x = x_ref[...]
        xn_s[...] = _rms(x, g2_ref[layer:layer + 1, :]).astype(_BF16)
        acc_s[...] = x

    cols = pl.ds(pl.multiple_of(c * FF_STREAM_CHUNK, FF_STREAM_CHUNK), FF_STREAM_CHUNK)
    xn = xn_s[...]
    up_a = _dot(xn, wa_ref[...])
    up_b = _dot(xn, wb_ref[...])
    fbuf[0:rb, :] = fcarry[0, :, cols]
    fbuf[rb:2 * rb, :] = fcarry[1, :, cols]
    fbuf[2 * rb:3 * rb, :] = up_a
    conv_a = (fcb_ref[layer:layer + 1, :]
              + fbuf[0:rb, :] * fcw_ref[0:1, :]
              + fbuf[rb:2 * rb, :] * fcw_ref[1:2, :]
              + fbuf[2 * rb:3 * rb, :] * fcw_ref[2:3, :])
    fcarry[0, :, cols] = fbuf[rb:2 * rb, :]
    fcarry[1, :, cols] = fbuf[2 * rb:3 * rb, :]
    hid = (_gelu(conv_a) * up_b).astype(_BF16)
    acc_s[...] += _dot(hid, wd_ref[...])

    @pl.when(c == pl.num_programs(0) - 1)
    def _():
        x2 = acc_s[...]
        xn3 = _rms(x2, g3_ref[layer:layer + 1, :]).astype(_BF16)
        gate = _sigmoid(_dot(xn3, w_pg_ref[...]))
        pe = _dot(p_ref[...].astype(_BF16), w_ple_ref[...])
        x3 = x2 + gate * pe
        if final_norm:
            x3 = _rms(x3, gf_ref[...])
        out_ref[...] = x3

    _conv_state_store(fcarry, fconv_out_hbm, layer, fsem, False, wait=False)
    _conv_state_store(fcarry, fconv_out_hbm, layer, fsem, False, wait=True)


def _const_spec(shape, layer=None):
    if layer is None:
        return pl.BlockSpec(shape, lambda j: (0,) * len(shape), pipeline_mode=pl.Buffered(1))
    return pl.BlockSpec((None,) + shape, lambda j: (layer,) + (0,) * len(shape),
                        pipeline_mode=pl.Buffered(1))


def _layer_weight(wts, name, layer):
    w = wts[name]
    return w[layer] if isinstance(w, list) else (w, layer)


def _native_scratch(tb, rb, width):
    return [pltpu.VMEM((2, tb, rb, width), _F32), pltpu.SemaphoreType.DMA((2,))]


def _stack_spec(n_layers, rb, width):
    return pl.BlockSpec((n_layers, rb, width), lambda j: (0, 0, 0))


def _conv_state_scratch(rb, width):
    return [pltpu.VMEM((CONV_WIDTH - 1, rb, width), _F32), pltpu.SemaphoreType.DMA((DEPTH, CONV_WIDTH - 1))]


def _mixer_call(x, conv0, ssm0, prev, wts, layer, ssm_layer, rb, tb, x_native, cast=()):
    n = x.shape[0] * x.shape[1] if x_native else x.shape[0]
    rows = rb * tb
    carry_rows = (CONV_WIDTH - 1) * rb
    hbm_spec = pl.BlockSpec(memory_space=pl.ANY)
    row_spec = pl.BlockSpec((rows, D_MODEL), lambda j: (j, 0))
    operands = [x]
    in_specs = [hbm_spec if x_native else row_spec]
    if conv0 is not None:
        operands.append(conv0)
        in_specs.append(hbm_spec)
    operands += list(ssm0)
    in_specs += [_const_spec((rb, SSM_FLAT), ssm_layer)] * 2
    aliases = {}
    if prev:
        aliases[len(operands)] = 1
        operands += list(prev)
        in_specs += [hbm_spec] + [_const_spec((layer, rb, SSM_FLAT))] * 2
    steps = n // rows
    cast_out_specs, cast_shapes = [], []
    for w, first, count in cast:
        per = steps // count
        assert steps % count == 0 and w.shape[1] % (16 * per) == 0, (steps, count, w.shape)
        block = (None, w.shape[1] // per, w.shape[2])
        operands.append(w)
        in_specs.append(pl.BlockSpec(block, lambda j, per=per, first=first: (first + j // per, j % per, 0)))
        cast_out_specs.append(pl.BlockSpec(block, lambda j, per=per: (j // per, j % per, 0)))
        cast_shapes.append(jax.ShapeDtypeStruct((count,) + w.shape[1:], _BF16))
    names = ("w_in", "w_out_a", "w_glu_a", "w_glu_b", "w_o")
    (w_in, i_in), (w_out_a, i_out_a), (w_glu_a, i_glu_a), (w_glu_b, i_glu_b), (w_o, i_o) = [
        _layer_weight(wts, k, layer) for k in names]
    in_specs += [
        _const_spec((DEPTH, D_MODEL)),
        _const_spec((D_MODEL, IN_PROJ_WIDTH), i_in),
        _const_spec((CONV_WIDTH, CONV_A_WIDTH), layer),
        _const_spec((DEPTH, CONV_A_WIDTH)),
        _const_spec((CONV_A_WIDTH, D_MODEL), i_out_a),
        _const_spec((4, SSM_FLAT), layer),
        _const_spec((SSM_CHUNKS, 2 * CHUNK_CH, 2 * CHUNK_ST), layer),
        _const_spec((SSM_CHUNKS, 2 * CHUNK_CH, CHUNK_ST), layer),
        _const_spec((SSM_CHUNKS, 2 * CHUNK_CH, CHUNK_ST), layer),
        _const_spec((SSM_CHUNKS, CHUNK_CH, CHUNK_CH), layer),
        _const_spec((DEPTH, SSM_WIDTH)),
        _const_spec((SSM_WIDTH, D_MODEL), i_glu_a),
        _const_spec((SSM_WIDTH, D_MODEL), i_glu_b),
        _const_spec((D_MODEL, D_MODEL), i_o),
    ]
    out_specs = [row_spec, hbm_spec] + [_stack_spec(layer + 1, rb, SSM_FLAT)] * 2 + cast_out_specs
    out_shape = ([jax.ShapeDtypeStruct((n, D_MODEL), _F32),
                  jax.ShapeDtypeStruct((DEPTH, rb, CONV_WIDTH - 1, CONV_A_WIDTH), _F32)]
                 + [jax.ShapeDtypeStruct((layer + 1, rb, SSM_FLAT), _F32)] * 2 + cast_shapes)
    s_rows = rows // 2 if tb % 2 == 0 else rows
    scratch = [
        pltpu.VMEM((carry_rows + rows, CONV_A_WIDTH), _F32),
        pltpu.VMEM((rb + s_rows, SSM_FLAT), _F32),
        pltpu.VMEM((rb + s_rows, SSM_FLAT), _F32),
    ]
    scratch += _conv_state_scratch(rb, CONV_A_WIDTH)
    if x_native:
        scratch += _native_scratch(tb, rb, D_MODEL)
    return pl.pallas_call(
        functools.partial(_mixer_kernel, rb=rb, tb=tb, layer=layer, zero_conv=conv0 is None,
                          single_step=n == rows, n_cast=len(cast)),
        grid=(n // rows,),
        in_specs=in_specs,
        out_specs=out_specs,
        out_shape=out_shape,
        scratch_shapes=scratch,
        input_output_aliases=aliases,
        compiler_params=pltpu.CompilerParams(
            dimension_semantics=("arbitrary",), vmem_limit_bytes=VMEM_LIMIT_BYTES),
        name=f"mixer_l{layer}_rb{rb}",
    )(*operands,
      wts["norm_mix"], w_in, wts["conv_a_w"], wts["conv_a_b"], w_out_a,
      wts["abar"], wts["b2blk"], wts["c2_re"], wts["c2_im"], wts["fblk"], wts["d_skip"],
      w_glu_a, w_glu_b, w_o)


def _ffn_call(x, p, f0, prev, wts, layer, rb, tb, final_norm, p_native, out_native):
    n = x.shape[0]
    rows = rb * tb
    carry_rows = (CONV_WIDTH - 1) * rb
    hbm_spec = pl.BlockSpec(memory_space=pl.ANY)
    row_spec = pl.BlockSpec((rows, D_MODEL), lambda j: (j, 0))
    operands = [x, p]
    in_specs = [
        row_spec,
        hbm_spec if p_native else pl.BlockSpec((None, rows, PLE_DIM), lambda j: (layer, j, 0)),
    ]
    if f0 is not None:
        operands.append(f0)
        in_specs.append(hbm_spec)
    aliases = {}
    if prev is not None:
        aliases[len(operands)] = 1
        operands.append(prev)
        in_specs.append(hbm_spec)
    in_specs += [
        _const_spec((DEPTH, D_MODEL)),
        _const_spec((D_MODEL, 2 * D_FF), layer),
        _const_spec((CONV_WIDTH, D_FF), layer),
        _const_spec((DEPTH, D_FF)),
        _const_spec((D_FF, D_MODEL), layer),
        _const_spec((DEPTH, D_MODEL)),
        _const_spec((D_MODEL, D_MODEL), layer),
        _const_spec((PLE_DIM, D_MODEL), layer),
        _const_spec((1, D_MODEL)),
    ]
    out_specs = [hbm_spec if out_native else row_spec, hbm_spec]
    out_shape = [jax.ShapeDtypeStruct((rb, n // rb, D_MODEL) if out_native else (n, D_MODEL), _F32),
                 jax.ShapeDtypeStruct((DEPTH, rb, CONV_WIDTH - 1, D_FF), _F32)]
    scratch = [pltpu.VMEM((carry_rows + rows, FF_CHUNK_MAX), _F32)]
    scratch += _conv_state_scratch(rb, D_FF)
    if p_native:
        scratch += _native_scratch(tb, rb, PLE_DIM)
    if out_native:
        scratch += _native_scratch(tb, rb, D_MODEL)
    return pl.pallas_call(
        functools.partial(_ffn_kernel, rb=rb, tb=tb, layer=layer, zero_conv=f0 is None,
                          single_step=n == rows,
                          final_norm=final_norm, p_native=p_native, out_native=out_native),
        grid=(n // rows,),
        in_specs=in_specs,
        out_specs=out_specs,
        out_shape=out_shape,
        scratch_shapes=scratch,
        input_output_aliases=aliases,
        compiler_params=pltpu.CompilerParams(
            dimension_semantics=("arbitrary",), vmem_limit_bytes=VMEM_LIMIT_BYTES),
        name=f"ffn_l{layer}_rb{rb}",
    )(*operands,
      wts["norm_ffn"], wts["w_up"], wts["ffn_conv_w"], wts["ffn_conv_b"], wts["w_down"],
      wts["norm_ple"], wts["w_ple_gate"], wts["w_ple"], wts["norm_final"])


def _ffn_stream_call(x, p, f0, prev, wts, layer, final_norm):
    rb = x.shape[0]
    chunks = D_FF // FF_STREAM_CHUNK
    hbm_spec = pl.BlockSpec(memory_space=pl.ANY)
    operands = [x, p, f0]
    in_specs = [_const_spec((rb, D_MODEL)), _const_spec((rb, PLE_DIM), layer), hbm_spec]
    aliases = {}
    if prev is not None:
        aliases[len(operands)] = 1
        operands.append(prev)
        in_specs.append(hbm_spec)
    in_specs += [
        _const_spec((DEPTH, D_MODEL)),
        pl.BlockSpec((None, D_MODEL, FF_STREAM_CHUNK), lambda c: (layer, 0, c)),
        pl.BlockSpec((None, D_MODEL, FF_STREAM_CHUNK), lambda c: (layer, 0, c + chunks)),
        pl.BlockSpec((None, CONV_WIDTH, FF_STREAM_CHUNK), lambda c: (layer, 0, c)),
        pl.BlockSpec((DEPTH, FF_STREAM_CHUNK), lambda c: (0, c)),
        pl.BlockSpec((None, FF_STREAM_CHUNK, D_MODEL), lambda c: (layer, c, 0)),
        _const_spec((DEPTH, D_MODEL)),
        _const_spec((D_MODEL, D_MODEL), layer),
        _const_spec((PLE_DIM, D_MODEL), layer),
        _const_spec((1, D_MODEL)),
    ]
    return pl.pallas_call(
        functools.partial(_ffn_stream_kernel, rb=rb, layer=layer, final_norm=final_norm),
        grid=(chunks,),
        in_specs=in_specs,
        out_specs=[pl.BlockSpec((rb, D_MODEL), lambda c: (0, 0)), hbm_spec],
        out_shape=[jax.ShapeDtypeStruct((rb, D_MODEL), _F32),
                   jax.ShapeDtypeStruct((DEPTH, rb, CONV_WIDTH - 1, D_FF), _F32)],
        scratch_shapes=[pltpu.VMEM((rb, D_MODEL), _BF16), pltpu.VMEM((rb, D_MODEL), _F32),
                        pltpu.VMEM((CONV_WIDTH * rb, FF_STREAM_CHUNK), _F32)]
        + _conv_state_scratch(rb, D_FF),
        input_output_aliases=aliases,
        compiler_params=pltpu.CompilerParams(
            dimension_semantics=("arbitrary",), vmem_limit_bytes=VMEM_LIMIT_BYTES),
        name=f"ffn_stream_l{layer}_rb{rb}",
    )(*operands,
      wts["norm_ffn"], wts["w_up"], wts["w_up"], wts["ffn_conv_w"], wts["ffn_conv_b"], wts["w_down"],
      wts["norm_ple"], wts["w_ple_gate"], wts["w_ple"], wts["norm_final"])


def _run_trunk(x, p, conv0, sre0, sim0, f0, wts, rb, tb_mixer, tb_ffn, native, cast_weights=False):
    mixer_states, ffn_state = [], None
    for layer in range(DEPTH):
        last = layer == DEPTH - 1
        ssm_layer = 0 if conv0 is None else layer
        cast = []
        if cast_weights and layer == 0:
            cast = ([(wts[k], 0, DEPTH) for k in FFN_MATMUL_WEIGHTS]
                    + [(wts[k][1], 1, DEPTH - 1) for k in MIXER_MATMUL_WEIGHTS])
        x, *mixer_states = _mixer_call(x, conv0, (sre0, sim0), mixer_states, wts, layer, ssm_layer,
                                       rb, tb_mixer[layer], x_native=native and layer == 0, cast=cast)
        if cast:
            rounded = mixer_states[3:]
            mixer_states = mixer_states[:3]
            wts.update(zip(FFN_MATMUL_WEIGHTS, rounded))
            for k, w in zip(MIXER_MATMUL_WEIGHTS, rounded[len(FFN_MATMUL_WEIGHTS):]):
                wts[k] = [wts[k][0]] + [(w, i) for i in range(DEPTH - 1)]
        if not native and tb_ffn == 1 and f0 is not None:
            x, ffn_state = _ffn_stream_call(x, p, f0, ffn_state, wts, layer, final_norm=last)
        else:
            x, ffn_state = _ffn_call(x, p, f0, ffn_state, wts, layer, rb, tb_ffn,
                                     final_norm=last, p_native=native, out_native=native and last)
    return (x, *mixer_states, ffn_state)


def kernel(x_prompt, x_sample, p_prompt, p_sample, state_conv_a, state_ssm_re, state_ssm_im, state_ffn_conv, norm_mix, w_in, conv_a_w, conv_a_b, w_out_a, log_dt, lam_re, lam_im, b_re, b_im, c_re, c_im, d_skip, w_glu_a, w_glu_b, w_o, norm_ffn, w_up, ffn_conv_w, ffn_conv_b, w_down, norm_ple, w_ple_gate, w_ple, norm_final):
    abar, b2blk, c2_re, c2_im, fblk = _ssm_prep(log_dt, lam_re, lam_im, b_re, b_im, c_re, c_im)
    mixer_f32 = (w_in, w_out_a, w_glu_a, w_glu_b, w_o)
    w_in_0, w_out_a_0, w_glu_a_0, w_glu_b_0, w_o_0 = _round_layer0(mixer_f32)
    wts = {
        "norm_mix": norm_mix,
        "w_in": [(w_in_0, 0), w_in],
        "conv_a_w": conv_a_w,
        "conv_a_b": conv_a_b,
        "w_out_a": [(w_out_a_0, 0), w_out_a],
        "abar": abar, "b2blk": b2blk, "c2_re": c2_re, "c2_im": c2_im, "fblk": fblk,
        "d_skip": d_skip,
        "w_glu_a": [(w_glu_a_0, 0), w_glu_a],
        "w_glu_b": [(w_glu_b_0, 0), w_glu_b],
        "w_o": [(w_o_0, 0), w_o],
        "norm_ffn": norm_ffn,
        "w_up": w_up,
        "ffn_conv_w": ffn_conv_w,
        "ffn_conv_b": ffn_conv_b,
        "w_down": w_down,
        "norm_ple": norm_ple,
        "w_ple_gate": w_ple_gate,
        "w_ple": w_ple,
        "norm_final": norm_final.reshape(1, D_MODEL),
    }

    bp = x_prompt.shape[0]
    zs = jnp.zeros((1, bp, SSM_FLAT), _F32)
    y_prompt, conv_p, sre_p, sim_p, ffn_p = _run_trunk(
        x_prompt, p_prompt, None, zs, zs, None, wts, bp, MIXER_TIME_BLOCKS, FFN_TIME_BLOCK,
        native=True, cast_weights=True)

    bs, sseq, _ = x_sample.shape
    xs = x_sample.transpose(1, 0, 2).reshape(sseq * bs, D_MODEL)
    ps = p_sample.transpose(0, 2, 1, 3).reshape(DEPTH, sseq * bs, PLE_DIM)
    ys, conv_s, sre_s, sim_s, ffn_s = _run_trunk(
        xs, ps, state_conv_a,
        state_ssm_re.reshape(DEPTH, bs, SSM_FLAT), state_ssm_im.reshape(DEPTH, bs, SSM_FLAT),
        state_ffn_conv, wts, bs, (sseq,) * DEPTH, sseq, native=False)
    y_sample = ys.reshape(sseq, bs, D_MODEL).transpose(1, 0, 2)

    def ssm_state(s, b):
        return s.reshape(DEPTH, b, SSM_GROUPS, SSM_STATE)

    return (y_prompt, y_sample,
            conv_p, ssm_state(sre_p, bp), ssm_state(sim_p, bp), ffn_p,
            conv_s, ssm_state(sre_s, bs), ssm_state(sim_s, bs), ffn_s)
```

```python
import functools
import math

import jax
import jax.numpy as jnp
from jax import lax
from jax.experimental import pallas as pl
from jax.experimental.pallas import tpu as pltpu

D_MODEL = 1024
DEPTH = 2
CONV_WIDTH = 3
CONV_A_WIDTH = 512
SSM_WIDTH = 512
SSM_GROUP = 16
SSM_GROUPS = 32
SSM_STATE = 64
SSM_FLAT = SSM_GROUPS * SSM_STATE
D_FF = 2816
PLE_DIM = 256
NORM_EPS = 1e-6
IN_PROJ_WIDTH = 4096

SSM_CHUNKS = 4
CHUNK_CH = SSM_WIDTH // SSM_CHUNKS
CHUNK_ST = SSM_FLAT // SSM_CHUNKS

FF_CHUNKS = ((0, 1024), (1024, 2048), (2048, 2816))
FF_CHUNK_MAX = 1024
FF_STREAM_CHUNK = 1408

MIXER_TIME_BLOCKS = (64, 128)
FFN_TIME_BLOCK = 128
MIXER_MATMUL_WEIGHTS = ("w_in", "w_out_a", "w_glu_a", "w_glu_b", "w_o")
FFN_MATMUL_WEIGHTS = ("w_up", "w_down", "w_ple_gate", "w_ple")
ROUND_CHUNKS = 8
VMEM_LIMIT_BYTES = 56 * 1024 * 1024

_BF16 = jnp.bfloat16
_F32 = jnp.float32


def _dot(a, b):
    return jnp.dot(a, b, preferred_element_type=_F32)


def _dot_nt(a, b):
    return lax.dot_general(a, b, (((1,), (1,)), ((), ())), preferred_element_type=_F32)


def _dot_nt_split(a, b):
    a_hi = a.astype(_BF16)
    b_hi = b.astype(_BF16)
    a_lo = (a - a_hi.astype(_F32)).astype(_BF16)
    b_lo = (b - b_hi.astype(_F32)).astype(_BF16)
    return _dot_nt(a_hi, b_hi) + _dot_nt(a_hi, b_lo) + _dot_nt(a_lo, b_hi)


def _rms(x, g):
    ms = jnp.mean(x * x, axis=-1, keepdims=True)
    return x * lax.rsqrt(ms + NORM_EPS) * g


def _gelu(x):
    c = math.sqrt(2.0 / math.pi)
    t = jnp.tanh(x * (c + (0.044715 * c) * (x * x)))
    return x * (0.5 + 0.5 * t)


def _sigmoid(x):
    return 0.5 * jnp.tanh(0.5 * x) + 0.5


def _seq_copies(hbm_ref, lead, buf, sem, step, slot, tb, to_hbm):
    copies = []
    for b in range(buf.shape[2]):
        hbm_view = hbm_ref.at[lead + (b, pl.ds(step * tb, tb))]
        vmem_view = buf.at[slot, :, b]
        src, dst = (vmem_view, hbm_view) if to_hbm else (hbm_view, vmem_view)
        copies.append(pltpu.make_async_copy(src, dst, sem.at[slot]))
    return copies


def _fetch_time_major(hbm_ref, lead, buf, sem, tb):
    j = pl.program_id(0)
    slot = j % 2

    @pl.when(j == 0)
    def _():
        for c in _seq_copies(hbm_ref, lead, buf, sem, 0, 0, tb, False):
            c.start()

    @pl.when(j + 1 < pl.num_programs(0))
    def _():
        for c in _seq_copies(hbm_ref, lead, buf, sem, j + 1, 1 - slot, tb, False):
            c.start()

    for c in _seq_copies(hbm_ref, lead, buf, sem, j, slot, tb, False):
        c.wait()


def _read_time_major(buf):
    tb, nb, width = buf.shape[1:]
    return buf[pl.program_id(0) % 2].reshape(tb * nb, width)


def _store_batch_major(val, hbm_ref, buf, sem, tb):
    j = pl.program_id(0)
    slot = j % 2
    buf[slot] = val.reshape(tb, buf.shape[2], buf.shape[3])
    for c in _seq_copies(hbm_ref, (), buf, sem, j, slot, tb, True):
        c.start()

    @pl.when(j >= 1)
    def _():
        for c in _seq_copies(hbm_ref, (), buf, sem, j - 1, 1 - slot, tb, True):
            c.wait()

    @pl.when(j == pl.num_programs(0) - 1)
    def _():
        for c in _seq_copies(hbm_ref, (), buf, sem, j, slot, tb, True):
            c.wait()


def _conv_state_copies(hbm_ref, slot, carry, sem, to_hbm):
    copies = []
    for k in range(CONV_WIDTH - 1):
        hbm_view = hbm_ref.at[slot, :, k, :]
        src, dst = (carry.at[k], hbm_view) if to_hbm else (hbm_view, carry.at[k])
        copies.append(pltpu.make_async_copy(src, dst, sem.at[slot, k]))
    return copies


def _at_step(first, single_step, body):
    if single_step:
        body()
    else:
        step = 0 if first else pl.num_programs(0) - 1
        pl.when(pl.program_id(0) == step)(body)


def _conv_state_fetch(state_hbm, layer, carry, sem, single_step, wait):
    def body():
        if state_hbm is None:
            if not wait:
                carry[...] = jnp.zeros(carry.shape, carry.dtype)
        else:
            for c in _conv_state_copies(state_hbm, layer, carry, sem, False):
                c.wait() if wait else c.start()

    _at_step(True, single_step, body)


def _conv_state_store(carry, out_hbm, layer, sem, single_step, wait):
    def body():
        for slot in range(layer, out_hbm.shape[0]):
            for c in _conv_state_copies(out_hbm, slot, carry, sem, True):
                c.wait() if wait else c.start()

    _at_step(False, single_step, body)


def _ssm_prep_kernel(logdt_ref, lre_ref, lim_ref, bre_ref, bim_ref, cre_ref, cim_ref,
                     abar_ref, b2blk_ref, c2_re_ref, c2_im_ref, fblk_ref):
    dt = jnp.exp(logdt_ref[...])
    lre = lre_ref[...]
    lim = lim_ref[...]
    mag = jnp.exp(lre * dt)
    ang = lim * dt
    ar = mag * jnp.cos(ang)
    ai = mag * jnp.sin(ang)
    abar_ref[0:1, :] = ar
    abar_ref[1:2, :] = ai
    abar_ref[2:3, :] = ar * ar - ai * ai
    abar_ref[3:4, :] = 2.0 * (ar * ai)
    den = lre * lre + lim * lim
    nr = ar - 1.0
    coef_re = (nr * lre + ai * lim) / den
    coef_im = (ai * lre - nr * lim) / den
    bre = bre_ref[...]
    bim = bim_ref[...]
    bbar_re = coef_re * bre - coef_im * bim
    bbar_im = coef_re * bim + coef_im * bre
    abb_re = ar * bbar_re - ai * bbar_im
    abb_im = ar * bbar_im + ai * bbar_re
    row = lax.broadcasted_iota(jnp.int32, (CHUNK_CH, CHUNK_ST), 0)
    lane = lax.broadcasted_iota(jnp.int32, (CHUNK_CH, CHUNK_ST), 1)
    same_group = (row // SSM_GROUP) == (lane // SSM_STATE)
    reps = CHUNK_CH // SSM_GROUP
    for c in range(SSM_CHUNKS):
        cols = slice(c * CHUNK_ST, (c + 1) * CHUNK_ST)
        rows = slice(c * CHUNK_CH, (c + 1) * CHUNK_CH)

        def b_block(v, cols=cols):
            return jnp.where(same_group, jnp.concatenate([v[:, cols]] * reps, axis=0), 0.0)

        b_re = b_block(bbar_re)
        b_im = b_block(bbar_im)
        b2blk_ref[c, 0:CHUNK_CH, 0:CHUNK_ST] = b_block(abb_re).astype(_BF16)
        b2blk_ref[c, 0:CHUNK_CH, CHUNK_ST:2 * CHUNK_ST] = b_block(abb_im).astype(_BF16)
        b2blk_ref[c, CHUNK_CH:2 * CHUNK_CH, 0:CHUNK_ST] = b_re.astype(_BF16)
        b2blk_ref[c, CHUNK_CH:2 * CHUNK_CH, CHUNK_ST:2 * CHUNK_ST] = b_im.astype(_BF16)
        c_re = jnp.where(same_group, cre_ref[rows, :], 0.0)
        c_im = jnp.where(same_group, cim_ref[rows, :], 0.0)
        arc = ar[:, cols]
        aic = ai[:, cols]
        c2_re_ref[c, 0:CHUNK_CH, :] = c_re.astype(_BF16)
        c2_re_ref[c, CHUNK_CH:2 * CHUNK_CH, :] = (c_re * arc - c_im * aic).astype(_BF16)
        c2_im_ref[c, 0:CHUNK_CH, :] = (-c_im).astype(_BF16)
        c2_im_ref[c, CHUNK_CH:2 * CHUNK_CH, :] = (-(c_re * aic + c_im * arc)).astype(_BF16)
        feed = _dot_nt_split(b_re, c_re) - _dot_nt_split(b_im, c_im)
        fblk_ref[c] = feed.astype(_BF16)


def _ssm_prep(log_dt, lam_re, lam_im, b_re, b_im, c_re, c_im):
    logdt = jnp.repeat(log_dt, SSM_STATE, axis=1).reshape(DEPTH, 1, SSM_FLAT)
    lre = lam_re.reshape(DEPTH, 1, SSM_FLAT)
    lim = lam_im.reshape(DEPTH, 1, SSM_FLAT)
    bre = b_re.transpose(0, 3, 1, 2).reshape(DEPTH, SSM_GROUP, SSM_FLAT)
    bim = b_im.transpose(0, 3, 1, 2).reshape(DEPTH, SSM_GROUP, SSM_FLAT)
    reps = CHUNK_ST // SSM_STATE
    cre = jnp.tile(c_re.reshape(DEPTH, SSM_WIDTH, SSM_STATE), (1, 1, reps))
    cim = jnp.tile(c_im.reshape(DEPTH, SSM_WIDTH, SSM_STATE), (1, 1, reps))

    def spec(*shape):
        return pl.BlockSpec((None,) + shape, lambda i: (i,) + (0,) * len(shape))

    return pl.pallas_call(
        _ssm_prep_kernel,
        grid=(DEPTH,),
        in_specs=[spec(1, SSM_FLAT), spec(1, SSM_FLAT), spec(1, SSM_FLAT),
                  spec(SSM_GROUP, SSM_FLAT), spec(SSM_GROUP, SSM_FLAT),
                  spec(SSM_WIDTH, CHUNK_ST), spec(SSM_WIDTH, CHUNK_ST)],
        out_specs=[spec(4, SSM_FLAT),
                   spec(SSM_CHUNKS, 2 * CHUNK_CH, 2 * CHUNK_ST),
                   spec(SSM_CHUNKS, 2 * CHUNK_CH, CHUNK_ST),
                   spec(SSM_CHUNKS, 2 * CHUNK_CH, CHUNK_ST),
                   spec(SSM_CHUNKS, CHUNK_CH, CHUNK_CH)],
        out_shape=[jax.ShapeDtypeStruct((DEPTH, 4, SSM_FLAT), _F32),
                   jax.ShapeDtypeStruct((DEPTH, SSM_CHUNKS, 2 * CHUNK_CH, 2 * CHUNK_ST), _BF16),
                   jax.ShapeDtypeStruct((DEPTH, SSM_CHUNKS, 2 * CHUNK_CH, CHUNK_ST), _BF16),
                   jax.ShapeDtypeStruct((DEPTH, SSM_CHUNKS, 2 * CHUNK_CH, CHUNK_ST), _BF16),
                   jax.ShapeDtypeStruct((DEPTH, SSM_CHUNKS, CHUNK_CH, CHUNK_CH), _BF16)],
        name="ssm_prep",
    )(logdt, lre, lim, bre, bim, cre, cim)


def _round_kernel(*refs):
    half = len(refs) // 2
    for src_ref, dst_ref in zip(refs[:half], refs[half:]):
        dst_ref[...] = src_ref[...].astype(_BF16)


def _round_layer0(ws):
    specs = [pl.BlockSpec((1, w.shape[1] // ROUND_CHUNKS, w.shape[2]), lambda j: (0, j, 0)) for w in ws]
    return pl.pallas_call(
        _round_kernel,
        grid=(ROUND_CHUNKS,),
        in_specs=specs,
        out_specs=specs,
        out_shape=[jax.ShapeDtypeStruct((1,) + w.shape[1:], _BF16) for w in ws],
        name="round_mixer_l0",
    )(*ws)


def _mixer_kernel(x_ref, *refs, rb, tb, layer, zero_conv, single_step, n_cast):
    refs = list(refs)
    conv0_hbm = None if zero_conv else refs.pop(0)
    sre0_ref, sim0_ref = refs.pop(0), refs.pop(0)
    prev = [refs.pop(0) for _ in range(3)] if layer > 0 else []
    cast_in = [refs.pop(0) for _ in range(n_cast)]
    (g_ref, w_in_ref, cw_ref, cb_ref, w_out_a_ref,
     abar_ref, b2blk_ref, c2_re_ref, c2_im_ref, fblk_ref, dskip_ref,
     glu_a_ref, glu_b_ref, w_o_ref,
     out_ref, conv_out_hbm, sre_stack_ref, sim_stack_ref, *refs) = refs
    cast_out = [refs.pop(0) for _ in range(n_cast)]
    vbuf, s_re, s_im, conv_carry, conv_sem, *x_dma = refs
    for src_ref, dst_ref in zip(cast_in, cast_out):
        dst_ref[...] = src_ref[...].astype(_BF16)
    rows = rb * tb
    carry_rows = (CONV_WIDTH - 1) * rb
    sre_out_ref = sre_stack_ref.at[layer]
    sim_out_ref = sim_stack_ref.at[layer]

    @pl.when(pl.program_id(0) == 0)
    def _():
        for stack_ref, prev_ref in zip((sre_stack_ref, sim_stack_ref), prev[1:]):
            stack_ref[0:layer] = prev_ref[...]
        sre_out_ref[...] = sre0_ref[...]
        sim_out_ref[...] = sim0_ref[...]

    conv_io = (layer, conv_carry, conv_sem, single_step)
    _conv_state_fetch(conv0_hbm, *conv_io, wait=False)
    if not single_step:
        _conv_state_fetch(conv0_hbm, *conv_io, wait=True)
    if x_dma:
        _fetch_time_major(x_ref, (), *x_dma, tb)
        x = _read_time_major(x_dma[0])
    else:
        x = x_ref[...]
    xn = _rms(x, g_ref[layer:layer + 1, :]).astype(_BF16)

    def proj(lo, hi):
        return _dot(xn, w_in_ref[:, lo:hi])

    u = proj(1536, 2048)
    pair = 2 if tb % 2 == 0 else 1
    steps = tb // pair
    srows = steps * rb
    if pair == 2:
        u4 = u.reshape(steps, 2, rb, SSM_WIDTH)
        u_parts = [u4[:, i].reshape(srows, SSM_WIDTH) for i in range(2)]
    else:
        u_parts = [u]
    ub_parts = [v.astype(_BF16) for v in u_parts]
    s_re[0:rb, :] = sre_out_ref[...]
    s_im[0:rb, :] = sim_out_ref[...]
    for c in range(SSM_CHUNKS):
        ch = slice(c * CHUNK_CH, (c + 1) * CHUNK_CH)
        if pair == 2:
            inc = _dot(jnp.concatenate([ub_parts[0][:, ch], ub_parts[1][:, ch]], axis=1), b2blk_ref[c])
        else:
            inc = _dot(ub_parts[0][:, ch], b2blk_ref[c, CHUNK_CH:2 * CHUNK_CH, :])
        s_re[rb:rb + srows, c * CHUNK_ST:(c + 1) * CHUNK_ST] = inc[:, 0:CHUNK_ST]
        s_im[rb:rb + srows, c * CHUNK_ST:(c + 1) * CHUNK_ST] = inc[:, CHUNK_ST:2 * CHUNK_ST]

    a_row = 2 * (pair - 1)
    for c in range(SSM_CHUNKS):
        cols = slice(c * CHUNK_ST, (c + 1) * CHUNK_ST)
        ar = jnp.broadcast_to(abar_ref[a_row:a_row + 1, cols], (rb, CHUNK_ST))
        ai = jnp.broadcast_to(abar_ref[a_row + 1:a_row + 2, cols], (rb, CHUNK_ST))

        def step(k, carry, cols=cols, ar=ar, ai=ai):
            sr, si = carry
            r0 = pl.multiple_of(rb + k * rb, rb)
            nsr = ar * sr - ai * si + s_re[pl.ds(r0, rb), cols]
            nsi = ar * si + ai * sr + s_im[pl.ds(r0, rb), cols]
            s_re[pl.ds(r0, rb), cols] = nsr
            s_im[pl.ds(r0, rb), cols] = nsi
            return nsr, nsi

        sr, si = lax.fori_loop(0, steps, step, (s_re[0:rb, cols], s_im[0:rb, cols]), unroll=True)
        sre_out_ref[:, cols] = sr
        sim_out_ref[:, cols] = si

    h = proj(0, 512)
    gate_c = proj(1024, 1536)
    if single_step:
        _conv_state_fetch(conv0_hbm, *conv_io, wait=True)
    vbuf[0:rb, :] = conv_carry[0]
    vbuf[rb:carry_rows, :] = conv_carry[1]
    vbuf[carry_rows:carry_rows + rows, :] = gate_c * h
    conv_y = (cb_ref[layer:layer + 1, :]
              + vbuf[0:rows, :] * cw_ref[0:1, :]
              + vbuf[rb:rb + rows, :] * cw_ref[1:2, :]
              + vbuf[2 * rb:2 * rb + rows, :] * cw_ref[2:3, :])
    conv_carry[0] = vbuf[rows:rows + rb, :]
    conv_carry[1] = vbuf[rows + rb:rows + carry_rows, :]
    if single_step:
        _conv_state_store(conv_carry, conv_out_hbm, layer, conv_sem, single_step, wait=False)
    gate_b = proj(512, 1024)
    z_a = _dot((gate_b * conv_y).astype(_BF16), w_out_a_ref[...])
    merged = _sigmoid(proj(2048, 3072)) * z_a

    y_parts = [[] for _ in range(pair)]
    for c in range(SSM_CHUNKS):
        ch = slice(c * CHUNK_CH, (c + 1) * CHUNK_CH)
        cols = slice(c * CHUNK_ST, (c + 1) * CHUNK_ST)
        if pair == 2:
            y2 = (_dot_nt(s_re[:, cols].astype(_BF16), c2_re_ref[c])
                  + _dot_nt(s_im[:, cols].astype(_BF16), c2_im_ref[c]))
            y_parts[0].append(y2[0:srows, CHUNK_CH:2 * CHUNK_CH] + _dot(ub_parts[0][:, ch], fblk_ref[c]))
            y_parts[1].append(y2[rb:rb + srows, 0:CHUNK_CH])
        else:
            y_parts[0].append(
                _dot_nt(s_re[rb:rb + srows, cols].astype(_BF16), c2_re_ref[c, 0:CHUNK_CH, :])
                + _dot_nt(s_im[rb:rb + srows, cols].astype(_BF16), c2_im_ref[c, 0:CHUNK_CH, :]))
    d_skip = dskip_ref[layer:layer + 1, :]
    ssm_parts = [jnp.concatenate(yp, axis=1) + d_skip * up for yp, up in zip(y_parts, u_parts)]
    if pair == 2:
        ssm_y = jnp.stack([v.reshape(steps, rb, SSM_WIDTH) for v in ssm_parts],
                          axis=1).reshape(rows, SSM_WIDTH)
    else:
        ssm_y = ssm_parts[0]
    sy = _gelu(ssm_y).astype(_BF16)
    z_b = _dot(sy, glu_a_ref[...]) * _sigmoid(_dot(sy, glu_b_ref[...]))
    merged = (merged + _sigmoid(proj(3072, 4096)) * z_b).astype(_BF16)
    out_ref[...] = x + _dot(merged, w_o_ref[...])
    if not single_step:
        _conv_state_store(conv_carry, conv_out_hbm, layer, conv_sem, single_step, wait=False)
    _conv_state_store(conv_carry, conv_out_hbm, layer, conv_sem, single_step, wait=True)


def _ffn_kernel(x_ref, p_ref, *refs, rb, tb, layer, zero_conv, single_step, final_norm, p_native,
                out_native):
    refs = list(refs)
    f0_hbm = None if zero_conv else refs.pop(0)
    if layer > 0:
        refs.pop(0)
    (g2_ref, w_up_ref, fcw_ref, fcb_ref, w_down_ref,
     g3_ref, w_pg_ref, w_ple_ref, gf_ref,
     out_ref, fconv_out_hbm,
     fbuf, fcarry, fsem, *dma) = refs
    rows = rb * tb
    carry_rows = (CONV_WIDTH - 1) * rb
    p_dma = [dma.pop(0), dma.pop(0)] if p_native else None
    out_dma = [dma.pop(0), dma.pop(0)] if out_native else None

    conv_io = (layer, fcarry, fsem, single_step)
    _conv_state_fetch(f0_hbm, *conv_io, wait=False)
    if not single_step:
        _conv_state_fetch(f0_hbm, *conv_io, wait=True)
    if p_native:
        _fetch_time_major(p_ref, (layer,), *p_dma, tb)

    x = x_ref[...]
    xn = _rms(x, g2_ref[layer:layer + 1, :]).astype(_BF16)
    acc = x
    if single_step:
        _conv_state_fetch(f0_hbm, *conv_io, wait=True)
    pending = None
    for lo, hi in FF_CHUNKS:
        wc = hi - lo
        up_a = _dot(xn, w_up_ref[:, lo:hi])
        up_b = _dot(xn, w_up_ref[:, D_FF + lo:D_FF + hi])
        if pending is not None:
            acc = acc + _dot(pending[0], w_down_ref[pending[1]:pending[2], :])
        fbuf[0:rb, 0:wc] = fcarry[0, :, lo:hi]
        fbuf[rb:carry_rows, 0:wc] = fcarry[1, :, lo:hi]
        fbuf[carry_rows:carry_rows + rows, 0:wc] = up_a
        conv_a = (fcb_ref[layer:layer + 1, lo:hi]
                  + fbuf[0:rows, 0:wc] * fcw_ref[0:1, lo:hi]
                  + fbuf[rb:rb + rows, 0:wc] * fcw_ref[1:2, lo:hi]
                  + fbuf[2 * rb:2 * rb + rows, 0:wc] * fcw_ref[2:3, lo:hi])
        fcarry[0, :, lo:hi] = fbuf[rows:rows + rb, 0:wc]
        fcarry[1, :, lo:hi] = fbuf[rows + rb:rows + carry_rows, 0:wc]
        pending = ((_gelu(conv_a) * up_b).astype(_BF16), lo, hi)
    acc = acc + _dot(pending[0], w_down_ref[pending[1]:pending[2], :])
    if single_step:
        _conv_state_store(fcarry, fconv_out_hbm, layer, fsem, single_step, wait=False)
    x2 = acc
    xn3 = _rms(x2, g3_ref[layer:layer + 1, :]).astype(_BF16)
    gate = _sigmoid(_dot(xn3, w_pg_ref[...]))
    p = _read_time_major(p_dma[0]) if p_native else p_ref[...]
    pe = _dot(p.astype(_BF16), w_ple_ref[...])
    x3 = x2 + gate * pe
    if final_norm:
        x3 = _rms(x3, gf_ref[...])
    if out_native:
        _store_batch_major(x3, out_ref, *out_dma, tb)
    else:
        out_ref[...] = x3
    if not single_step:
        _conv_state_store(fcarry, fconv_out_hbm, layer, fsem, single_step, wait=False)
    _conv_state_store(fcarry, fconv_out_hbm, layer, fsem, single_step, wait=True)


def _ffn_stream_kernel(x_ref, p_ref, f0_hbm, *refs, rb, layer, final_norm):
    refs = list(refs)
    if layer > 0:
        refs.pop(0)
    (g2_ref, wa_ref, wb_ref, fcw_ref, fcb_ref, wd_ref, g3_ref, w_pg_ref, w_ple_ref, gf_ref,
     out_ref, fconv_out_hbm, xn_s, acc_s, fbuf, fcarry, fsem) = refs
    c = pl.program_id(0)
    conv_io = (layer, fcarry, fsem, False)
    _conv_state_fetch(f0_hbm, *conv_io, wait=False)
    _conv_state_fetch(f0_hbm, *conv_io, wait=True)

    @pl.when(c == 0)
    def _():
        x = x_ref[...]
        xn_s[...] = _rms(x, g2_ref[layer:layer + 1, :]).astype(_BF16)
        acc_s[...] = x

    cols = pl.ds(pl.multiple_of(c * FF_STREAM_CHUNK, FF_STREAM_CHUNK), FF_STREAM_CHUNK)
    xn = xn_s[...]
    up_a = _dot(xn, wa_ref[...])
    up_b = _dot(xn, wb_ref[...])
    fbuf[0:rb, :] = fcarry[0, :, cols]
    fbuf[rb:2 * rb, :] = fcarry[1, :, cols]
    fbuf[2 * rb:3 * rb, :] = up_a
    conv_a = (fcb_ref[layer:layer + 1, :]
              + fbuf[0:rb, :] * fcw_ref[0:1, :]
              + fbuf[rb:2 * rb, :] * fcw_ref[1:2, :]
              + fbuf[2 * rb:3 * rb, :] * fcw_ref[2:3, :])
    fcarry[0, :, cols] = fbuf[rb:2 * rb, :]
    fcarry[1, :, cols] = fbuf[2 * rb:3 * rb, :]
    hid = (_gelu(conv_a) * up_b).astype(_BF16)
    acc_s[...] += _dot(hid, wd_ref[...])

    @pl.when(c == pl.num_programs(0) - 1)
    def _():
        x2 = acc_s[...]
        xn3 = _rms(x2, g3_ref[layer:layer + 1, :]).astype(_BF16)
        gate = _sigmoid(_dot(xn3, w_pg_ref[...]))
        pe = _dot(p_ref[...].astype(_BF16), w_ple_ref[...])
        x3 = x2 + gate * pe
        if final_norm:
            x3 = _rms(x3, gf_ref[...])
        out_ref[...] = x3

    _conv_state_store(fcarry, fconv_out_hbm, layer, fsem, False, wait=False)
    _conv_state_store(fcarry, fconv_out_hbm, layer, fsem, False, wait=True)


def _const_spec(shape, layer=None):
    if layer is None:
        return pl.BlockSpec(shape, lambda j: (0,) * len(shape), pipeline_mode=pl.Buffered(1))
    return pl.BlockSpec((None,) + shape, lambda j: (layer,) + (0,) * len(shape),
                        pipeline_mode=pl.Buffered(1))


def _layer_weight(wts, name, layer):
    w = wts[name]
    return w[layer] if isinstance(w, list) else (w, layer)


def _native_scratch(tb, rb, width):
    return [pltpu.VMEM((2, tb, rb, width), _F32), pltpu.SemaphoreType.DMA((2,))]


def _stack_spec(n_layers, rb, width):
    return pl.BlockSpec((n_layers, rb, width), lambda j: (0, 0, 0))


def _conv_state_scratch(rb, width):
    return [pltpu.VMEM((CONV_WIDTH - 1, rb, width), _F32), pltpu.SemaphoreType.DMA((DEPTH, CONV_WIDTH - 1))]


def _mixer_call(x, conv0, ssm0, prev, wts, layer, ssm_layer, rb, tb, x_native, cast=()):
    n = x.shape[0] * x.shape[1] if x_native else x.shape[0]
    rows = rb * tb
    carry_rows = (CONV_WIDTH - 1) * rb
    hbm_spec = pl.BlockSpec(memory_space=pl.ANY)
    row_spec = pl.BlockSpec((rows, D_MODEL), lambda j: (j, 0))
    operands = [x]
    in_specs = [hbm_spec if x_native else row_spec]
    if conv0 is not None:
        operands.append(conv0)
        in_specs.append(hbm_spec)
    operands += list(ssm0)
    in_specs += [_const_spec((rb, SSM_FLAT), ssm_layer)] * 2
    aliases = {}
    if prev:
        aliases[len(operands)] = 1
        operands += list(prev)
        in_specs += [hbm_spec] + [_const_spec((layer, rb, SSM_FLAT))] * 2
    steps = n // rows
    cast_out_specs, cast_shapes = [], []
    for w, first, count in cast:
        per = steps // count
        assert steps % count == 0 and w.shape[1] % (16 * per) == 0, (steps, count, w.shape)
        block = (None, w.shape[1] // per, w.shape[2])
        operands.append(w)
        in_specs.append(pl.BlockSpec(block, lambda j, per=per, first=first: (first + j // per, j % per, 0)))
        cast_out_specs.append(pl.BlockSpec(block, lambda j, per=per: (j // per, j % per, 0)))
        cast_shapes.append(jax.ShapeDtypeStruct((count,) + w.shape[1:], _BF16))
    names = ("w_in", "w_out_a", "w_glu_a", "w_glu_b", "w_o")
    (w_in, i_in), (w_out_a, i_out_a), (w_glu_a, i_glu_a), (w_glu_b, i_glu_b), (w_o, i_o) = [
        _layer_weight(wts, k, layer) for k in names]
    in_specs += [
        _const_spec((DEPTH, D_MODEL)),
        _const_spec((D_MODEL, IN_PROJ_WIDTH), i_in),
        _const_spec((CONV_WIDTH, CONV_A_WIDTH), layer),
        _const_spec((DEPTH, CONV_A_WIDTH)),
        _const_spec((CONV_A_WIDTH, D_MODEL), i_out_a),
        _const_spec((4, SSM_FLAT), layer),
        _const_spec((SSM_CHUNKS, 2 * CHUNK_CH, 2 * CHUNK_ST), layer),
        _const_spec((SSM_CHUNKS, 2 * CHUNK_CH, CHUNK_ST), layer),
        _const_spec((SSM_CHUNKS, 2 * CHUNK_CH, CHUNK_ST), layer),
        _const_spec((SSM_CHUNKS, CHUNK_CH, CHUNK_CH), layer),
        _const_spec((DEPTH, SSM_WIDTH)),
        _const_spec((SSM_WIDTH, D_MODEL), i_glu_a),
        _const_spec((SSM_WIDTH, D_MODEL), i_glu_b),
        _const_spec((D_MODEL, D_MODEL), i_o),
    ]
    out_specs = [row_spec, hbm_spec] + [_stack_spec(layer + 1, rb, SSM_FLAT)] * 2 + cast_out_specs
    out_shape = ([jax.ShapeDtypeStruct((n, D_MODEL), _F32),
                  jax.ShapeDtypeStruct((DEPTH, rb, CONV_WIDTH - 1, CONV_A_WIDTH), _F32)]
                 + [jax.ShapeDtypeStruct((layer + 1, rb, SSM_FLAT), _F32)] * 2 + cast_shapes)
    s_rows = rows // 2 if tb % 2 == 0 else rows
    scratch = [
        pltpu.VMEM((carry_rows + rows, CONV_A_WIDTH), _F32),
        pltpu.VMEM((rb + s_rows, SSM_FLAT), _F32),
        pltpu.VMEM((rb + s_rows, SSM_FLAT), _F32),
    ]
    scratch += _conv_state_scratch(rb, CONV_A_WIDTH)
    if x_native:
        scratch += _native_scratch(tb, rb, D_MODEL)
    return pl.pallas_call(
        functools.partial(_mixer_kernel, rb=rb, tb=tb, layer=layer, zero_conv=conv0 is None,
                          single_step=n == rows, n_cast=len(cast)),
        grid=(n // rows,),
        in_specs=in_specs,
        out_specs=out_specs,
        out_shape=out_shape,
        scratch_shapes=scratch,
        input_output_aliases=aliases,
        compiler_params=pltpu.CompilerParams(
            dimension_semantics=("arbitrary",), vmem_limit_bytes=VMEM_LIMIT_BYTES),
        name=f"mixer_l{layer}_rb{rb}",
    )(*operands,
      wts["norm_mix"], w_in, wts["conv_a_w"], wts["conv_a_b"], w_out_a,
      wts["abar"], wts["b2blk"], wts["c2_re"], wts["c2_im"], wts["fblk"], wts["d_skip"],
      w_glu_a, w_glu_b, w_o)


def _ffn_call(x, p, f0, prev, wts, layer, rb, tb, final_norm, p_native, out_native):
    n = x.shape[0]
    rows = rb * tb
    carry_rows = (CONV_WIDTH - 1) * rb
    hbm_spec = pl.BlockSpec(memory_space=pl.ANY)
    row_spec = pl.BlockSpec((rows, D_MODEL), lambda j: (j, 0))
    operands = [x, p]
    in_specs = [
        row_spec,
        hbm_spec if p_native else pl.BlockSpec((None, rows, PLE_DIM), lambda j: (layer, j, 0)),
    ]
    if f0 is not None:
        operands.append(f0)
        in_specs.append(hbm_spec)
    aliases = {}
    if prev is not None:
        aliases[len(operands)] = 1
        operands.append(prev)
        in_specs.append(hbm_spec)
    in_specs += [
        _const_spec((DEPTH, D_MODEL)),
        _const_spec((D_MODEL, 2 * D_FF), layer),
        _const_spec((CONV_WIDTH, D_FF), layer),
        _const_spec((DEPTH, D_FF)),
        _const_spec((D_FF, D_MODEL), layer),
        _const_spec((DEPTH, D_MODEL)),
        _const_spec((D_MODEL, D_MODEL), layer),
        _const_spec((PLE_DIM, D_MODEL), layer),
        _const_spec((1, D_MODEL)),
    ]
    out_specs = [hbm_spec if out_native else row_spec, hbm_spec]
    out_shape = [jax.ShapeDtypeStruct((rb, n // rb, D_MODEL) if out_native else (n, D_MODEL), _F32),
                 jax.ShapeDtypeStruct((DEPTH, rb, CONV_WIDTH - 1, D_FF), _F32)]
    scratch = [pltpu.VMEM((carry_rows + rows, FF_CHUNK_MAX), _F32)]
    scratch += _conv_state_scratch(rb, D_FF)
    if p_native:
        scratch += _native_scratch(tb, rb, PLE_DIM)
    if out_native:
        scratch += _native_scratch(tb, rb, D_MODEL)
    return pl.pallas_call(
        functools.partial(_ffn_kernel, rb=rb, tb=tb, layer=layer, zero_conv=f0 is None,
                          single_step=n == rows,
                          final_norm=final_norm, p_native=p_native, out_native=out_native),
        grid=(n // rows,),
        in_specs=in_specs,
        out_specs=out_specs,
        out_shape=out_shape,
        scratch_shapes=scratch,
        input_output_aliases=aliases,
        compiler_params=pltpu.CompilerParams(
            dimension_semantics=("arbitrary",), vmem_limit_bytes=VMEM_LIMIT_BYTES),
        name=f"ffn_l{layer}_rb{rb}",
    )(*operands,
      wts["norm_ffn"], wts["w_up"], wts["ffn_conv_w"], wts["ffn_conv_b"], wts["w_down"],
      wts["norm_ple"], wts["w_ple_gate"], wts["w_ple"], wts["norm_final"])


def _ffn_stream_call(x, p, f0, prev, wts, layer, final_norm):
    rb = x.shape[0]
    chunks = D_FF // FF_STREAM_CHUNK
    hbm_spec = pl.BlockSpec(memory_space=pl.ANY)
    operands = [x, p, f0]
    in_specs = [_const_spec((rb, D_MODEL)), _const_spec((rb, PLE_DIM), layer), hbm_spec]
    aliases = {}
    if prev is not None:
        aliases[len(operands)] = 1
        operands.append(prev)
        in_specs.append(hbm_spec)
    in_specs += [
        _const_spec((DEPTH, D_MODEL)),
        pl.BlockSpec((None, D_MODEL, FF_STREAM_CHUNK), lambda c: (layer, 0, c)),
        pl.BlockSpec((None, D_MODEL, FF_STREAM_CHUNK), lambda c: (layer, 0, c + chunks)),
        pl.BlockSpec((None, CONV_WIDTH, FF_STREAM_CHUNK), lambda c: (layer, 0, c)),
        pl.BlockSpec((DEPTH, FF_STREAM_CHUNK), lambda c: (0, c)),
        pl.BlockSpec((None, FF_STREAM_CHUNK, D_MODEL), lambda c: (layer, c, 0)),
        _const_spec((DEPTH, D_MODEL)),
        _const_spec((D_MODEL, D_MODEL), layer),
        _const_spec((PLE_DIM, D_MODEL), layer),
        _const_spec((1, D_MODEL)),
    ]
    return pl.pallas_call(
        functools.partial(_ffn_stream_kernel, rb=rb, layer=layer, final_norm=final_norm),
        grid=(chunks,),
        in_specs=in_specs,
        out_specs=[pl.BlockSpec((rb, D_MODEL), lambda c: (0, 0)), hbm_spec],
        out_shape=[jax.ShapeDtypeStruct((rb, D_MODEL), _F32),
                   jax.ShapeDtypeStruct((DEPTH, rb, CONV_WIDTH - 1, D_FF), _F32)],
        scratch_shapes=[pltpu.VMEM((rb, D_MODEL), _BF16), pltpu.VMEM((rb, D_MODEL), _F32),
                        pltpu.VMEM((CONV_WIDTH * rb, FF_STREAM_CHUNK), _F32)]
        + _conv_state_scratch(rb, D_FF),
        input_output_aliases=aliases,
        compiler_params=pltpu.CompilerParams(
            dimension_semantics=("arbitrary",), vmem_limit_bytes=VMEM_LIMIT_BYTES),
        name=f"ffn_stream_l{layer}_rb{rb}",
    )(*operands,
      wts["norm_ffn"], wts["w_up"], wts["w_up"], wts["ffn_conv_w"], wts["ffn_conv_b"], wts["w_down"],
      wts["norm_ple"], wts["w_ple_gate"], wts["w_ple"], wts["norm_final"])


def _run_trunk(x, p, conv0, sre0, sim0, f0, wts, rb, tb_mixer, tb_ffn, native, cast_weights=False):
    mixer_states, ffn_state = [], None
    for layer in range(DEPTH):
        last = layer == DEPTH - 1
        ssm_layer = 0 if conv0 is None else layer
        cast = []
        if cast_weights and layer == 0:
            cast = ([(wts[k], 0, DEPTH) for k in FFN_MATMUL_WEIGHTS]
                    + [(wts[k][1], 1, DEPTH - 1) for k in MIXER_MATMUL_WEIGHTS])
        x, *mixer_states = _mixer_call(x, conv0, (sre0, sim0), mixer_states, wts, layer, ssm_layer,
                                       rb, tb_mixer[layer], x_native=native and layer == 0, cast=cast)
        if cast:
            rounded = mixer_states[3:]
            mixer_states = mixer_states[:3]
            wts.update(zip(FFN_MATMUL_WEIGHTS, rounded))
            for k, w in zip(MIXER_MATMUL_WEIGHTS, rounded[len(FFN_MATMUL_WEIGHTS):]):
                wts[k] = [wts[k][0]] + [(w, i) for i in range(DEPTH - 1)]
        if not native and tb_ffn == 1 and f0 is not None:
            x, ffn_state = _ffn_stream_call(x, p, f0, ffn_state, wts, layer, final_norm=last)
        else:
            x, ffn_state = _ffn_call(x, p, f0, ffn_state, wts, layer, rb, tb_ffn,
                                     final_norm=last, p_native=native, out_native=native and last)
    return (x, *mixer_states, ffn_state)


def kernel(x_prompt, x_sample, p_prompt, p_sample, state_conv_a, state_ssm_re, state_ssm_im, state_ffn_conv, norm_mix, w_in, conv_a_w, conv_a_b, w_out_a, log_dt, lam_re, lam_im, b_re, b_im, c_re, c_im, d_skip, w_glu_a, w_glu_b, w_o, norm_ffn, w_up, ffn_conv_w, ffn_conv_b, w_down, norm_ple, w_ple_gate, w_ple, norm_final):
    abar, b2blk, c2_re, c2_im, fblk = _ssm_prep(log_dt, lam_re, lam_im, b_re, b_im, c_re, c_im)
    mixer_f32 = (w_in, w_out_a, w_glu_a, w_glu_b, w_o)
    w_in_0, w_out_a_0, w_glu_a_0, w_glu_b_0, w_o_0 = _round_layer0(mixer_f32)
    wts = {
        "norm_mix": norm_mix,
        "w_in": [(w_in_0, 0), w_in],
        "conv_a_w": conv_a_w,
        "conv_a_b": conv_a_b,
        "w_out_a": [(w_out_a_0, 0), w_out_a],
        "abar": abar, "b2blk": b2blk, "c2_re": c2_re, "c2_im": c2_im, "fblk": fblk,
        "d_skip": d_skip,
        "w_glu_a": [(w_glu_a_0, 0), w_glu_a],
        "w_glu_b": [(w_glu_b_0, 0), w_glu_b],
        "w_o": [(w_o_0, 0), w_o],
        "norm_ffn": norm_ffn,
        "w_up": w_up,
        "ffn_conv_w": ffn_conv_w,
        "ffn_conv_b": ffn_conv_b,
        "w_down": w_down,
        "norm_ple": norm_ple,
        "w_ple_gate": w_ple_gate,
        "w_ple": w_ple,
        "norm_final": norm_final.reshape(1, D_MODEL),
    }

    bp = x_prompt.shape[0]
    zs = jnp.zeros((1, bp, SSM_FLAT), _F32)
    y_prompt, conv_p, sre_p, sim_p, ffn_p = _run_trunk(
        x_prompt, p_prompt, None, zs, zs, None, wts, bp, MIXER_TIME_BLOCKS, FFN_TIME_BLOCK,
        native=True, cast_weights=True)

    bs, sseq, _ = x_sample.shape
    xs = x_sample.transpose(1, 0, 2).reshape(sseq * bs, D_MODEL)
    ps = p_sample.transpose(0, 2, 1, 3).reshape(DEPTH, sseq * bs, PLE_DIM)
    ys, conv_s, sre_s, sim_s, ffn_s = _run_trunk(
        xs, ps, state_conv_a,
        state_ssm_re.reshape(DEPTH, bs, SSM_FLAT), state_ssm_im.reshape(DEPTH, bs, SSM_FLAT),
        state_ffn_conv, wts, bs, (sseq,) * DEPTH, sseq, native=False)
    y_sample = ys.reshape(sseq, bs, D_MODEL).transpose(1, 0, 2)

    def ssm_state(s, b):
        return s.reshape(DEPTH, b, SSM_GROUPS, SSM_STATE)

    return (y_prompt, y_sample,
            conv_p, ssm_state(sre_p, bp), ssm_state(sim_p, bp), ffn_p,
            conv_s, ssm_state(sre_s, bs), ssm_state(sim_s, bs), ffn_s)
```

```python
import functools
import math

import jax
import jax.numpy as jnp
from jax import lax
from jax.experimental import pallas as pl
from jax.experimental.pallas import tpu as pltpu

D_MODEL = 1024
DEPTH = 2
CONV_WIDTH = 3
CONV_A_WIDTH = 512
SSM_WIDTH = 512
SSM_GROUP = 16
SSM_GROUPS = 32
SSM_STATE = 64
SSM_FLAT = SSM_GROUPS * SSM_STATE
D_FF = 2816
PLE_DIM = 256
NORM_EPS = 1e-6
IN_PROJ_WIDTH = 4096

SSM_CHUNKS = 4
CHUNK_CH = SSM_WIDTH // SSM_CHUNKS
CHUNK_ST = SSM_FLAT // SSM_CHUNKS

FF_CHUNKS = ((0, 1024), (1024, 2048), (2048, 2816))
FF_CHUNK_MAX = 1024

MIXER_TIME_BLOCKS = (64, 128)
FFN_TIME_BLOCK = 128
MIXER_MATMUL_WEIGHTS = ("w_in", "w_out_a", "w_glu_a", "w_glu_b", "w_o")
FFN_MATMUL_WEIGHTS = ("w_up", "w_down", "w_ple_gate", "w_ple")
ROUND_CHUNKS = 8
VMEM_LIMIT_BYTES = 56 * 1024 * 1024

_BF16 = jnp.bfloat16
_F32 = jnp.float32


def _dot(a, b):
    return jnp.dot(a, b, preferred_element_type=_F32)


def _dot_nt(a, b):
    return lax.dot_general(a, b, (((1,), (1,)), ((), ())), preferred_element_type=_F32)


def _dot_nt_split(a, b):
    a_hi = a.astype(_BF16)
    b_hi = b.astype(_BF16)
    a_lo = (a - a_hi.astype(_F32)).astype(_BF16)
    b_lo = (b - b_hi.astype(_F32)).astype(_BF16)
    return _dot_nt(a_hi, b_hi) + _dot_nt(a_hi, b_lo) + _dot_nt(a_lo, b_hi)


def _rms(x, g):
    ms = jnp.mean(x * x, axis=-1, keepdims=True)
    return x * lax.rsqrt(ms + NORM_EPS) * g


def _gelu(x):
    c = math.sqrt(2.0 / math.pi)
    t = jnp.tanh(x * (c + (0.044715 * c) * (x * x)))
    return x * (0.5 + 0.5 * t)


def _sigmoid(x):
    return 0.5 * jnp.tanh(0.5 * x) + 0.5


def _seq_copies(hbm_ref, lead, buf, sem, step, slot, tb, to_hbm):
    copies = []
    for b in range(buf.shape[2]):
        hbm_view = hbm_ref.at[lead + (b, pl.ds(step * tb, tb))]
        vmem_view = buf.at[slot, :, b]
        src, dst = (vmem_view, hbm_view) if to_hbm else (hbm_view, vmem_view)
        copies.append(pltpu.make_async_copy(src, dst, sem.at[slot]))
    return copies


def _start_split(copies):
    for i, c in enumerate(copies):
        c.start(priority=i % 2)


def _fetch_time_major(hbm_ref, lead, buf, sem, tb):
    j = pl.program_id(0)
    slot = j % 2

    @pl.when(j == 0)
    def _():
        _start_split(_seq_copies(hbm_ref, lead, buf, sem, 0, 0, tb, False))

    @pl.when(j + 1 < pl.num_programs(0))
    def _():
        _start_split(_seq_copies(hbm_ref, lead, buf, sem, j + 1, 1 - slot, tb, False))

    for c in _seq_copies(hbm_ref, lead, buf, sem, j, slot, tb, False):
        c.wait()


def _read_time_major(buf):
    tb, nb, width = buf.shape[1:]
    return buf[pl.program_id(0) % 2].reshape(tb * nb, width)


def _store_batch_major(val, hbm_ref, buf, sem, tb):
    j = pl.program_id(0)
    slot = j % 2
    buf[slot] = val.reshape(tb, buf.shape[2], buf.shape[3])
    _start_split(_seq_copies(hbm_ref, (), buf, sem, j, slot, tb, True))

    @pl.when(j >= 1)
    def _():
        for c in _seq_copies(hbm_ref, (), buf, sem, j - 1, 1 - slot, tb, True):
            c.wait()

    @pl.when(j == pl.num_programs(0) - 1)
    def _():
        for c in _seq_copies(hbm_ref, (), buf, sem, j, slot, tb, True):
            c.wait()


def _conv_state_copies(hbm_ref, slot, carry, sem, to_hbm):
    copies = []
    for k in range(CONV_WIDTH - 1):
        hbm_view = hbm_ref.at[slot, :, k, :]
        src, dst = (carry.at[k], hbm_view) if to_hbm else (hbm_view, carry.at[k])
        copies.append(pltpu.make_async_copy(src, dst, sem.at[slot, k]))
    return copies


def _at_step(first, single_step, body):
    if single_step:
        body()
    else:
        step = 0 if first else pl.num_programs(0) - 1
        pl.when(pl.program_id(0) == step)(body)


def _conv_state_fetch(state_hbm, layer, carry, sem, single_step, wait):
    def body():
        if state_hbm is None:
            if not wait:
                carry[...] = jnp.zeros(carry.shape, carry.dtype)
        else:
            for k, c in enumerate(_conv_state_copies(state_hbm, layer, carry, sem, False)):
                c.wait() if wait else c.start(priority=k % 2)

    _at_step(True, single_step, body)


def _conv_state_store(carry, out_hbm, layer, sem, single_step, wait):
    def body():
        for slot in range(layer, out_hbm.shape[0]):
            for k, c in enumerate(_conv_state_copies(out_hbm, slot, carry, sem, True)):
                c.wait() if wait else c.start(priority=k % 2)

    _at_step(False, single_step, body)


def _ssm_prep_kernel(logdt_ref, lre_ref, lim_ref, bre_ref, bim_ref, cre_ref, cim_ref,
                     abar_ref, b2blk_ref, c2_re_ref, c2_im_ref, fblk_ref):
    dt = jnp.exp(logdt_ref[...])
    lre = lre_ref[...]
    lim = lim_ref[...]
    mag = jnp.exp(lre * dt)
    ang = lim * dt
    ar = mag * jnp.cos(ang)
    ai = mag * jnp.sin(ang)
    abar_ref[0:1, :] = ar
    abar_ref[1:2, :] = ai
    abar_ref[2:3, :] = ar * ar - ai * ai
    abar_ref[3:4, :] = 2.0 * (ar * ai)
    den = lre * lre + lim * lim
    nr = ar - 1.0
    coef_re = (nr * lre + ai * lim) / den
    coef_im = (ai * lre - nr * lim) / den
    bre = bre_ref[...]
    bim = bim_ref[...]
    bbar_re = coef_re * bre - coef_im * bim
    bbar_im = coef_re * bim + coef_im * bre
    abb_re = ar * bbar_re - ai * bbar_im
    abb_im = ar * bbar_im + ai * bbar_re
    row = lax.broadcasted_iota(jnp.int32, (CHUNK_CH, CHUNK_ST), 0)
    lane = lax.broadcasted_iota(jnp.int32, (CHUNK_CH, CHUNK_ST), 1)
    same_group = (row // SSM_GROUP) == (lane // SSM_STATE)
    reps = CHUNK_CH // SSM_GROUP
    for c in range(SSM_CHUNKS):
        cols = slice(c * CHUNK_ST, (c + 1) * CHUNK_ST)
        rows = slice(c * CHUNK_CH, (c + 1) * CHUNK_CH)

        def b_block(v, cols=cols):
            return jnp.where(same_group, jnp.concatenate([v[:, cols]] * reps, axis=0), 0.0)

        b_re = b_block(bbar_re)
        b_im = b_block(bbar_im)
        b2blk_ref[c, 0:CHUNK_CH, 0:CHUNK_ST] = b_block(abb_re).astype(_BF16)
        b2blk_ref[c, 0:CHUNK_CH, CHUNK_ST:2 * CHUNK_ST] = b_block(abb_im).astype(_BF16)
        b2blk_ref[c, CHUNK_CH:2 * CHUNK_CH, 0:CHUNK_ST] = b_re.astype(_BF16)
        b2blk_ref[c, CHUNK_CH:2 * CHUNK_CH, CHUNK_ST:2 * CHUNK_ST] = b_im.astype(_BF16)
        c_re = jnp.where(same_group, cre_ref[rows, :], 0.0)
        c_im = jnp.where(same_group, cim_ref[rows, :], 0.0)
        arc = ar[:, cols]
        aic = ai[:, cols]
        c2_re_ref[c, 0:CHUNK_CH, :] = c_re.astype(_BF16)
        c2_re_ref[c, CHUNK_CH:2 * CHUNK_CH, :] = (c_re * arc - c_im * aic).astype(_BF16)
        c2_im_ref[c, 0:CHUNK_CH, :] = (-c_im).astype(_BF16)
        c2_im_ref[c, CHUNK_CH:2 * CHUNK_CH, :] = (-(c_re * aic + c_im * arc)).astype(_BF16)
        feed = _dot_nt_split(b_re, c_re) - _dot_nt_split(b_im, c_im)
        fblk_ref[c] = feed.astype(_BF16)


def _ssm_prep(log_dt, lam_re, lam_im, b_re, b_im, c_re, c_im):
    logdt = jnp.repeat(log_dt, SSM_STATE, axis=1).reshape(DEPTH, 1, SSM_FLAT)
    lre = lam_re.reshape(DEPTH, 1, SSM_FLAT)
    lim = lam_im.reshape(DEPTH, 1, SSM_FLAT)
    bre = b_re.transpose(0, 3, 1, 2).reshape(DEPTH, SSM_GROUP, SSM_FLAT)
    bim = b_im.transpose(0, 3, 1, 2).reshape(DEPTH, SSM_GROUP, SSM_FLAT)
    reps = CHUNK_ST // SSM_STATE
    cre = jnp.tile(c_re.reshape(DEPTH, SSM_WIDTH, SSM_STATE), (1, 1, reps))
    cim = jnp.tile(c_im.reshape(DEPTH, SSM_WIDTH, SSM_STATE), (1, 1, reps))

    def spec(*shape):
        return pl.BlockSpec((None,) + shape, lambda i: (i,) + (0,) * len(shape))

    return pl.pallas_call(
        _ssm_prep_kernel,
        grid=(DEPTH,),
        in_specs=[spec(1, SSM_FLAT), spec(1, SSM_FLAT), spec(1, SSM_FLAT),
                  spec(SSM_GROUP, SSM_FLAT), spec(SSM_GROUP, SSM_FLAT),
                  spec(SSM_WIDTH, CHUNK_ST), spec(SSM_WIDTH, CHUNK_ST)],
        out_specs=[spec(4, SSM_FLAT),
                   spec(SSM_CHUNKS, 2 * CHUNK_CH, 2 * CHUNK_ST),
                   spec(SSM_CHUNKS, 2 * CHUNK_CH, CHUNK_ST),
                   spec(SSM_CHUNKS, 2 * CHUNK_CH, CHUNK_ST),
                   spec(SSM_CHUNKS, CHUNK_CH, CHUNK_CH)],
        out_shape=[jax.ShapeDtypeStruct((DEPTH, 4, SSM_FLAT), _F32),
                   jax.ShapeDtypeStruct((DEPTH, SSM_CHUNKS, 2 * CHUNK_CH, 2 * CHUNK_ST), _BF16),
                   jax.ShapeDtypeStruct((DEPTH, SSM_CHUNKS, 2 * CHUNK_CH, CHUNK_ST), _BF16),
                   jax.ShapeDtypeStruct((DEPTH, SSM_CHUNKS, 2 * CHUNK_CH, CHUNK_ST), _BF16),
                   jax.ShapeDtypeStruct((DEPTH, SSM_CHUNKS, CHUNK_CH, CHUNK_CH), _BF16)],
        name="ssm_prep",
    )(logdt, lre, lim, bre, bim, cre, cim)


def _round_kernel(*refs):
    half = len(refs) // 2
    for src_ref, dst_ref in zip(refs[:half], refs[half:]):
        dst_ref[...] = src_ref[...].astype(_BF16)


def _round_layer0(ws):
    specs = [pl.BlockSpec((1, w.shape[1] // ROUND_CHUNKS, w.shape[2]), lambda j: (0, j, 0)) for w in ws]
    return pl.pallas_call(
        _round_kernel,
        grid=(ROUND_CHUNKS,),
        in_specs=specs,
        out_specs=specs,
        out_shape=[jax.ShapeDtypeStruct((1,) + w.shape[1:], _BF16) for w in ws],
        name="round_mixer_l0",
    )(*ws)


def _mixer_kernel(x_ref, *refs, rb, tb, layer, zero_conv, single_step, n_cast):
    refs = list(refs)
    conv0_hbm = None if zero_conv else refs.pop(0)
    sre0_ref, sim0_ref = refs.pop(0), refs.pop(0)
    prev = [refs.pop(0) for _ in range(3)] if layer > 0 else []
    cast_in = [refs.pop(0) for _ in range(n_cast)]
    (g_ref, w_in_ref, cw_ref, cb_ref, w_out_a_ref,
     abar_ref, b2blk_ref, c2_re_ref, c2_im_ref, fblk_ref, dskip_ref,
     glu_a_ref, glu_b_ref, w_o_ref,
     out_ref, conv_out_hbm, sre_stack_ref, sim_stack_ref, *refs) = refs
    cast_out = [refs.pop(0) for _ in range(n_cast)]
    vbuf, s_re, s_im, conv_carry, conv_sem, *x_dma = refs
    for src_ref, dst_ref in zip(cast_in, cast_out):
        dst_ref[...] = src_ref[...].astype(_BF16)
    rows = rb * tb
    carry_rows = (CONV_WIDTH - 1) * rb
    sre_out_ref = sre_stack_ref.at[layer]
    sim_out_ref = sim_stack_ref.at[layer]

    @pl.when(pl.program_id(0) == 0)
    def _():
        for stack_ref, prev_ref in zip((sre_stack_ref, sim_stack_ref), prev[1:]):
            stack_ref[0:layer] = prev_ref[...]
        sre_out_ref[...] = sre0_ref[...]
        sim_out_ref[...] = sim0_ref[...]

    conv_io = (layer, conv_carry, conv_sem, single_step)
    _conv_state_fetch(conv0_hbm, *conv_io, wait=False)
    if not single_step:
        _conv_state_fetch(conv0_hbm, *conv_io, wait=True)
    if x_dma:
        _fetch_time_major(x_ref, (), *x_dma, tb)
        x = _read_time_major(x_dma[0])
    else:
        x = x_ref[...]
    xn = _rms(x, g_ref[layer:layer + 1, :]).astype(_BF16)

    def proj(lo, hi):
        return _dot(xn, w_in_ref[:, lo:hi])

    u = proj(1536, 2048)
    pair = 2 if tb % 2 == 0 else 1
    steps = tb // pair
    srows = steps * rb
    if pair == 2:
        u4 = u.reshape(steps, 2, rb, SSM_WIDTH)
        u_parts = [u4[:, i].reshape(srows, SSM_WIDTH) for i in range(2)]
    else:
        u_parts = [u]
    ub_parts = [v.astype(_BF16) for v in u_parts]
    s_re[0:rb, :] = sre_out_ref[...]
    s_im[0:rb, :] = sim_out_ref[...]
    for c in range(SSM_CHUNKS):
        ch = slice(c * CHUNK_CH, (c + 1) * CHUNK_CH)
        if pair == 2:
            inc = _dot(jnp.concatenate([ub_parts[0][:, ch], ub_parts[1][:, ch]], axis=1), b2blk_ref[c])
        else:
            inc = _dot(ub_parts[0][:, ch], b2blk_ref[c, CHUNK_CH:2 * CHUNK_CH, :])
        s_re[rb:rb + srows, c * CHUNK_ST:(c + 1) * CHUNK_ST] = inc[:, 0:CHUNK_ST]
        s_im[rb:rb + srows, c * CHUNK_ST:(c + 1) * CHUNK_ST] = inc[:, CHUNK_ST:2 * CHUNK_ST]

    a_row = 2 * (pair - 1)
    for c in range(SSM_CHUNKS):
        cols = slice(c * CHUNK_ST, (c + 1) * CHUNK_ST)
        ar = jnp.broadcast_to(abar_ref[a_row:a_row + 1, cols], (rb, CHUNK_ST))
        ai = jnp.broadcast_to(abar_ref[a_row + 1:a_row + 2, cols], (rb, CHUNK_ST))

        def step(k, carry, cols=cols, ar=ar, ai=ai):
            sr, si = carry
            r0 = pl.multiple_of(rb + k * rb, rb)
            nsr = ar * sr - ai * si + s_re[pl.ds(r0, rb), cols]
            nsi = ar * si + ai * sr + s_im[pl.ds(r0, rb), cols]
            s_re[pl.ds(r0, rb), cols] = nsr
            s_im[pl.ds(r0, rb), cols] = nsi
            return nsr, nsi

        sr, si = lax.fori_loop(0, steps, step, (s_re[0:rb, cols], s_im[0:rb, cols]), unroll=True)
        sre_out_ref[:, cols] = sr
        sim_out_ref[:, cols] = si

    h = proj(0, 512)
    gate_c = proj(1024, 1536)
    if single_step:
        _conv_state_fetch(conv0_hbm, *conv_io, wait=True)
    vbuf[0:rb, :] = conv_carry[0]
    vbuf[rb:carry_rows, :] = conv_carry[1]
    vbuf[carry_rows:carry_rows + rows, :] = gate_c * h
    conv_y = (cb_ref[layer:layer + 1, :]
              + vbuf[0:rows, :] * cw_ref[0:1, :]
              + vbuf[rb:rb + rows, :] * cw_ref[1:2, :]
              + vbuf[2 * rb:2 * rb + rows, :] * cw_ref[2:3, :])
    conv_carry[0] = vbuf[rows:rows + rb, :]
    conv_carry[1] = vbuf[rows + rb:rows + carry_rows, :]
    if single_step:
        _conv_state_store(conv_carry, conv_out_hbm, layer, conv_sem, single_step, wait=False)
    gate_b = proj(512, 1024)
    z_a = _dot((gate_b * conv_y).astype(_BF16), w_out_a_ref[...])
    merged = _sigmoid(proj(2048, 3072)) * z_a

    y_parts = [[] for _ in range(pair)]
    for c in range(SSM_CHUNKS):
        ch = slice(c * CHUNK_CH, (c + 1) * CHUNK_CH)
        cols = slice(c * CHUNK_ST, (c + 1) * CHUNK_ST)
        if pair == 2:
            y2 = (_dot_nt(s_re[:, cols].astype(_BF16), c2_re_ref[c])
                  + _dot_nt(s_im[:, cols].astype(_BF16), c2_im_ref[c]))
            y_parts[0].append(y2[0:srows, CHUNK_CH:2 * CHUNK_CH] + _dot(ub_parts[0][:, ch], fblk_ref[c]))
            y_parts[1].append(y2[rb:rb + srows, 0:CHUNK_CH])
        else:
            y_parts[0].append(
                _dot_nt(s_re[rb:rb + srows, cols].astype(_BF16), c2_re_ref[c, 0:CHUNK_CH, :])
                + _dot_nt(s_im[rb:rb + srows, cols].astype(_BF16), c2_im_ref[c, 0:CHUNK_CH, :]))
    d_skip = dskip_ref[layer:layer + 1, :]
    ssm_parts = [jnp.concatenate(yp, axis=1) + d_skip * up for yp, up in zip(y_parts, u_parts)]
    if pair == 2:
        ssm_y = jnp.stack([v.reshape(steps, rb, SSM_WIDTH) for v in ssm_parts],
                          axis=1).reshape(rows, SSM_WIDTH)
    else:
        ssm_y = ssm_parts[0]
    sy = _gelu(ssm_y).astype(_BF16)
    z_b = _dot(sy, glu_a_ref[...]) * _sigmoid(_dot(sy, glu_b_ref[...]))
    merged = (merged + _sigmoid(proj(3072, 4096)) * z_b).astype(_BF16)
    out_ref[...] = x + _dot(merged, w_o_ref[...])
    if not single_step:
        _conv_state_store(conv_carry, conv_out_hbm, layer, conv_sem, single_step, wait=False)
    _conv_state_store(conv_carry, conv_out_hbm, layer, conv_sem, single_step, wait=True)


def _ffn_kernel(x_ref, p_ref, *refs, rb, tb, layer, zero_conv, single_step, final_norm, p_native,
                out_native):
    refs = list(refs)
    f0_hbm = None if zero_conv else refs.pop(0)
    if layer > 0:
        refs.pop(0)
    (g2_ref, w_up_ref, fcw_ref, fcb_ref, w_down_ref,
     g3_ref, w_pg_ref, w_ple_ref, gf_ref,
     out_ref, fconv_out_hbm,
     fbuf, fcarry, fsem, *dma) = refs
    rows = rb * tb
    carry_rows = (CONV_WIDTH - 1) * rb
    p_dma = [dma.pop(0), dma.pop(0)] if p_native else None
    out_dma = [dma.pop(0), dma.pop(0)] if out_native else None

    conv_io = (layer, fcarry, fsem, single_step)
    _conv_state_fetch(f0_hbm, *conv_io, wait=False)
    if not single_step:
        _conv_state_fetch(f0_hbm, *conv_io, wait=True)
    if p_native:
        _fetch_time_major(p_ref, (layer,), *p_dma, tb)

    x = x_ref[...]
    xn = _rms(x, g2_ref[layer:layer + 1, :]).astype(_BF16)
    acc = x
    if single_step:
        _conv_state_fetch(f0_hbm, *conv_io, wait=True)
    pending = None
    for lo, hi in FF_CHUNKS:
        wc = hi - lo
        up_a = _dot(xn, w_up_ref[:, lo:hi])
        up_b = _dot(xn, w_up_ref[:, D_FF + lo:D_FF + hi])
        if pending is not None:
            acc = acc + _dot(pending[0], w_down_ref[pending[1]:pending[2], :])
        fbuf[0:rb, 0:wc] = fcarry[0, :, lo:hi]
        fbuf[rb:carry_rows, 0:wc] = fcarry[1, :, lo:hi]
        fbuf[carry_rows:carry_rows + rows, 0:wc] = up_a
        conv_a = (fcb_ref[layer:layer + 1, lo:hi]
                  + fbuf[0:rows, 0:wc] * fcw_ref[0:1, lo:hi]
                  + fbuf[rb:rb + rows, 0:wc] * fcw_ref[1:2, lo:hi]
                  + fbuf[2 * rb:2 * rb + rows, 0:wc] * fcw_ref[2:3, lo:hi])
        fcarry[0, :, lo:hi] = fbuf[rows:rows + rb, 0:wc]
        fcarry[1, :, lo:hi] = fbuf[rows + rb:rows + carry_rows, 0:wc]
        pending = ((_gelu(conv_a) * up_b).astype(_BF16), lo, hi)
    acc = acc + _dot(pending[0], w_down_ref[pending[1]:pending[2], :])
    if single_step:
        _conv_state_store(fcarry, fconv_out_hbm, layer, fsem, single_step, wait=False)
    x2 = acc
    xn3 = _rms(x2, g3_ref[layer:layer + 1, :]).astype(_BF16)
    gate = _sigmoid(_dot(xn3, w_pg_ref[...]))
    p = _read_time_major(p_dma[0]) if p_native else p_ref[...]
    pe = _dot(p.astype(_BF16), w_ple_ref[...])
    x3 = x2 + gate * pe
    if final_norm:
        x3 = _rms(x3, gf_ref[...])
    if out_native:
        _store_batch_major(x3, out_ref, *out_dma, tb)
    else:
        out_ref[...] = x3
    if not single_step:
        _conv_state_store(fcarry, fconv_out_hbm, layer, fsem, single_step, wait=False)
    _conv_state_store(fcarry, fconv_out_hbm, layer, fsem, single_step, wait=True)


def _const_spec(shape, layer=None):
    if layer is None:
        return pl.BlockSpec(shape, lambda j: (0,) * len(shape), pipeline_mode=pl.Buffered(1))
    return pl.BlockSpec((None,) + shape, lambda j: (layer,) + (0,) * len(shape),
                        pipeline_mode=pl.Buffered(1))


def _layer_weight(wts, name, layer):
    w = wts[name]
    return w[layer] if isinstance(w, list) else (w, layer)


def _native_scratch(tb, rb, width):
    return [pltpu.VMEM((2, tb, rb, width), _F32), pltpu.SemaphoreType.DMA((2,))]


def _stack_spec(n_layers, rb, width):
    return pl.BlockSpec((n_layers, rb, width), lambda j: (0, 0, 0))


def _conv_state_scratch(rb, width):
    return [pltpu.VMEM((CONV_WIDTH - 1, rb, width), _F32), pltpu.SemaphoreType.DMA((DEPTH, CONV_WIDTH - 1))]


def _mixer_call(x, conv0, ssm0, prev, wts, layer, ssm_layer, rb, tb, x_native, cast=()):
    n = x.shape[0] * x.shape[1] if x_native else x.shape[0]
    rows = rb * tb
    carry_rows = (CONV_WIDTH - 1) * rb
    hbm_spec = pl.BlockSpec(memory_space=pl.ANY)
    row_spec = pl.BlockSpec((rows, D_MODEL), lambda j: (j, 0))
    operands = [x]
    in_specs = [hbm_spec if x_native else row_spec]
    if conv0 is not None:
        operands.append(conv0)
        in_specs.append(hbm_spec)
    operands += list(ssm0)
    in_specs += [_const_spec((rb, SSM_FLAT), ssm_layer)] * 2
    aliases = {}
    if prev:
        aliases[len(operands)] = 1
        operands += list(prev)
        in_specs += [hbm_spec] + [_const_spec((layer, rb, SSM_FLAT))] * 2
    steps = n // rows
    cast_out_specs, cast_shapes = [], []
    for w, first, count in cast:
        per = steps // count
        assert steps % count == 0 and w.shape[1] % (16 * per) == 0, (steps, count, w.shape)
        block = (None, w.shape[1] // per, w.shape[2])
        operands.append(w)
        in_specs.append(pl.BlockSpec(block, lambda j, per=per, first=first: (first + j // per, j % per, 0)))
        cast_out_specs.append(pl.BlockSpec(block, lambda j, per=per: (j // per, j % per, 0)))
        cast_shapes.append(jax.ShapeDtypeStruct((count,) + w.shape[1:], _BF16))
    names = ("w_in", "w_out_a", "w_glu_a", "w_glu_b", "w_o")
    (w_in, i_in), (w_out_a, i_out_a), (w_glu_a, i_glu_a), (w_glu_b, i_glu_b), (w_o, i_o) = [
        _layer_weight(wts, k, layer) for k in names]
    in_specs += [
        _const_spec((DEPTH, D_MODEL)),
        _const_spec((D_MODEL, IN_PROJ_WIDTH), i_in),
        _const_spec((CONV_WIDTH, CONV_A_WIDTH), layer),
        _const_spec((DEPTH, CONV_A_WIDTH)),
        _const_spec((CONV_A_WIDTH, D_MODEL), i_out_a),
        _const_spec((4, SSM_FLAT), layer),
        _const_spec((SSM_CHUNKS, 2 * CHUNK_CH, 2 * CHUNK_ST), layer),
        _const_spec((SSM_CHUNKS, 2 * CHUNK_CH, CHUNK_ST), layer),
        _const_spec((SSM_CHUNKS, 2 * CHUNK_CH, CHUNK_ST), layer),
        _const_spec((SSM_CHUNKS, CHUNK_CH, CHUNK_CH), layer),
        _const_spec((DEPTH, SSM_WIDTH)),
        _const_spec((SSM_WIDTH, D_MODEL), i_glu_a),
        _const_spec((SSM_WIDTH, D_MODEL), i_glu_b),
        _const_spec((D_MODEL, D_MODEL), i_o),
    ]
    out_specs = [row_spec, hbm_spec] + [_stack_spec(layer + 1, rb, SSM_FLAT)] * 2 + cast_out_specs
    out_shape = ([jax.ShapeDtypeStruct((n, D_MODEL), _F32),
                  jax.ShapeDtypeStruct((DEPTH, rb, CONV_WIDTH - 1, CONV_A_WIDTH), _F32)]
                 + [jax.ShapeDtypeStruct((layer + 1, rb, SSM_FLAT), _F32)] * 2 + cast_shapes)
    s_rows = rows // 2 if tb % 2 == 0 else rows
    scratch = [
        pltpu.VMEM((carry_rows + rows, CONV_A_WIDTH), _F32),
        pltpu.VMEM((rb + s_rows, SSM_FLAT), _F32),
        pltpu.VMEM((rb + s_rows, SSM_FLAT), _F32),
    ]
    scratch += _conv_state_scratch(rb, CONV_A_WIDTH)
    if x_native:
        scratch += _native_scratch(tb, rb, D_MODEL)
    return pl.pallas_call(
        functools.partial(_mixer_kernel, rb=rb, tb=tb, layer=layer, zero_conv=conv0 is None,
                          single_step=n == rows, n_cast=len(cast)),
        grid=(n // rows,),
        in_specs=in_specs,
        out_specs=out_specs,
        out_shape=out_shape,
        scratch_shapes=scratch,
        input_output_aliases=aliases,
        compiler_params=pltpu.CompilerParams(
            dimension_semantics=("arbitrary",), vmem_limit_bytes=VMEM_LIMIT_BYTES),
        name=f"mixer_l{layer}_rb{rb}",
    )(*operands,
      wts["norm_mix"], w_in, wts["conv_a_w"], wts["conv_a_b"], w_out_a,
      wts["abar"], wts["b2blk"], wts["c2_re"], wts["c2_im"], wts["fblk"], wts["d_skip"],
      w_glu_a, w_glu_b, w_o)


def _ffn_call(x, p, f0, prev, wts, layer, rb, tb, final_norm, p_native, out_native):
    n = x.shape[0]
    rows = rb * tb
    carry_rows = (CONV_WIDTH - 1) * rb
    hbm_spec = pl.BlockSpec(memory_space=pl.ANY)
    row_spec = pl.BlockSpec((rows, D_MODEL), lambda j: (j, 0))
    operands = [x, p]
    in_specs = [
        row_spec,
        hbm_spec if p_native else pl.BlockSpec((None, rows, PLE_DIM), lambda j: (layer, j, 0)),
    ]
    if f0 is not None:
        operands.append(f0)
        in_specs.append(hbm_spec)
    aliases = {}
    if prev is not None:
        aliases[len(operands)] = 1
        operands.append(prev)
        in_specs.append(hbm_spec)
    in_specs += [
        _const_spec((DEPTH, D_MODEL)),
        _const_spec((D_MODEL, 2 * D_FF), layer),
        _const_spec((CONV_WIDTH, D_FF), layer),
        _const_spec((DEPTH, D_FF)),
        _const_spec((D_FF, D_MODEL), layer),
        _const_spec((DEPTH, D_MODEL)),
        _const_spec((D_MODEL, D_MODEL), layer),
        _const_spec((PLE_DIM, D_MODEL), layer),
        _const_spec((1, D_MODEL)),
    ]
    out_specs = [hbm_spec if out_native else row_spec, hbm_spec]
    out_shape = [jax.ShapeDtypeStruct((rb, n // rb, D_MODEL) if out_native else (n, D_MODEL), _F32),
                 jax.ShapeDtypeStruct((DEPTH, rb, CONV_WIDTH - 1, D_FF), _F32)]
    scratch = [pltpu.VMEM((carry_rows + rows, FF_CHUNK_MAX), _F32)]
    scratch += _conv_state_scratch(rb, D_FF)
    if p_native:
        scratch += _native_scratch(tb, rb, PLE_DIM)
    if out_native:
        scratch += _native_scratch(tb, rb, D_MODEL)
    return pl.pallas_call(
        functools.partial(_ffn_kernel, rb=rb, tb=tb, layer=layer, zero_conv=f0 is None,
                          single_step=n == rows,
                          final_norm=final_norm, p_native=p_native, out_native=out_native),
        grid=(n // rows,),
        in_specs=in_specs,
        out_specs=out_specs,
        out_shape=out_shape,
        scratch_shapes=scratch,
        input_output_aliases=aliases,
        compiler_params=pltpu.CompilerParams(
            dimension_semantics=("arbitrary",), vmem_limit_bytes=VMEM_LIMIT_BYTES),
        name=f"ffn_l{layer}_rb{rb}",
    )(*operands,
      wts["norm_ffn"], wts["w_up"], wts["ffn_conv_w"], wts["ffn_conv_b"], wts["w_down"],
      wts["norm_ple"], wts["w_ple_gate"], wts["w_ple"], wts["norm_final"])


def _run_trunk(x, p, conv0, sre0, sim0, f0, wts, rb, tb_mixer, tb_ffn, native, cast_weights=False):
    mixer_states, ffn_state = [], None
    for layer in range(DEPTH):
        last = layer == DEPTH - 1
        ssm_layer = 0 if conv0 is None else layer
        cast = []
        if cast_weights and layer == 0:
            cast = ([(wts[k], 0, DEPTH) for k in FFN_MATMUL_WEIGHTS]
                    + [(wts[k][1], 1, DEPTH - 1) for k in MIXER_MATMUL_WEIGHTS])
        x, *mixer_states = _mixer_call(x, conv0, (sre0, sim0), mixer_states, wts, layer, ssm_layer,
                                       rb, tb_mixer[layer], x_native=native and layer == 0, cast=cast)
        if cast:
            rounded = mixer_states[3:]
            mixer_states = mixer_states[:3]
            wts.update(zip(FFN_MATMUL_WEIGHTS, rounded))
            for k, w in zip(MIXER_MATMUL_WEIGHTS, rounded[len(FFN_MATMUL_WEIGHTS):]):
                wts[k] = [wts[k][0]] + [(w, i) for i in range(DEPTH - 1)]
        x, ffn_state = _ffn_call(x, p, f0, ffn_state, wts, layer, rb, tb_ffn,
                                 final_norm=last, p_native=native, out_native=native and last)
    return (x, *mixer_states, ffn_state)


def kernel(x_prompt, x_sample, p_prompt, p_sample, state_conv_a, state_ssm_re, state_ssm_im, state_ffn_conv, norm_mix, w_in, conv_a_w, conv_a_b, w_out_a, log_dt, lam_re, lam_im, b_re, b_im, c_re, c_im, d_skip, w_glu_a, w_glu_b, w_o, norm_ffn, w_up, ffn_conv_w, ffn_conv_b, w_down, norm_ple, w_ple_gate, w_ple, norm_final):
    abar, b2blk, c2_re, c2_im, fblk = _ssm_prep(log_dt, lam_re, lam_im, b_re, b_im, c_re, c_im)
    mixer_f32 = (w_in, w_out_a, w_glu_a, w_glu_b, w_o)
    w_in_0, w_out_a_0, w_glu_a_0, w_glu_b_0, w_o_0 = _round_layer0(mixer_f32)
    wts = {
        "norm_mix": norm_mix,
        "w_in": [(w_in_0, 0), w_in],
        "conv_a_w": conv_a_w,
        "conv_a_b": conv_a_b,
        "w_out_a": [(w_out_a_0, 0), w_out_a],
        "abar": abar, "b2blk": b2blk, "c2_re": c2_re, "c2_im": c2_im, "fblk": fblk,
        "d_skip": d_skip,
        "w_glu_a": [(w_glu_a_0, 0), w_glu_a],
        "w_glu_b": [(w_glu_b_0, 0), w_glu_b],
        "w_o": [(w_o_0, 0), w_o],
        "norm_ffn": norm_ffn,
        "w_up": w_up,
        "ffn_conv_w": ffn_conv_w,
        "ffn_conv_b": ffn_conv_b,
        "w_down": w_down,
        "norm_ple": norm_ple,
        "w_ple_gate": w_ple_gate,
        "w_ple": w_ple,
        "norm_final": norm_final.reshape(1, D_MODEL),
    }

    bp = x_prompt.shape[0]
    zs = jnp.zeros((1, bp, SSM_FLAT), _F32)
    y_prompt, conv_p, sre_p, sim_p, ffn_p = _run_trunk(
        x_prompt, p_prompt, None, zs, zs, None, wts, bp, MIXER_TIME_BLOCKS, FFN_TIME_BLOCK,
        native=True, cast_weights=True)

    bs, sseq, _ = x_sample.shape
    xs = x_sample.transpose(1, 0, 2).reshape(sseq * bs, D_MODEL)
    ps = p_sample.transpose(0, 2, 1, 3).reshape(DEPTH, sseq * bs, PLE_DIM)
    ys, conv_s, sre_s, sim_s, ffn_s = _run_trunk(
        xs, ps, state_conv_a,
        state_ssm_re.reshape(DEPTH, bs, SSM_FLAT), state_ssm_im.reshape(DEPTH, bs, SSM_FLAT),
        state_ffn_conv, wts, bs, (sseq,) * DEPTH, sseq, native=False)
    y_sample = ys.reshape(sseq, bs, D_MODEL).transpose(1, 0, 2)

    def ssm_state(s, b):
        return s.reshape(DEPTH, b, SSM_GROUPS, SSM_STATE)

    return (y_prompt, y_sample,
            conv_p, ssm_state(sre_p, bp), ssm_state(sim_p, bp), ffn_p,
            conv_s, ssm_state(sre_s, bs), ssm_state(sim_s, bs), ffn_s)
```
